```python
import jax, jax.numpy as jnp
from jax import lax
import numpy as np

D_MODEL = 1024
BATCH = 1
SEQ = 16384
DEPTH = 2

HEAD_DIM = 64
SB_HEADS = 8
MOBA_HEADS = 8
DSA_HEADS = 16
QBLK = 128
MOBA_BLOCK = 256
MOBA_TOPK = 3
DSA_TOPK = 256
IDX_HEADS = 8
IDX_DIM = 64
N_EXPERTS = 32
TOP_K = 4
D_FF = 1024
SWIGLU_LIMIT = 7.0
SWIGLU_ALPHA = 1.702
MOE_GROUP = 128
PLE_DIM = 256
ROPE_THETA = 10000.0
LN_EPS = 1e-5
DN_ALPHA = (2 * DEPTH) ** 0.25
DN_BETA = (8 * DEPTH) ** -0.25

AB_WIDTH = (SB_HEADS + MOBA_HEADS) * HEAD_DIM
AB_IN = 3 * AB_WIDTH
C_WIDTH = DSA_HEADS * HEAD_DIM
C_IN = 3 * C_WIDTH + IDX_HEADS * IDX_DIM + IDX_DIM + IDX_HEADS
N_EVEN = (DEPTH + 1) // 2
N_ODD = DEPTH // 2

kernel_name = "hybrid_stickbreak_moba_dsa_moe_deepnorm"


def layer_norm(x, g, b):
    xf = x.astype(jnp.float32)
    mu = jnp.mean(xf, axis=-1, keepdims=True)
    var = jnp.mean(jnp.square(xf - mu), axis=-1, keepdims=True)
    return ((xf - mu) * lax.rsqrt(var + LN_EPS) * g + b).astype(x.dtype)


def rope(x, pos):
    d = x.shape[-1]
    half = d // 2
    inv = ROPE_THETA ** (-jnp.arange(half, dtype=jnp.float32) / half)
    ang = pos.astype(jnp.float32)[:, None] * inv[None, :]
    shp = (1, ang.shape[0]) + (1,) * (x.ndim - 3) + (half,)
    cos = jnp.cos(ang).reshape(shp)
    sin = jnp.sin(ang).reshape(shp)
    xf = x.astype(jnp.float32)
    x1, x2 = xf[..., :half], xf[..., half:]
    return jnp.concatenate([x1 * cos - x2 * sin, x2 * cos + x1 * sin], axis=-1).astype(x.dtype)


def stick_breaking_attention(q, k, v):
    B, S, H, Dh = q.shape
    scale = Dh ** -0.5
    kpos = jnp.arange(S)

    def block(i):
        qi = lax.dynamic_slice_in_dim(q, i * QBLK, QBLK, axis=1)
        z = jnp.einsum('bqhd,bshd->bhqs', qi, k).astype(jnp.float32) * scale
        qpos = i * QBLK + jnp.arange(QBLK)
        mask = (kpos[None, :] < qpos[:, None])[None, None]
        log_stay = jnp.where(mask, jax.nn.log_sigmoid(-z), 0.0)
        suffix = lax.cumsum(log_stay, axis=3, reverse=True) - log_stay
        a = jnp.where(mask, jnp.exp(jax.nn.log_sigmoid(z) + suffix), 0.0)
        return jnp.einsum('bhqs,bshd->bqhd', a.astype(v.dtype), v)

    out = lax.map(block, jnp.arange(S // QBLK))
    return out.transpose(1, 0, 2, 3, 4).reshape(B, S, H, Dh)


def moba_attention(q, k, v):
    B, S, H, Dh = q.shape
    scale = Dh ** -0.5
    nb = -(-S // MOBA_BLOCK)
    pad = nb * MOBA_BLOCK - S
    kp = jnp.pad(k, ((0, 0), (0, pad), (0, 0), (0, 0)))
    vp = jnp.pad(v, ((0, 0), (0, pad), (0, 0), (0, 0)))
    kblk = kp.reshape(B, nb, MOBA_BLOCK, H, Dh)
    vblk = vp.reshape(B, nb, MOBA_BLOCK, H, Dh)
    kmean = jnp.mean(kblk.astype(jnp.float32), axis=2).astype(k.dtype)
    kbh = kblk.transpose(0, 3, 1, 2, 4)
    vbh = vblk.transpose(0, 3, 1, 2, 4)
    n_sel = min(MOBA_TOPK, nb - 1)
    bi = jnp.arange(B)[:, None, None, None]
    hi = jnp.arange(H)[None, :, None, None]
    own_off = jnp.arange(MOBA_BLOCK)

    def block(i):
        q0 = i * QBLK
        qi = lax.dynamic_slice_in_dim(q, q0, QBLK, axis=1)
        qh = qi.transpose(0, 2, 1, 3)
        qpos = q0 + jnp.arange(QBLK)
        cur = q0 // MOBA_BLOCK
        k_own = lax.dynamic_index_in_dim(kblk, cur, axis=1, keepdims=False)
        v_own = lax.dynamic_index_in_dim(vblk, cur, axis=1, keepdims=False)
        s_own = jnp.einsum('bhqd,bkhd->bhqk', qh, k_own).astype(jnp.float32) * scale
        own_pos = cur * MOBA_BLOCK + own_off
        s_own = jnp.where((own_pos[None, :] <= qpos[:, None])[None, None], s_own, -jnp.inf)
        if n_sel == 0:
            pr = jax.nn.softmax(s_own, axis=-1)
            return jnp.einsum('bhqk,bkhd->bqhd', pr.astype(v.dtype), v_own)
        gate = jnp.einsum('bhqd,bnhd->bhqn', qh, kmean).astype(jnp.float32)
        gate = jnp.where((jnp.arange(nb) < cur)[None, None, None], gate, -jnp.inf)
        _, g_idx = lax.top_k(gate, n_sel)
        valid = g_idx < cur
        k_sel = kbh[bi, hi, g_idx]
        v_sel = vbh[bi, hi, g_idx]
        s_sel = jnp.einsum('bhqd,bhqjkd->bhqjk', qh, k_sel).astype(jnp.float32) * scale
        s_sel = jnp.where(valid[..., None], s_sel, -jnp.inf).reshape(B, H, QBLK, n_sel * MOBA_BLOCK)
        pr = jax.nn.softmax(jnp.concatenate([s_own, s_sel], axis=-1), axis=-1).astype(v.dtype)
        p_own = pr[..., :MOBA_BLOCK]
        p_sel = pr[..., MOBA_BLOCK:].reshape(B, H, QBLK, n_sel, MOBA_BLOCK)
        return (jnp.einsum('bhqk,bkhd->bqhd', p_own, v_own)
                + jnp.einsum('bhqjk,bhqjkd->bqhd', p_sel, v_sel))

    out = lax.map(block, jnp.arange(S // QBLK))
    return out.transpose(1, 0, 2, 3, 4).reshape(B, S, H, Dh)


def dsa_attention(q, k, v, qi, ki, wi):
    B, S, H, Dh = q.shape
    scale = Dh ** -0.5
    idx_scale = IDX_DIM ** -0.5
    topk = min(DSA_TOPK, S // 4)
    kpos = jnp.arange(S)
    bi = jnp.arange(B)[:, None, None]

    def block(i):
        q0 = i * QBLK
        qb = lax.dynamic_slice_in_dim(q, q0, QBLK, axis=1)
        qib = lax.dynamic_slice_in_dim(qi, q0, QBLK, axis=1)
        wib = lax.dynamic_slice_in_dim(wi, q0, QBLK, axis=1).astype(jnp.float32)
        qpos = q0 + jnp.arange(QBLK)
        logits = jnp.einsum('bqhd,bsd->bqhs', qib, ki).astype(jnp.float32) * idx_scale
        score = jnp.einsum('bqh,bqhs->bqs', wib, jax.nn.relu(logits))
        causal = (kpos[None, :] <= qpos[:, None])[None]
        score = jnp.where(causal, score, -jnp.inf)
        _, sel = lax.top_k(score, topk)
        valid = sel <= qpos[None, :, None]
        k_sel = k[bi, sel]
        v_sel = v[bi, sel]
        s = jnp.einsum('bqhd,bqkhd->bhqk', qb, k_sel).astype(jnp.float32) * scale
        s = jnp.where(valid[:, None], s, -jnp.inf)
        pr = jax.nn.softmax(s, axis=-1).astype(v.dtype)
        return jnp.einsum('bhqk,bqkhd->bqhd', pr, v_sel)

    out = lax.map(block, jnp.arange(S // QBLK))
    return out.transpose(1, 0, 2, 3, 4).reshape(B, S, H, Dh)


def mixer_ab(x, w_in, w_out):
    B, S, _ = x.shape
    pos = jnp.arange(S)
    h = x @ w_in
    qa, ka, va, qb, kb, vb = jnp.split(h, 6, axis=-1)
    shp_a = (B, S, SB_HEADS, HEAD_DIM)
    shp_b = (B, S, MOBA_HEADS, HEAD_DIM)
    oa = stick_breaking_attention(qa.reshape(shp_a), ka.reshape(shp_a), va.reshape(shp_a))
    ob = moba_attention(rope(qb.reshape(shp_b), pos), rope(kb.reshape(shp_b), pos), vb.reshape(shp_b))
    o = jnp.concatenate([oa.reshape(B, S, -1), ob.reshape(B, S, -1)], axis=-1)
    return o @ w_out


def mixer_c(x, w_in, w_out):
    B, S, _ = x.shape
    pos = jnp.arange(S)
    h = x @ w_in
    cuts = np.cumsum([C_WIDTH, C_WIDTH, C_WIDTH, IDX_HEADS * IDX_DIM, IDX_DIM]).tolist()
    q, k, v, qi, ki, wi = jnp.split(h, cuts, axis=-1)
    shp = (B, S, DSA_HEADS, HEAD_DIM)
    q = rope(q.reshape(shp), pos)
    k = rope(k.reshape(shp), pos)
    qi = rope(qi.reshape(B, S, IDX_HEADS, IDX_DIM), pos)
    ki = rope(ki, pos)
    wi = wi * IDX_HEADS ** -0.5
    o = dsa_attention(q, k, v.reshape(shp), qi, ki, wi)
    return o.reshape(B, S, -1) @ w_out


def moe(x, w_r, b_r, w1, b1, w2, b2):
    B, S, D = x.shape
    T = B * S
    x2 = x.reshape(T, D)
    logits = (x2 @ w_r + b_r).astype(jnp.float32)
    top_val, top_idx = lax.top_k(logits, TOP_K)
    gates = jax.nn.softmax(top_val, axis=-1)
    n_assign = T * TOP_K
    flat_e = top_idx.reshape(-1)
    flat_tok = jnp.arange(n_assign) // TOP_K
    order = jnp.argsort(flat_e)
    sorted_e = flat_e[order]
    sorted_tok = flat_tok[order]
    gate_sorted = gates.reshape(-1)[order].astype(x.dtype)
    counts = jnp.bincount(flat_e, length=N_EXPERTS)
    starts = jnp.cumsum(counts) - counts
    pcounts = (counts + MOE_GROUP - 1) // MOE_GROUP * MOE_GROUP
    pends = jnp.cumsum(pcounts)
    pstarts = pends - pcounts
    dest = pstarts[sorted_e] + (jnp.arange(n_assign) - starts[sorted_e])
    P = n_assign + N_EXPERTS * MOE_GROUP
    n_grp = P // MOE_GROUP
    buf = jnp.zeros((P, D), x.dtype).at[dest].set(x2[sorted_tok])
    grp_e = jnp.minimum(jnp.searchsorted(pends, jnp.arange(n_grp) * MOE_GROUP, side='right'), N_EXPERTS - 1)

    def expert_group(args):
        xb, e = args
        hgu = xb @ w1[e] + b1[e]
        g = jnp.minimum(hgu[:, 0::2], SWIGLU_LIMIT)
        u = jnp.clip(hgu[:, 1::2], -SWIGLU_LIMIT, SWIGLU_LIMIT)
        glu = g * jax.nn.sigmoid(SWIGLU_ALPHA * g)
        return ((u + 1.0) * glu) @ w2[e] + b2[e]

    out = lax.map(expert_group, (buf.reshape(n_grp, MOE_GROUP, D), grp_e)).reshape(P, D)
    y_sorted = out[dest] * gate_sorted[:, None]
    y = jax.ops.segment_sum(y_sorted, sorted_tok, num_segments=T)
    return y.reshape(B, S, D)


def setup_inputs(seed: int = 0) -> dict:
    key = jax.random.key(seed)
    ks = jax.random.split(key, 17)
    nrm = jax.random.normal
    f32 = jnp.float32
    return {
        "x": nrm(ks[0], (BATCH, SEQ, D_MODEL), f32),
        "p": nrm(ks[1], (DEPTH, BATCH, SEQ, PLE_DIM), f32),
        "ab_w_in": nrm(ks[2], (N_EVEN, D_MODEL, AB_IN), f32) * D_MODEL ** -0.5,
        "ab_w_out": nrm(ks[3], (N_EVEN, AB_WIDTH, D_MODEL), f32) * (AB_WIDTH ** -0.5 * DN_BETA),
        "c_w_in": nrm(ks[4], (N_ODD, D_MODEL, C_IN), f32) * D_MODEL ** -0.5,
        "c_w_out": nrm(ks[5], (N_ODD, C_WIDTH, D_MODEL), f32) * (C_WIDTH ** -0.5 * DN_BETA),
        "ln_g": 1.0 + 0.01 * nrm(ks[6], (DEPTH, 2, D_MODEL), f32),
        "ln_b": 0.01 * nrm(ks[7], (DEPTH, 2, D_MODEL), f32),
        "router_w": nrm(ks[8], (DEPTH, D_MODEL, N_EXPERTS), f32) * D_MODEL ** -0.5,
        "router_b": 0.01 * nrm(ks[9], (DEPTH, N_EXPERTS), f32),
        "moe_w1": nrm(ks[10], (DEPTH, N_EXPERTS, D_MODEL, 2 * D_FF), f32) * D_MODEL ** -0.5,
        "moe_b1": 0.01 * nrm(ks[11], (DEPTH, N_EXPERTS, 2 * D_FF), f32),
        "moe_w2": nrm(ks[12], (DEPTH, N_EXPERTS, D_FF, D_MODEL), f32) * (D_FF ** -0.5 * DN_BETA),
        "moe_b2": 0.01 * nrm(ks[13], (DEPTH, N_EXPERTS, D_MODEL), f32),
        "ple_w_proj": nrm(ks[14], (DEPTH, PLE_DIM, D_MODEL), f32) * PLE_DIM ** -0.5,
        "ple_w_gate": nrm(ks[15], (DEPTH, D_MODEL, D_MODEL), f32) * D_MODEL ** -0.5,
    }


def reference(x, p, ab_w_in, ab_w_out, c_w_in, c_w_out, ln_g, ln_b, router_w, router_b,
              moe_w1, moe_b1, moe_w2, moe_b2, ple_w_proj, ple_w_gate):
    for i in range(DEPTH):
        j = i // 2
        if i % 2 == 0:
            m = mixer_ab(x, ab_w_in[j], ab_w_out[j])
        else:
            m = mixer_c(x, c_w_in[j], c_w_out[j])
        x = layer_norm(DN_ALPHA * x + m, ln_g[i, 0], ln_b[i, 0])
        f = moe(x, router_w[i], router_b[i], moe_w1[i], moe_b1[i], moe_w2[i], moe_b2[i])
        x = layer_norm(DN_ALPHA * x + f, ln_g[i, 1], ln_b[i, 1])
        x = x + jax.nn.sigmoid(x @ ple_w_gate[i]) * (p[i] @ ple_w_proj[i])
    return x
```

```python
import functools

import jax
import jax.numpy as jnp
import numpy as np
from jax import lax
from jax.experimental import pallas as pl
from jax.experimental.pallas import tpu as pltpu

F32 = jnp.float32
BF16 = jnp.bfloat16
I32 = jnp.int32

HEAD_DIM = 64
LANES = 128
MOBA_BLOCK = 256
MOBA_TOPK = 3
DSA_TOPK = 256
IDX_HEADS = 8
N_EXPERTS = 32
TOP_K = 4
SWIGLU_LIMIT = 7.0
SWIGLU_ALPHA = 1.702
ROPE_THETA = 10000.0
LN_EPS = 1e-5
NEG = -1e30
EXP_FLUSH = -88.0
INT_MIN = -(2 ** 31)
VMEM_LIMIT = 56 * 1024 * 1024


def _cparams(sem):
    return pltpu.CompilerParams(dimension_semantics=sem, vmem_limit_bytes=VMEM_LIMIT)


def _dot(a, b):
    return jnp.dot(a, b, preferred_element_type=F32)


def _dot_nt(a, b):
    return lax.dot_general(a, b, (((1,), (1,)), ((), ())), preferred_element_type=F32)


def _dot_tn(a, b):
    return lax.dot_general(a, b, (((0,), (0,)), ((), ())), preferred_element_type=F32)


def _iota(shape, dim):
    return lax.broadcasted_iota(I32, shape, dim)


def _proj_kernel(x_ref, w_ref, scale_ref, cos_ref, sin_ref, o_ref, *, rope_ranges, tn):
    j = pl.program_id(1)
    h = _dot(x_ref[...].astype(BF16), w_ref[...]) * scale_ref[...]
    is_rope = functools.reduce(jnp.logical_or, [(j >= a) & (j < b) for a, b in rope_ranges])

    @pl.when(is_rope)
    def _():
        lane = _iota(h.shape, 1)
        first_half = (lane % HEAD_DIM) < (HEAD_DIM // 2)
        partner = jnp.where(first_half, pltpu.roll(h, tn - HEAD_DIM // 2, 1),
                            pltpu.roll(h, HEAD_DIM // 2, 1))
        o_ref[...] = (h * cos_ref[...] + partner * sin_ref[...]).astype(o_ref.dtype)

    @pl.when(jnp.logical_not(is_rope))
    def _():
        o_ref[...] = h.astype(o_ref.dtype)


def _proj(x, w, scale, cos_t, sin_t, rope_ranges, *, tm=512, tn=256):
    s, d = x.shape
    n = w.shape[1]
    return pl.pallas_call(
        functools.partial(_proj_kernel, rope_ranges=rope_ranges, tn=tn),
        grid=(s // tm, n // tn),
        in_specs=[
            pl.BlockSpec((tm, d), lambda i, j: (i, 0)),
            pl.BlockSpec((d, tn), lambda i, j: (0, j)),
            pl.BlockSpec((1, tn), lambda i, j: (0, j)),
            pl.BlockSpec((tm, tn), lambda i, j: (i, 0)),
            pl.BlockSpec((tm, tn), lambda i, j: (i, 0)),
        ],
        out_specs=pl.BlockSpec((tm, tn), lambda i, j: (i, j)),
        out_shape=jax.ShapeDtypeStruct((s, n), BF16),
        compiler_params=_cparams(("arbitrary", "arbitrary")),
        name="in_proj",
    )(x, w, scale, cos_t, sin_t)


def _rope_tables(s, width):
    half = HEAD_DIM // 2
    inv = ROPE_THETA ** (-jnp.arange(half, dtype=F32) / half)
    ang = jnp.arange(s, dtype=F32)[:, None] * inv[None, :]
    cos = jnp.cos(ang)
    sin = jnp.sin(ang)
    reps = width // HEAD_DIM
    cos_t = jnp.tile(jnp.concatenate([cos, cos], axis=1), (1, reps))
    sin_t = jnp.tile(jnp.concatenate([-sin, sin], axis=1), (1, reps))
    return cos_t, sin_t


def _sb_kernel(q_ref, k_ref, v_ref, o_ref, acc_ref, run_ref, *, t):
    i = pl.program_id(1)
    q = q_ref[...]
    lane = _iota((t, LANES), 1)
    lo = lane < HEAD_DIM
    zero = jnp.zeros_like(q)
    qh = (jnp.where(lo, q, zero), jnp.where(lo, zero, q))
    row = _iota((t, t), 0)
    col = _iota((t, t), 1)
    incl = (row >= col).astype(BF16)
    ones = jnp.ones((t, LANES), BF16)
    strict = col < row

    acc_ref[...] = jnp.zeros_like(acc_ref)
    run_ref[...] = jnp.zeros_like(run_ref)

    def tile(kt, diagonal):
        start = pl.multiple_of(kt * t, t)
        kblk = k_ref[pl.ds(start, t), :]
        vblk = v_ref[pl.ds(start, t), :]
        for h in range(2):
            z = _dot_nt(qh[h], kblk)
            log_stay = -(jnp.maximum(z, 0.0) + jnp.log1p(jnp.exp(-jnp.abs(z))))
            if diagonal:
                log_stay = jnp.where(strict, log_stay, 0.0)
            ls16 = log_stay.astype(BF16)
            cum = _dot(ls16, incl)
            a = jnp.exp(z + cum + run_ref[h])
            if diagonal:
                a = jnp.where(strict, a, 0.0)
            acc_ref[h] += _dot(a.astype(BF16), vblk)
            run_ref[h] += _dot(ls16, ones)

    tile(i, True)

    def cond(c):
        kt, live = c
        return jnp.logical_and(kt >= 0, live)

    def body(c):
        kt, _ = c
        tile(kt, False)
        live = jnp.max(jnp.maximum(run_ref[0], run_ref[1])) > EXP_FLUSH
        return kt - 1, live

    lax.while_loop(cond, body, (i - 1, jnp.bool_(True)))
    o_ref[...] = jnp.where(lo, acc_ref[0], acc_ref[1]).astype(o_ref.dtype)


def _sb_attention(h0, *, q_col, k_col, v_col, n_pairs, t=128):
    s = h0.shape[0]
    return pl.pallas_call(
        functools.partial(_sb_kernel, t=t),
        grid=(n_pairs, s // t),
        in_specs=[
            pl.BlockSpec((t, LANES), lambda p, i: (i, q_col + p)),
            pl.BlockSpec((s, LANES), lambda p, i: (0, k_col + p)),
            pl.BlockSpec((s, LANES), lambda p, i: (0, v_col + p)),
        ],
        out_specs=pl.BlockSpec((t, LANES), lambda p, i: (i, p)),
        out_shape=jax.ShapeDtypeStruct((s, n_pairs * LANES), BF16),
        scratch_shapes=[pltpu.VMEM((2, t, LANES), F32), pltpu.VMEM((2, t, LANES), F32)],
        compiler_params=_cparams(("arbitrary", "arbitrary")),
        name="stick_breaking",
    )(h0, h0, h0)


def _moba_prep_kernel(k_ref, kaug_ref, kmean_ref):
    b = pl.program_id(1)
    k = k_ref[...]
    kmean_ref[...] = jnp.mean(k.astype(F32), axis=0, keepdims=True)[None]
    onehot = (_iota(k.shape, 1) == b).astype(BF16)
    kaug_ref[...] = jnp.concatenate([k, onehot], axis=1)


def _moba_prep(h0, *, k_col, n_pairs):
    s = h0.shape[0]
    nb = s // MOBA_BLOCK
    return pl.pallas_call(
        _moba_prep_kernel,
        grid=(n_pairs, nb),
        in_specs=[pl.BlockSpec((MOBA_BLOCK, LANES), lambda p, b: (b, k_col + p))],
        out_specs=[
            pl.BlockSpec((MOBA_BLOCK, 2 * LANES), lambda p, b: (b, p)),
            pl.BlockSpec((1, 1, LANES), lambda p, b: (p * nb + b, 0, 0)),
        ],
        out_shape=[
            jax.ShapeDtypeStruct((s, n_pairs * 2 * LANES), BF16),
            jax.ShapeDtypeStruct((n_pairs * nb, 1, LANES), F32),
        ],
        compiler_params=_cparams(("arbitrary", "arbitrary")),
        name="moba_prep",
    )(h0)


def _first_argmax(x, lane):
    m = jnp.max(x, axis=1, keepdims=True)
    idx = jnp.min(jnp.where(x == m, lane, float(LANES)), axis=1, keepdims=True)
    return m, idx


def _moba_kernel(q_ref, kaug_ref, v_ref, kmean_ref, o_ref, m_ref, l_ref, acc_ref, *, nb):
    t = MOBA_BLOCK
    i = pl.program_id(1)
    q = q_ref[...]
    lane = _iota((t, LANES), 1)
    lo = lane < HEAD_DIM
    zero = jnp.zeros_like(q)
    qh = (jnp.where(lo, q, zero), jnp.where(lo, zero, q))
    kmean = kmean_ref[...]
    km_hi = kmean.astype(BF16)
    km_lo = (kmean - km_hi.astype(F32)).astype(BF16)
    causal = _iota((t, t), 1) <= _iota((t, t), 0)

    q_aug = []
    for h in range(2):
        gate = _dot_nt(qh[h], km_hi) + _dot_nt(qh[h], km_lo)
        gate = jnp.where(lane < i, gate, -jnp.inf)
        chosen = lane == i
        lane_f = lane.astype(F32)
        for _ in range(MOBA_TOPK):
            m, idx = _first_argmax(gate, lane_f)
            pick = (lane_f == idx) & (m > -jnp.inf)
            chosen = chosen | pick
            gate = jnp.where(lane_f == idx, -jnp.inf, gate)
        bias = jnp.where(chosen | (lane >= nb), 0.0, NEG).astype(BF16)
        q_aug.append(jnp.concatenate([qh[h], bias], axis=1))

    def scores(kt):
        start = pl.multiple_of(kt * t, t)
        kblk = kaug_ref[pl.ds(start, t), :]
        vblk = v_ref[pl.ds(start, t), :]
        return [(_dot_nt(q_aug[h], kblk), vblk) for h in range(2)]

    for h, (s, vblk) in enumerate(scores(i)):
        s = jnp.where(causal, s, NEG)
        m = jnp.max(s, axis=1, keepdims=True)
        p = jnp.exp(s - m)
        m_ref[h] = m
        l_ref[h] = jnp.sum(p, axis=1, keepdims=True)
        acc_ref[h] = _dot(p.astype(BF16), vblk)

    def body(kt, carry):
        for h, (s, vblk) in enumerate(scores(kt)):
            m_old = m_ref[h]
            m_new = jnp.maximum(m_old, jnp.max(s, axis=1, keepdims=True))
            alpha = jnp.exp(m_old - m_new)
            p = jnp.exp(s - m_new)
            m_ref[h] = m_new
            l_ref[h] = alpha * l_ref[h] + jnp.sum(p, axis=1, keepdims=True)
            acc_ref[h] = alpha * acc_ref[h] + _dot(p.astype(BF16), vblk)
        return carry

    lax.fori_loop(0, i, body, 0)
    out0 = acc_ref[0] / l_ref[0]
    out1 = acc_ref[1] / l_ref[1]
    o_ref[...] = jnp.where(lo, out0, out1).astype(o_ref.dtype)


def _moba_attention(h0, kaug, kmean, *, q_col, v_col, n_pairs):
    s = h0.shape[0]
    t = MOBA_BLOCK
    nb = s // t
    assert nb <= LANES
    return pl.pallas_call(
        functools.partial(_moba_kernel, nb=nb),
        grid=(n_pairs, nb),
        in_specs=[
            pl.BlockSpec((t, LANES), lambda p, i: (i, q_col + p)),
            pl.BlockSpec((s, 2 * LANES), lambda p, i: (0, p)),
            pl.BlockSpec((s, LANES), lambda p, i: (0, v_col + p)),
            pl.BlockSpec((LANES, LANES), lambda p, i: (p, 0)),
        ],
        out_specs=pl.BlockSpec((t, LANES), lambda p, i: (i, p)),
        out_shape=jax.ShapeDtypeStruct((s, n_pairs * LANES), BF16),
        scratch_shapes=[pltpu.VMEM((2, t, 1), F32), pltpu.VMEM((2, t, 1), F32),
                        pltpu.VMEM((2, t, LANES), F32)],
        compiler_params=_cparams(("arbitrary", "arbitrary")),
        name="moba",
    )(h0, kaug, h0, kmean)


def _dsa_kernel(qi_ref, wi_ref, ki_ref, q_ref, k_ref, v_ref, o_ref,
                key_ref, thr_ref, need_ref, seen_ref, m_ref, l_ref, acc_ref,
                *, t, n_heads, topk):
    i = pl.program_id(0)
    j = pl.program_id(1)
    lane = _iota((t, LANES), 1)
    lo = lane < HEAD_DIM
    causal = _iota((t, t), 1) <= _iota((t, t), 0)

    @pl.when(j == 0)
    def _select():
        qi = qi_ref[...]
        wi = wi_ref[...].astype(F32)
        zero = jnp.zeros((t, LANES), BF16)
        qih, wih = [], []
        for h in range(IDX_HEADS):
            pair = qi[:, (h // 2) * LANES:(h // 2 + 1) * LANES]
            qih.append(jnp.where(lo, pair, zero) if h % 2 == 0 else jnp.where(lo, zero, pair))
            wih.append(jnp.sum(jnp.where(lane == h, wi, 0.0), axis=1, keepdims=True))

        def score_tile(kt, carry):
            start = pl.multiple_of(kt * t, t)
            kblk = ki_ref[pl.ds(start, t), :]
            sc = jnp.zeros((t, t), F32)
            for h in range(IDX_HEADS):
                sc = sc + wih[h] * jnp.maximum(_dot_nt(qih[h], kblk), 0.0)
            bits = pltpu.bitcast(sc, I32)
            key = jnp.where(bits >= 0, bits, bits ^ jnp.int32(0x7FFFFFFF))
            key = jnp.where(jnp.logical_or(kt < i, causal), key, jnp.int32(INT_MIN))
            key_ref[kt] = key
            return carry

        lax.fori_loop(0, i + 1, score_tile, 0)

        def count(pred):
            def cbody(kt, c):
                hit = jnp.where(pred(key_ref[kt]), 1.0, 0.0)
                part = hit[:, :LANES]
                for g in range(1, t // LANES):
                    part = part + hit[:, g * LANES:(g + 1) * LANES]
                return c + part
            c = lax.fori_loop(0, i + 1, cbody, jnp.zeros((t, LANES), F32))
            return jnp.sum(c, axis=1, keepdims=True)

        thr_u = jnp.zeros((t, 1), I32)
        for b in range(31, -1, -1):
            bit = jnp.int32(INT_MIN) if b == 31 else jnp.int32(1 << b)
            cand_u = thr_u | bit
            cand_s = cand_u ^ jnp.int32(INT_MIN)
            cnt = count(lambda kk, c=cand_s: kk >= c)
            thr_u = jnp.where(cnt >= float(topk), cand_u, thr_u)
        thr = thr_u ^ jnp.int32(INT_MIN)
        n_gt = count(lambda kk: kk > thr)
        thr_ref[...] = thr
        need_ref[...] = float(topk) - n_gt
        seen_ref[...] = jnp.zeros_like(seen_ref)
        m_ref[...] = jnp.full(m_ref.shape, NEG, F32)
        l_ref[...] = jnp.zeros_like(l_ref)
        acc_ref[...] = jnp.zeros_like(acc_ref)

    @pl.when(j <= i)
    def _attend():
        key = key_ref[j]
        thr = thr_ref[...]
        eq = key == thr
        eq16 = jnp.where(eq, 1.0, 0.0).astype(BF16)
        before = (_iota((t, t), 0) < _iota((t, t), 1)).astype(BF16)
        rank = seen_ref[...] + _dot(eq16, before)
        sel = (key > thr) | (eq & (rank < need_ref[...]))
        sel = sel & jnp.logical_or(j < i, causal)
        seen_ref[...] += jnp.sum(jnp.where(eq, 1.0, 0.0), axis=1, keepdims=True)
        bias = jnp.where(sel, 0.0, NEG)

        q = q_ref[...]
        kb = k_ref[...]
        vb = v_ref[...]
        zero = jnp.zeros((t, LANES), BF16)
        for p in range(n_heads // 2):
            sl = slice(p * LANES, (p + 1) * LANES)
            qp, kp, vp = q[:, sl], kb[:, sl], vb[:, sl]
            outs, alphas = [], []
            for hh in range(2):
                h = 2 * p + hh
                qm = jnp.where(lo, qp, zero) if hh == 0 else jnp.where(lo, zero, qp)
                s = _dot_nt(qm, kp) + bias
                m_old = m_ref[h]
                m_new = jnp.maximum(m_old, jnp.max(s, axis=1, keepdims=True))
                alpha = jnp.exp(m_old - m_new)
                pr = jnp.exp(s - m_new)
                m_ref[h] = m_new
                l_ref[h] = alpha * l_ref[h] + jnp.sum(pr, axis=1, keepdims=True)
                outs.append(_dot(pr.astype(BF16), vp))
                alphas.append(alpha)
            upd = jnp.where(lo, outs[0], outs[1])
            al = jnp.where(lo, alphas[0], alphas[1])
            acc_ref[:, sl] = al * acc_ref[:, sl] + upd

    @pl.when(j == i)
    def _finish():
        for p in range(n_heads // 2):
            sl = slice(p * LANES, (p + 1) * LANES)
            linv = jnp.where(lo, 1.0 / l_ref[2 * p], 1.0 / l_ref[2 * p + 1])
            o_ref[:, sl] = (acc_ref[:, sl] * linv).astype(o_ref.dtype)


def _dsa_attention(h1, *, n_heads, t=256):
    s = h1.shape[0]
    width = n_heads * HEAD_DIM
    nt = s // t
    topk = min(DSA_TOPK, s // 4)
    qi_blk = 3 * width // (IDX_HEADS * HEAD_DIM)
    ki_blk = (3 * width + IDX_HEADS * HEAD_DIM) // LANES
    wi_blk = ki_blk + 2
    return pl.pallas_call(
        functools.partial(_dsa_kernel, t=t, n_heads=n_heads, topk=topk),
        grid=(nt, nt),
        in_specs=[
            pl.BlockSpec((t, IDX_HEADS * HEAD_DIM), lambda i, j: (i, qi_blk)),
            pl.BlockSpec((t, LANES), lambda i, j: (i, wi_blk)),
            pl.BlockSpec((s, LANES), lambda i, j: (0, ki_blk)),
            pl.BlockSpec((t, width), lambda i, j: (i, 0)),
            pl.BlockSpec((t, width), lambda i, j: (jnp.minimum(j, i), 1)),
            pl.BlockSpec((t, width), lambda i, j: (jnp.minimum(j, i), 2)),
        ],
        out_specs=pl.BlockSpec((t, width), lambda i, j: (i, 0)),
        out_shape=jax.ShapeDtypeStruct((s, width), BF16),
        scratch_shapes=[
            pltpu.VMEM((nt, t, t), I32),
            pltpu.VMEM((t, 1), I32),
            pltpu.VMEM((t, 1), F32),
            pltpu.VMEM((t, 1), F32),
            pltpu.VMEM((n_heads, t, 1), F32),
            pltpu.VMEM((n_heads, t, 1), F32),
            pltpu.VMEM((t, width), F32),
        ],
        compiler_params=_cparams(("arbitrary", "arbitrary")),
        name="dsa",
    )(h1, h1, h1, h1, h1, h1)


def _layer_norm(y, g, b):
    mu = jnp.mean(y, axis=-1, keepdims=True)
    yc = y - mu
    var = jnp.mean(yc * yc, axis=-1, keepdims=True)
    return yc * lax.rsqrt(var + LN_EPS) * g + b


def _outproj_kernel(*refs, n_in, alpha):
    o_refs = refs[:n_in]
    w_refs = refs[n_in:2 * n_in]
    x_ref, g_ref, b_ref, out_ref = refs[2 * n_in:]
    y = alpha * x_ref[...]
    for o_ref, w_ref in zip(o_refs, w_refs):
        y = y + _dot(o_ref[...], w_ref[...])
    out_ref[...] = _layer_norm(y, g_ref[...], b_ref[...])


def _outproj_ln(os_, ws, x, g, b, alpha, *, tm=256):
    s, d = x.shape
    n_in = len(os_)
    in_specs = ([pl.BlockSpec((tm, o.shape[1]), lambda i: (i, 0)) for o in os_]
                + [pl.BlockSpec(w.shape, lambda i: (0, 0)) for w in ws]
                + [pl.BlockSpec((tm, d), lambda i: (i, 0)),
                   pl.BlockSpec((1, d), lambda i: (0, 0)),
                   pl.BlockSpec((1, d), lambda i: (0, 0))])
    return pl.pallas_call(
        functools.partial(_outproj_kernel, n_in=n_in, alpha=alpha),
        grid=(s // tm,),
        in_specs=in_specs,
        out_specs=pl.BlockSpec((tm, d), lambda i: (i, 0)),
        out_shape=jax.ShapeDtypeStruct((s, d), F32),
        compiler_params=_cparams(("arbitrary",)),
        name="out_proj_ln",
    )(*os_, *ws, x, g, b)


def _moe_kernel(x_ref, wr_ref, br_ref, w1g_ref, w1u_ref, b1g_ref, b1u_ref, w2_ref, b2_ref,
                g_ref, b_ref, out_ref,
                xb_ref, rank_ref, rank_t_ref, gate_t_ref, cnt_ref, acc_ref,
                *, tt, ch, alpha):
    e = pl.program_id(1)
    n_e = pl.num_programs(1)

    @pl.when(e == 0)
    def _route():
        x = x_ref[...]
        xb_ref[...] = x.astype(BF16)
        logits = lax.dot_general(wr_ref[...], x, (((1,), (1,)), ((), ())),
                                 precision=lax.Precision.HIGHEST,
                                 preferred_element_type=F32) + br_ref[...]
        row = _iota(logits.shape, 0).astype(F32)
        work = jnp.where(row < N_EXPERTS, logits, -jnp.inf)
        chosen = row < 0
        top = None
        for r in range(TOP_K):
            m = jnp.max(work, axis=0, keepdims=True)
            idx = jnp.min(jnp.where(work == m, row, float(LANES)), axis=0, keepdims=True)
            pick = row == idx
            chosen = chosen | pick
            work = jnp.where(pick, -jnp.inf, work)
            if r == 0:
                top = m
        ex = jnp.where(chosen, jnp.exp(logits - top), 0.0)
        gate = ex / jnp.sum(ex, axis=0, keepdims=True)
        sel16 = jnp.where(chosen, 1.0, 0.0).astype(BF16)
        before = (_iota((tt, tt), 0) < _iota((tt, tt), 1)).astype(BF16)
        pos = _dot(sel16, before)
        rank = jnp.where(chosen, pos, -1.0)
        rank_ref[...] = rank
        rank_t_ref[...] = rank.T
        gate_t_ref[...] = gate.T
        cnt = jnp.sum(jnp.where(chosen, 1.0, 0.0), axis=1, keepdims=True)
        cnt_ref[...] = jnp.broadcast_to(cnt, cnt_ref.shape)
        acc_ref[...] = jnp.zeros_like(acc_ref)

    lane = _iota((tt, LANES), 1)
    on_e = lane == e
    rank_col = jnp.sum(jnp.where(on_e, rank_t_ref[...], 0.0), axis=1, keepdims=True)
    gate_col = jnp.sum(jnp.where(on_e, gate_t_ref[...], 0.0), axis=1, keepdims=True)
    rank_row = rank_ref[pl.ds(e, 1), :]
    n_rows = jnp.max(cnt_ref[pl.ds(e, 1), :]).astype(I32)
    n_chunks = lax.shift_right_logical(n_rows + (ch - 1), int(np.log2(ch)))
    w1g = w1g_ref[0]
    w1u = w1u_ref[0]
    w2 = w2_ref[0]
    b1g = b1g_ref[0]
    b1u = b1u_ref[0]
    b2 = b2_ref[0]

    def chunk(c, carry):
        r0 = (c * ch).astype(F32)
        gather = (rank_row == (r0 + _iota((ch, tt), 0).astype(F32))).astype(BF16)
        xg = _dot(gather, xb_ref[...]).astype(BF16)
        hg = jnp.minimum(_dot(xg, w1g) + b1g, SWIGLU_LIMIT)
        hu = jnp.clip(_dot(xg, w1u) + b1u, -SWIGLU_LIMIT, SWIGLU_LIMIT)
        act = (hu + 1.0) * (hg * jax.nn.sigmoid(SWIGLU_ALPHA * hg))
        y = _dot(act.astype(BF16), w2) + b2
        scatter = jnp.where(rank_col == (r0 + _iota((tt, ch), 1).astype(F32)), gate_col, 0.0)
        acc_ref[...] += _dot(scatter.astype(BF16), y.astype(BF16))
        return carry

    lax.fori_loop(0, n_chunks, chunk, 0)

    @pl.when(e == n_e - 1)
    def _finish():
        out_ref[...] = _layer_norm(alpha * x_ref[...] + acc_ref[...], g_ref[...], b_ref[...])


def _moe_ln(x, wr_t, br, w1g, w1u, b1g, b1u, w2, b2, g, b, alpha, *, tt=1024, ch=128):
    t, d = x.shape
    tt = min(tt, t)
    f = w2.shape[1]
    return pl.pallas_call(
        functools.partial(_moe_kernel, tt=tt, ch=ch, alpha=alpha),
        grid=(t // tt, N_EXPERTS),
        in_specs=[
            pl.BlockSpec((tt, d), lambda i, e: (i, 0)),
            pl.BlockSpec((LANES, d), lambda i, e: (0, 0)),
            pl.BlockSpec((LANES, 1), lambda i, e: (0, 0)),
            pl.BlockSpec((1, d, f), lambda i, e: (e, 0, 0)),
            pl.BlockSpec((1, d, f), lambda i, e: (e, 0, 0)),
            pl.BlockSpec((1, 1, f), lambda i, e: (e, 0, 0)),
            pl.BlockSpec((1, 1, f), lambda i, e: (e, 0, 0)),
            pl.BlockSpec((1, f, d), lambda i, e: (e, 0, 0)),
            pl.BlockSpec((1, 1, d), lambda i, e: (e, 0, 0)),
            pl.BlockSpec((1, d), lambda i, e: (0, 0)),
            pl.BlockSpec((1, d), lambda i, e: (0, 0)),
        ],
        out_specs=pl.BlockSpec((tt, d), lambda i, e: (i, 0)),
        out_shape=jax.ShapeDtypeStruct((t, d), F32),
        scratch_shapes=[
            pltpu.VMEM((tt, d), BF16),
            pltpu.VMEM((LANES, tt), F32),
            pltpu.VMEM((tt, LANES), F32),
            pltpu.VMEM((tt, LANES), F32),
            pltpu.VMEM((LANES, LANES), F32),
            pltpu.VMEM((tt, d), F32),
        ],
        compiler_params=_cparams(("arbitrary", "arbitrary")),
        name="moe_ln",
    )(x, wr_t, br, w1g, w1u, b1g, b1u, w2, b2, g, b)


def _ple_kernel(x_ref, p_ref, wg_ref, wp_ref, o_ref):
    x = x_ref[...]
    gate = jax.nn.sigmoid(_dot(x.astype(BF16), wg_ref[...]))
    emb = _dot(p_ref[...].astype(BF16), wp_ref[...])
    o_ref[...] = x + gate * emb


def _ple(x, p, wg, wp, *, tm=512):
    s, d = x.shape
    dp = p.shape[1]
    return pl.pallas_call(
        _ple_kernel,
        grid=(s // tm,),
        in_specs=[
            pl.BlockSpec((tm, d), lambda i: (i, 0)),
            pl.BlockSpec((tm, dp), lambda i: (i, 0)),
            pl.BlockSpec((d, d), lambda i: (0, 0)),
            pl.BlockSpec((dp, d), lambda i: (0, 0)),
        ],
        out_specs=pl.BlockSpec((tm, d), lambda i: (i, 0)),
        out_shape=jax.ShapeDtypeStruct((s, d), F32),
        compiler_params=_cparams(("arbitrary",)),
        name="ple",
    )(x, p, wg, wp)


def _mixer_ab(x, w_in, cos_t, sin_t, *, sb_heads, moba_heads):
    s = x.shape[0]
    wa = sb_heads * HEAD_DIM
    wb = moba_heads * HEAD_DIM
    qa, ka, va, qb, kb, vb = jnp.split(w_in, np.cumsum([wa, wa, wa, wb, wb]).tolist(), axis=1)
    w = jnp.concatenate([qb, kb, qa, ka, va, vb], axis=1).astype(BF16)
    att = HEAD_DIM ** -0.5
    scale = jnp.concatenate([jnp.full((wb,), att), jnp.ones((wb,)), jnp.full((wa,), att),
                             jnp.ones((2 * wa + wb,))]).astype(F32)[None]
    tn = cos_t.shape[1]
    h0 = _proj(x, w, scale, cos_t, sin_t, ((0, 2 * wb // tn),), tn=tn)
    c = lambda off: off // LANES
    o_a = _sb_attention(h0, q_col=c(2 * wb), k_col=c(2 * wb + wa), v_col=c(2 * wb + 2 * wa),
                        n_pairs=sb_heads // 2)
    n_pairs = moba_heads // 2
    nb = s // MOBA_BLOCK
    kaug, kmean = _moba_prep(h0, k_col=c(wb), n_pairs=n_pairs)
    kmean = jnp.pad(kmean.reshape(n_pairs, nb, LANES), ((0, 0), (0, LANES - nb), (0, 0)))
    kmean = kmean.reshape(n_pairs * LANES, LANES)
    o_b = _moba_attention(h0, kaug, kmean, q_col=0, v_col=c(2 * wb + 3 * wa), n_pairs=n_pairs)
    return o_a, o_b


def _mixer_c(x, w_in, cos_t, sin_t, *, n_heads):
    width = n_heads * HEAD_DIM
    iw = IDX_HEADS * HEAD_DIM
    d = w_in.shape[0]
    q, k, v, qi, ki, wi = jnp.split(w_in, np.cumsum([width, width, width, iw, HEAD_DIM]).tolist(), axis=1)
    tn = cos_t.shape[1]
    ki_pad = jnp.zeros((d, tn - 2 * HEAD_DIM), w_in.dtype)
    wi_pad = jnp.zeros((d, tn - IDX_HEADS), w_in.dtype)
    w = jnp.concatenate([q, k, v, qi, ki, ki, ki_pad, wi, wi_pad], axis=1).astype(BF16)
    att = HEAD_DIM ** -0.5
    scale = jnp.concatenate([jnp.full((width,), att), jnp.ones((2 * width,)), jnp.full((iw,), att),
                             jnp.ones((tn,)), jnp.full((tn,), IDX_HEADS ** -0.5)]).astype(F32)[None]
    rope = ((0, 2 * width // tn), (3 * width // tn, (3 * width + iw) // tn + 1))
    h1 = _proj(x, w, scale, cos_t, sin_t, rope, tn=tn)
    return _dsa_attention(h1, n_heads=n_heads)


def _moe_weights(w_r, b_r, w1, b1, w2, b2):
    wr_t = jnp.pad(w_r.T, ((0, LANES - N_EXPERTS), (0, 0)))
    br = jnp.pad(b_r, (0, LANES - N_EXPERTS))[:, None]
    return (wr_t, br, w1[:, :, 0::2].astype(BF16), w1[:, :, 1::2].astype(BF16),
            b1[:, None, 0::2], b1[:, None, 1::2], w2.astype(BF16), b2[:, None, :])


def kernel(x, p, ab_w_in, ab_w_out, c_w_in, c_w_out, ln_g, ln_b, router_w, router_b,
           moe_w1, moe_b1, moe_w2, moe_b2, ple_w_proj, ple_w_gate):
    b, s, d = x.shape
    assert b == 1
    depth = p.shape[0]
    alpha = float((2 * depth) ** 0.25)
    cos_t, sin_t = _rope_tables(s, 2 * LANES)
    sb_heads = moba_heads = ab_w_out.shape[1] // (2 * HEAD_DIM)
    dsa_heads = c_w_out.shape[1] // HEAD_DIM
    xs = x[0]
    for i in range(depth):
        j = i // 2
        g = ln_g[i][:, None, :]
        bb = ln_b[i][:, None, :]
        if i % 2 == 0:
            o_a, o_b = _mixer_ab(xs, ab_w_in[j], cos_t, sin_t, sb_heads=sb_heads, moba_heads=moba_heads)
            w_out = ab_w_out[j].astype(BF16)
            wa = sb_heads * HEAD_DIM
            xs = _outproj_ln([o_a, o_b], [w_out[:wa], w_out[wa:]], xs, g[0], bb[0], alpha)
        else:
            o_c = _mixer_c(xs, c_w_in[j], cos_t, sin_t, n_heads=dsa_heads)
            xs = _outproj_ln([o_c], [c_w_out[j].astype(BF16)], xs, g[0], bb[0], alpha)
        mw = _moe_weights(router_w[i], router_b[i], moe_w1[i], moe_b1[i], moe_w2[i], moe_b2[i])
        xs = _moe_ln(xs, *mw, g[1], bb[1], alpha)
        xs = _ple(xs, p[i, 0], ple_w_gate[i].astype(BF16), ple_w_proj[i].astype(BF16))
    return xs[None]
```

```python
import functools

import jax
import jax.numpy as jnp
import numpy as np
from jax import lax
from jax.experimental import pallas as pl
from jax.experimental.pallas import tpu as pltpu

F32 = jnp.float32
BF16 = jnp.bfloat16
I32 = jnp.int32

HEAD_DIM = 64
LANES = 128
MOBA_BLOCK = 256
MOBA_TOPK = 3
DSA_TOPK = 256
IDX_HEADS = 8
N_EXPERTS = 32
TOP_K = 4
SWIGLU_LIMIT = 7.0
SWIGLU_ALPHA = 1.702
ROPE_THETA = 10000.0
LN_EPS = 1e-5
NEG = -1e30
EXP_FLUSH = -88.0
INT_MIN = -(2 ** 31)
VMEM_LIMIT = 56 * 1024 * 1024


def _cparams(sem):
    return pltpu.CompilerParams(dimension_semantics=sem, vmem_limit_bytes=VMEM_LIMIT)


def _dot(a, b):
    return jnp.dot(a, b, preferred_element_type=F32)


def _dot_nt(a, b):
    return lax.dot_general(a, b, (((1,), (1,)), ((), ())), preferred_element_type=F32)


def _dot_tn(a, b):
    return lax.dot_general(a, b, (((0,), (0,)), ((), ())), preferred_element_type=F32)


def _iota(shape, dim):
    return lax.broadcasted_iota(I32, shape, dim)


def _proj_kernel(x_ref, w_ref, scale_ref, cos_ref, sin_ref, o_ref, xb_ref, *, rope_ranges, tn):
    j = pl.program_id(1)

    @pl.when(j == 0)
    def _():
        xb_ref[...] = x_ref[...].astype(BF16)

    h = _dot(xb_ref[...], w_ref[...]) * scale_ref[...]
    is_rope = functools.reduce(jnp.logical_or, [(j >= a) & (j < b) for a, b in rope_ranges])

    @pl.when(is_rope)
    def _():
        lane = _iota(h.shape, 1)
        first_half = (lane % HEAD_DIM) < (HEAD_DIM // 2)
        partner = jnp.where(first_half, pltpu.roll(h, tn - HEAD_DIM // 2, 1),
                            pltpu.roll(h, HEAD_DIM // 2, 1))
        o_ref[...] = (h * cos_ref[...] + partner * sin_ref[...]).astype(o_ref.dtype)

    @pl.when(jnp.logical_not(is_rope))
    def _():
        o_ref[...] = h.astype(o_ref.dtype)


def _proj(x, w, scale, cos_t, sin_t, rope_ranges, *, tm=512, tn=256):
    s, d = x.shape
    n = w.shape[1]
    return pl.pallas_call(
        functools.partial(_proj_kernel, rope_ranges=rope_ranges, tn=tn),
        grid=(s // tm, n // tn),
        in_specs=[
            pl.BlockSpec((tm, d), lambda i, j: (i, 0)),
            pl.BlockSpec((d, tn), lambda i, j: (0, j)),
            pl.BlockSpec((1, tn), lambda i, j: (0, j)),
            pl.BlockSpec((tm, tn), lambda i, j: (i, 0)),
            pl.BlockSpec((tm, tn), lambda i, j: (i, 0)),
        ],
        out_specs=pl.BlockSpec((tm, tn), lambda i, j: (i, j)),
        out_shape=jax.ShapeDtypeStruct((s, n), BF16),
        scratch_shapes=[pltpu.VMEM((tm, d), BF16)],
        compiler_params=_cparams(("arbitrary", "arbitrary")),
        name="in_proj",
    )(x, w, scale, cos_t, sin_t)


def _rope_tables(s, width):
    half = HEAD_DIM // 2
    inv = ROPE_THETA ** (-jnp.arange(half, dtype=F32) / half)
    ang = jnp.arange(s, dtype=F32)[:, None] * inv[None, :]
    cos = jnp.cos(ang)
    sin = jnp.sin(ang)
    reps = width // HEAD_DIM
    cos_t = jnp.tile(jnp.concatenate([cos, cos], axis=1), (1, reps))
    sin_t = jnp.tile(jnp.concatenate([-sin, sin], axis=1), (1, reps))
    return cos_t, sin_t


def _sb_kernel(q_ref, k_ref, v_ref, o_ref, acc_ref, run_ref, *, t):
    i = pl.program_id(1)
    q = q_ref[...]
    lane = _iota((t, LANES), 1)
    lo = lane < HEAD_DIM
    zero = jnp.zeros_like(q)
    qh = (jnp.where(lo, q, zero), jnp.where(lo, zero, q))
    row = _iota((t, t), 0)
    col = _iota((t, t), 1)
    incl = (row >= col).astype(BF16)
    ones = jnp.ones((t, LANES), BF16)
    strict = col < row

    acc_ref[...] = jnp.zeros_like(acc_ref)
    run_ref[...] = jnp.zeros_like(run_ref)

    def tile(kt, diagonal):
        start = pl.multiple_of(kt * t, t)
        kblk = k_ref[pl.ds(start, t), :]
        vblk = v_ref[pl.ds(start, t), :]
        for h in range(2):
            z = _dot_nt(qh[h], kblk)
            log_stay = -(jnp.maximum(z, 0.0) + jnp.log1p(jnp.exp(-jnp.abs(z))))
            if diagonal:
                log_stay = jnp.where(strict, log_stay, 0.0)
            ls16 = log_stay.astype(BF16)
            cum = _dot(ls16, incl)
            a = jnp.exp(z + cum + run_ref[h])
            if diagonal:
                a = jnp.where(strict, a, 0.0)
            acc_ref[h] += _dot(a.astype(BF16), vblk)
            run_ref[h] += _dot(ls16, ones)

    tile(i, True)

    def cond(c):
        kt, live = c
        return jnp.logical_and(kt >= 0, live)

    def body(c):
        kt, _ = c
        tile(kt, False)
        live = jnp.max(jnp.maximum(run_ref[0], run_ref[1])) > EXP_FLUSH
        return kt - 1, live

    lax.while_loop(cond, body, (i - 1, jnp.bool_(True)))
    o_ref[...] = jnp.where(lo, acc_ref[0], acc_ref[1]).astype(o_ref.dtype)


def _sb_attention(h0, *, q_col, k_col, v_col, n_pairs, t=128):
    s = h0.shape[0]
    return pl.pallas_call(
        functools.partial(_sb_kernel, t=t),
        grid=(n_pairs, s // t),
        in_specs=[
            pl.BlockSpec((t, LANES), lambda p, i: (i, q_col + p)),
            pl.BlockSpec((s, LANES), lambda p, i: (0, k_col + p)),
            pl.BlockSpec((s, LANES), lambda p, i: (0, v_col + p)),
        ],
        out_specs=pl.BlockSpec((t, LANES), lambda p, i: (i, p)),
        out_shape=jax.ShapeDtypeStruct((s, n_pairs * LANES), BF16),
        scratch_shapes=[pltpu.VMEM((2, t, LANES), F32), pltpu.VMEM((2, t, LANES), F32)],
        compiler_params=_cparams(("arbitrary", "arbitrary")),
        name="stick_breaking",
    )(h0, h0, h0)


def _moba_prep_kernel(k_ref, kaug_ref, kmean_ref):
    b = pl.program_id(1)
    k = k_ref[...]
    kmean_ref[...] = jnp.mean(k.astype(F32), axis=0, keepdims=True)[None]
    onehot = (_iota(k.shape, 1) == b).astype(BF16)
    kaug_ref[...] = jnp.concatenate([k, onehot], axis=1)


def _moba_prep(h0, *, k_col, n_pairs):
    s = h0.shape[0]
    nb = s // MOBA_BLOCK
    return pl.pallas_call(
        _moba_prep_kernel,
        grid=(n_pairs, nb),
        in_specs=[pl.BlockSpec((MOBA_BLOCK, LANES), lambda p, b: (b, k_col + p))],
        out_specs=[
            pl.BlockSpec((MOBA_BLOCK, 2 * LANES), lambda p, b: (b, p)),
            pl.BlockSpec((1, 1, LANES), lambda p, b: (p * nb + b, 0, 0)),
        ],
        out_shape=[
            jax.ShapeDtypeStruct((s, n_pairs * 2 * LANES), BF16),
            jax.ShapeDtypeStruct((n_pairs * nb, 1, LANES), F32),
        ],
        compiler_params=_cparams(("arbitrary", "arbitrary")),
        name="moba_prep",
    )(h0)


def _first_argmax(x, lane):
    m = jnp.max(x, axis=1, keepdims=True)
    idx = jnp.min(jnp.where(x == m, lane, float(LANES)), axis=1, keepdims=True)
    return m, idx


def _moba_kernel(q_ref, kaug_ref, v_ref, kmean_ref, o_ref, m_ref, l_ref, acc_ref, *, nb):
    t = MOBA_BLOCK
    i = pl.program_id(1)
    q = q_ref[...]
    lane = _iota((t, LANES), 1)
    lo = lane < HEAD_DIM
    zero = jnp.zeros_like(q)
    qh = (jnp.where(lo, q, zero), jnp.where(lo, zero, q))
    kmean = kmean_ref[...]
    km_hi = kmean.astype(BF16)
    km_lo = (kmean - km_hi.astype(F32)).astype(BF16)
    causal = _iota((t, t), 1) <= _iota((t, t), 0)

    q_aug = []
    for h in range(2):
        gate = _dot_nt(qh[h], km_hi) + _dot_nt(qh[h], km_lo)
        gate = jnp.where(lane < i, gate, -jnp.inf)
        chosen = lane == i
        lane_f = lane.astype(F32)
        for _ in range(MOBA_TOPK):
            m, idx = _first_argmax(gate, lane_f)
            pick = (lane_f == idx) & (m > -jnp.inf)
            chosen = chosen | pick
            gate = jnp.where(lane_f == idx, -jnp.inf, gate)
        bias = jnp.where(chosen | (lane >= nb), 0.0, NEG).astype(BF16)
        q_aug.append(jnp.concatenate([qh[h], bias], axis=1))

    def scores(kt):
        start = pl.multiple_of(kt * t, t)
        kblk = kaug_ref[pl.ds(start, t), :]
        vblk = v_ref[pl.ds(start, t), :]
        return [(_dot_nt(q_aug[h], kblk), vblk) for h in range(2)]

    for h, (s, vblk) in enumerate(scores(i)):
        s = jnp.where(causal, s, NEG)
        m = jnp.max(s, axis=1, keepdims=True)
        p = jnp.exp(s - m)
        m_ref[h] = m
        l_ref[h] = jnp.sum(p, axis=1, keepdims=True)
        acc_ref[h] = _dot(p.astype(BF16), vblk)

    def body(kt, carry):
        for h, (s, vblk) in enumerate(scores(kt)):
            m_old = m_ref[h]
            m_new = jnp.maximum(m_old, jnp.max(s, axis=1, keepdims=True))
            alpha = jnp.exp(m_old - m_new)
            p = jnp.exp(s - m_new)
            m_ref[h] = m_new
            l_ref[h] = alpha * l_ref[h] + jnp.sum(p, axis=1, keepdims=True)
            acc_ref[h] = alpha * acc_ref[h] + _dot(p.astype(BF16), vblk)
        return carry

    lax.fori_loop(0, i, body, 0)
    out0 = acc_ref[0] / l_ref[0]
    out1 = acc_ref[1] / l_ref[1]
    o_ref[...] = jnp.where(lo, out0, out1).astype(o_ref.dtype)


def _moba_attention(h0, kaug, kmean, *, q_col, v_col, n_pairs):
    s = h0.shape[0]
    t = MOBA_BLOCK
    nb = s // t
    assert nb <= LANES
    return pl.pallas_call(
        functools.partial(_moba_kernel, nb=nb),
        grid=(n_pairs, nb),
        in_specs=[
            pl.BlockSpec((t, LANES), lambda p, i: (i, q_col + p)),
            pl.BlockSpec((s, 2 * LANES), lambda p, i: (0, p)),
            pl.BlockSpec((s, LANES), lambda p, i: (0, v_col + p)),
            pl.BlockSpec((LANES, LANES), lambda p, i: (p, 0)),
        ],
        out_specs=pl.BlockSpec((t, LANES), lambda p, i: (i, p)),
        out_shape=jax.ShapeDtypeStruct((s, n_pairs * LANES), BF16),
        scratch_shapes=[pltpu.VMEM((2, t, 1), F32), pltpu.VMEM((2, t, 1), F32),
                        pltpu.VMEM((2, t, LANES), F32)],
        compiler_params=_cparams(("arbitrary", "arbitrary")),
        name="moba",
    )(h0, kaug, h0, kmean)


def _dsa_kernel(qi_ref, wi_ref, ki_ref, q_ref, k_ref, v_ref, o_ref,
                key_ref, thr_ref, need_ref, seen_ref, m_ref, l_ref, acc_ref,
                *, t, n_heads, topk):
    i = pl.program_id(0)
    j = pl.program_id(1)
    lane = _iota((t, LANES), 1)
    lo = lane < HEAD_DIM
    causal = _iota((t, t), 1) <= _iota((t, t), 0)

    @pl.when(j == 0)
    def _select():
        qi = qi_ref[...]
        wi = wi_ref[...].astype(F32)
        zero = jnp.zeros((t, LANES), BF16)
        qih, wih = [], []
        for h in range(IDX_HEADS):
            pair = qi[:, (h // 2) * LANES:(h // 2 + 1) * LANES]
            qih.append(jnp.where(lo, pair, zero) if h % 2 == 0 else jnp.where(lo, zero, pair))
            wih.append(jnp.sum(jnp.where(lane == h, wi, 0.0), axis=1, keepdims=True))

        def score_tile(kt, carry):
            start = pl.multiple_of(kt * t, t)
            kblk = ki_ref[pl.ds(start, t), :]
            sc = jnp.zeros((t, t), F32)
            for h in range(IDX_HEADS):
                sc = sc + wih[h] * jnp.maximum(_dot_nt(qih[h], kblk), 0.0)
            bits = pltpu.bitcast(sc, I32)
            key = jnp.where(bits >= 0, bits, bits ^ jnp.int32(0x7FFFFFFF))
            key = jnp.where(jnp.logical_or(kt < i, causal), key, jnp.int32(INT_MIN))
            key_ref[kt] = key
            return carry

        lax.fori_loop(0, i + 1, score_tile, 0)

        def count(pred):
            def cbody(kt, c):
                hit = jnp.where(pred(key_ref[kt]), 1.0, 0.0)
                part = hit[:, :LANES]
                for g in range(1, t // LANES):
                    part = part + hit[:, g * LANES:(g + 1) * LANES]
                return c + part
            c = lax.fori_loop(0, i + 1, cbody, jnp.zeros((t, LANES), F32))
            return jnp.sum(c, axis=1, keepdims=True)

        thr_u = jnp.zeros((t, 1), I32)
        for b in range(31, -1, -1):
            bit = jnp.int32(INT_MIN) if b == 31 else jnp.int32(1 << b)
            cand_u = thr_u | bit
            cand_s = cand_u ^ jnp.int32(INT_MIN)
            cnt = count(lambda kk, c=cand_s: kk >= c)
            thr_u = jnp.where(cnt >= float(topk), cand_u, thr_u)
        thr = thr_u ^ jnp.int32(INT_MIN)
        n_gt = count(lambda kk: kk > thr)
        thr_ref[...] = thr
        need_ref[...] = float(topk) - n_gt
        seen_ref[...] = jnp.zeros_like(seen_ref)
        m_ref[...] = jnp.full(m_ref.shape, NEG, F32)
        l_ref[...] = jnp.zeros_like(l_ref)
        acc_ref[...] = jnp.zeros_like(acc_ref)

    @pl.when(j <= i)
    def _attend():
        key = key_ref[j]
        thr = thr_ref[...]
        eq = key == thr
        eq16 = jnp.where(eq, 1.0, 0.0).astype(BF16)
        before = (_iota((t, t), 0) < _iota((t, t), 1)).astype(BF16)
        rank = seen_ref[...] + _dot(eq16, before)
        sel = (key > thr) | (eq & (rank < need_ref[...]))
        sel = sel & jnp.logical_or(j < i, causal)
        seen_ref[...] += jnp.sum(jnp.where(eq, 1.0, 0.0), axis=1, keepdims=True)
        bias = jnp.where(sel, 0.0, NEG)

        q = q_ref[...]
        kb = k_ref[...]
        vb = v_ref[...]
        zero = jnp.zeros((t, LANES), BF16)
        for p in range(n_heads // 2):
            sl = slice(p * LANES, (p + 1) * LANES)
            qp, kp, vp = q[:, sl], kb[:, sl], vb[:, sl]
            outs, alphas = [], []
            for hh in range(2):
                h = 2 * p + hh
                qm = jnp.where(lo, qp, zero) if hh == 0 else jnp.where(lo, zero, qp)
                s = _dot_nt(qm, kp) + bias
                m_old = m_ref[h]
                m_new = jnp.maximum(m_old, jnp.max(s, axis=1, keepdims=True))
                alpha = jnp.exp(m_old - m_new)
                pr = jnp.exp(s - m_new)
                m_ref[h] = m_new
                l_ref[h] = alpha * l_ref[h] + jnp.sum(pr, axis=1, keepdims=True)
                outs.append(_dot(pr.astype(BF16), vp))
                alphas.append(alpha)
            upd = jnp.where(lo, outs[0], outs[1])
            al = jnp.where(lo, alphas[0], alphas[1])
            acc_ref[:, sl] = al * acc_ref[:, sl] + upd

    @pl.when(j == i)
    def _finish():
        for p in range(n_heads // 2):
            sl = slice(p * LANES, (p + 1) * LANES)
            linv = jnp.where(lo, 1.0 / l_ref[2 * p], 1.0 / l_ref[2 * p + 1])
            o_ref[:, sl] = (acc_ref[:, sl] * linv).astype(o_ref.dtype)


def _dsa_attention(h1, *, n_heads, t=256):
    s = h1.shape[0]
    width = n_heads * HEAD_DIM
    nt = s // t
    topk = min(DSA_TOPK, s // 4)
    qi_blk = 3 * width // (IDX_HEADS * HEAD_DIM)
    ki_blk = (3 * width + IDX_HEADS * HEAD_DIM) // LANES
    wi_blk = ki_blk + 2
    return pl.pallas_call(
        functools.partial(_dsa_kernel, t=t, n_heads=n_heads, topk=topk),
        grid=(nt, nt),
        in_specs=[
            pl.BlockSpec((t, IDX_HEADS * HEAD_DIM), lambda i, j: (i, qi_blk)),
            pl.BlockSpec((t, LANES), lambda i, j: (i, wi_blk)),
            pl.BlockSpec((s, LANES), lambda i, j: (0, ki_blk)),
            pl.BlockSpec((t, width), lambda i, j: (i, 0)),
            pl.BlockSpec((t, width), lambda i, j: (jnp.minimum(j, i), 1)),
            pl.BlockSpec((t, width), lambda i, j: (jnp.minimum(j, i), 2)),
        ],
        out_specs=pl.BlockSpec((t, width), lambda i, j: (i, 0)),
        out_shape=jax.ShapeDtypeStruct((s, width), BF16),
        scratch_shapes=[
            pltpu.VMEM((nt, t, t), I32),
            pltpu.VMEM((t, 1), I32),
            pltpu.VMEM((t, 1), F32),
            pltpu.VMEM((t, 1), F32),
            pltpu.VMEM((n_heads, t, 1), F32),
            pltpu.VMEM((n_heads, t, 1), F32),
            pltpu.VMEM((t, width), F32),
        ],
        compiler_params=_cparams(("arbitrary", "arbitrary")),
        name="dsa",
    )(h1, h1, h1, h1, h1, h1)


def _layer_norm(y, g, b):
    mu = jnp.mean(y, axis=-1, keepdims=True)
    yc = y - mu
    var = jnp.mean(yc * yc, axis=-1, keepdims=True)
    return yc * lax.rsqrt(var + LN_EPS) * g + b


def _outproj_kernel(*refs, n_in, alpha):
    o_refs = refs[:n_in]
    w_refs = refs[n_in:2 * n_in]
    x_ref, g_ref, b_ref, out_ref = refs[2 * n_in:]
    y = alpha * x_ref[...]
    for o_ref, w_ref in zip(o_refs, w_refs):
        y = y + _dot(o_ref[...], w_ref[...])
    out_ref[...] = _layer_norm(y, g_ref[...], b_ref[...])


def _outproj_ln(os_, ws, x, g, b, alpha, *, tm=256):
    s, d = x.shape
    n_in = len(os_)
    in_specs = ([pl.BlockSpec((tm, o.shape[1]), lambda i: (i, 0)) for o in os_]
                + [pl.BlockSpec(w.shape, lambda i: (0, 0)) for w in ws]
                + [pl.BlockSpec((tm, d), lambda i: (i, 0)),
                   pl.BlockSpec((1, d), lambda i: (0, 0)),
                   pl.BlockSpec((1, d), lambda i: (0, 0))])
    return pl.pallas_call(
        functools.partial(_outproj_kernel, n_in=n_in, alpha=alpha),
        grid=(s // tm,),
        in_specs=in_specs,
        out_specs=pl.BlockSpec((tm, d), lambda i: (i, 0)),
        out_shape=jax.ShapeDtypeStruct((s, d), F32),
        compiler_params=_cparams(("arbitrary",)),
        name="out_proj_ln",
    )(*os_, *ws, x, g, b)


def _moe_kernel(x_ref, wr_ref, br_ref, w1g_ref, w1u_ref, b1g_ref, b1u_ref, w2_ref, b2_ref,
                g_ref, b_ref, out_ref,
                xb_ref, rank_ref, rank_t_ref, gate_t_ref, cnt_ref, acc_ref,
                *, tt, ch, alpha):
    e = pl.program_id(1)
    n_e = pl.num_programs(1)

    @pl.when(e == 0)
    def _route():
        x = x_ref[...]
        xb_ref[...] = x.astype(BF16)
        logits = lax.dot_general(wr_ref[...], x, (((1,), (1,)), ((), ())),
                                 precision=lax.Precision.HIGHEST,
                                 preferred_element_type=F32) + br_ref[...]
        row = _iota(logits.shape, 0).astype(F32)
        work = jnp.where(row < N_EXPERTS, logits, -jnp.inf)
        chosen = row < 0
        top = None
        for r in range(TOP_K):
            m = jnp.max(work, axis=0, keepdims=True)
            idx = jnp.min(jnp.where(work == m, row, float(LANES)), axis=0, keepdims=True)
            pick = row == idx
            chosen = chosen | pick
            work = jnp.where(pick, -jnp.inf, work)
            if r == 0:
                top = m
        ex = jnp.where(chosen, jnp.exp(logits - top), 0.0)
        gate = ex / jnp.sum(ex, axis=0, keepdims=True)
        sel16 = jnp.where(chosen, 1.0, 0.0).astype(BF16)
        before = (_iota((tt, tt), 0) < _iota((tt, tt), 1)).astype(BF16)
        pos = _dot(sel16, before)
        rank = jnp.where(chosen, pos, -1.0)
        rank_ref[...] = rank
        rank_t_ref[...] = rank.T
        gate_t_ref[...] = gate.T
        cnt = jnp.sum(jnp.where(chosen, 1.0, 0.0), axis=1, keepdims=True)
        cnt_ref[...] = jnp.broadcast_to(cnt, cnt_ref.shape)
        acc_ref[...] = jnp.zeros_like(acc_ref)

    lane = _iota((tt, LANES), 1)
    on_e = lane == e
    rank_col = jnp.sum(jnp.where(on_e, rank_t_ref[...], 0.0), axis=1, keepdims=True)
    gate_col = jnp.sum(jnp.where(on_e, gate_t_ref[...], 0.0), axis=1, keepdims=True)
    rank_row = rank_ref[pl.ds(e, 1), :]
    n_rows = jnp.max(cnt_ref[pl.ds(e, 1), :]).astype(I32)
    n_chunks = lax.shift_right_logical(n_rows + (ch - 1), int(np.log2(ch)))
    w1g = w1g_ref[0]
    w1u = w1u_ref[0]
    w2 = w2_ref[0]
    b1g = b1g_ref[0]
    b1u = b1u_ref[0]
    b2 = b2_ref[0]

    def chunk(c, carry):
        r0 = (c * ch).astype(F32)
        gather = (rank_row == (r0 + _iota((ch, tt), 0).astype(F32))).astype(BF16)
        xg = _dot(gather, xb_ref[...]).astype(BF16)
        hg = jnp.minimum(_dot(xg, w1g) + b1g, SWIGLU_LIMIT)
        hu = jnp.clip(_dot(xg, w1u) + b1u, -SWIGLU_LIMIT, SWIGLU_LIMIT)
        act = (hu + 1.0) * (hg * jax.nn.sigmoid(SWIGLU_ALPHA * hg))
        y = _dot(act.astype(BF16), w2) + b2
        scatter = jnp.where(rank_col == (r0 + _iota((tt, ch), 1).astype(F32)), gate_col, 0.0)
        acc_ref[...] += _dot(scatter.astype(BF16), y.astype(BF16))
        return carry

    lax.fori_loop(0, n_chunks, chunk, 0)

    @pl.when(e == n_e - 1)
    def _finish():
        out_ref[...] = _layer_norm(alpha * x_ref[...] + acc_ref[...], g_ref[...], b_ref[...])


def _moe_ln(x, wr_t, br, w1g, w1u, b1g, b1u, w2, b2, g, b, alpha, *, tt=1024, ch=128):
    t, d = x.shape
    tt = min(tt, t)
    f = w2.shape[1]
    return pl.pallas_call(
        functools.partial(_moe_kernel, tt=tt, ch=ch, alpha=alpha),
        grid=(t // tt, N_EXPERTS),
        in_specs=[
            pl.BlockSpec((tt, d), lambda i, e: (i, 0)),
            pl.BlockSpec((LANES, d), lambda i, e: (0, 0)),
            pl.BlockSpec((LANES, 1), lambda i, e: (0, 0)),
            pl.BlockSpec((1, d, f), lambda i, e: (e, 0, 0)),
            pl.BlockSpec((1, d, f), lambda i, e: (e, 0, 0)),
            pl.BlockSpec((1, 1, f), lambda i, e: (e, 0, 0)),
            pl.BlockSpec((1, 1, f), lambda i, e: (e, 0, 0)),
            pl.BlockSpec((1, f, d), lambda i, e: (e, 0, 0)),
            pl.BlockSpec((1, 1, d), lambda i, e: (e, 0, 0)),
            pl.BlockSpec((1, d), lambda i, e: (0, 0)),
            pl.BlockSpec((1, d), lambda i, e: (0, 0)),
        ],
        out_specs=pl.BlockSpec((tt, d), lambda i, e: (i, 0)),
        out_shape=jax.ShapeDtypeStruct((t, d), F32),
        scratch_shapes=[
            pltpu.VMEM((tt, d), BF16),
            pltpu.VMEM((LANES, tt), F32),
            pltpu.VMEM((tt, LANES), F32),
            pltpu.VMEM((tt, LANES), F32),
            pltpu.VMEM((LANES, LANES), F32),
            pltpu.VMEM((tt, d), F32),
        ],
        compiler_params=_cparams(("arbitrary", "arbitrary")),
        name="moe_ln",
    )(x, wr_t, br, w1g, w1u, b1g, b1u, w2, b2, g, b)


def _ple_kernel(x_ref, p_ref, wg_ref, wp_ref, o_ref):
    x = x_ref[...]
    gate = jax.nn.sigmoid(_dot(x.astype(BF16), wg_ref[...]))
    emb = _dot(p_ref[...].astype(BF16), wp_ref[...])
    o_ref[...] = x + gate * emb


def _ple(x, p, wg, wp, *, tm=512):
    s, d = x.shape
    dp = p.shape[1]
    return pl.pallas_call(
        _ple_kernel,
        grid=(s // tm,),
        in_specs=[
            pl.BlockSpec((tm, d), lambda i: (i, 0)),
            pl.BlockSpec((tm, dp), lambda i: (i, 0)),
            pl.BlockSpec((d, d), lambda i: (0, 0)),
            pl.BlockSpec((dp, d), lambda i: (0, 0)),
        ],
        out_specs=pl.BlockSpec((tm, d), lambda i: (i, 0)),
        out_shape=jax.ShapeDtypeStruct((s, d), F32),
        compiler_params=_cparams(("arbitrary",)),
        name="ple",
    )(x, p, wg, wp)


def _mixer_ab(x, w_in, cos_t, sin_t, *, sb_heads, moba_heads):
    s = x.shape[0]
    wa = sb_heads * HEAD_DIM
    wb = moba_heads * HEAD_DIM
    qa, ka, va, qb, kb, vb = jnp.split(w_in, np.cumsum([wa, wa, wa, wb, wb]).tolist(), axis=1)
    w = jnp.concatenate([qb, kb, qa, ka, va, vb], axis=1).astype(BF16)
    att = HEAD_DIM ** -0.5
    scale = jnp.concatenate([jnp.full((wb,), att), jnp.ones((wb,)), jnp.full((wa,), att),
                             jnp.ones((2 * wa + wb,))]).astype(F32)[None]
    tn = cos_t.shape[1]
    h0 = _proj(x, w, scale, cos_t, sin_t, ((0, 2 * wb // tn),), tn=tn)
    c = lambda off: off // LANES
    o_a = _sb_attention(h0, q_col=c(2 * wb), k_col=c(2 * wb + wa), v_col=c(2 * wb + 2 * wa),
                        n_pairs=sb_heads // 2)
    n_pairs = moba_heads // 2
    nb = s // MOBA_BLOCK
    kaug, kmean = _moba_prep(h0, k_col=c(wb), n_pairs=n_pairs)
    kmean = jnp.pad(kmean.reshape(n_pairs, nb, LANES), ((0, 0), (0, LANES - nb), (0, 0)))
    kmean = kmean.reshape(n_pairs * LANES, LANES)
    o_b = _moba_attention(h0, kaug, kmean, q_col=0, v_col=c(2 * wb + 3 * wa), n_pairs=n_pairs)
    return o_a, o_b


def _mixer_c(x, w_in, cos_t, sin_t, *, n_heads):
    width = n_heads * HEAD_DIM
    iw = IDX_HEADS * HEAD_DIM
    d = w_in.shape[0]
    q, k, v, qi, ki, wi = jnp.split(w_in, np.cumsum([width, width, width, iw, HEAD_DIM]).tolist(), axis=1)
    tn = cos_t.shape[1]
    ki_pad = jnp.zeros((d, tn - 2 * HEAD_DIM), w_in.dtype)
    wi_pad = jnp.zeros((d, tn - IDX_HEADS), w_in.dtype)
    w = jnp.concatenate([q, k, v, qi, ki, ki, ki_pad, wi, wi_pad], axis=1).astype(BF16)
    att = HEAD_DIM ** -0.5
    scale = jnp.concatenate([jnp.full((width,), att), jnp.ones((2 * width,)), jnp.full((iw,), att),
                             jnp.ones((tn,)), jnp.full((tn,), IDX_HEADS ** -0.5)]).astype(F32)[None]
    rope = ((0, 2 * width // tn), (3 * width // tn, (3 * width + iw) // tn + 1))
    h1 = _proj(x, w, scale, cos_t, sin_t, rope, tn=tn)
    return _dsa_attention(h1, n_heads=n_heads)


def _split_kernel(w_ref, perm_ref, g_ref, u_ref):
    half = g_ref.shape[2]
    both = _dot(w_ref[0].astype(BF16), perm_ref[...])
    g_ref[0] = both[:, :half].astype(BF16)
    u_ref[0] = both[:, half:].astype(BF16)


def _split_gate_up(w1, *, tc=512):
    n_e, d, f2 = w1.shape
    half = tc // 2
    src = jnp.concatenate([2 * jnp.arange(half), 2 * jnp.arange(half) + 1])
    perm = (jnp.arange(tc)[:, None] == src[None, :]).astype(BF16)
    out = jax.ShapeDtypeStruct((n_e, d, f2 // 2), BF16)
    return pl.pallas_call(
        _split_kernel,
        grid=(n_e, f2 // tc),
        in_specs=[pl.BlockSpec((1, d, tc), lambda e, c: (e, 0, c)),
                  pl.BlockSpec((tc, tc), lambda e, c: (0, 0))],
        out_specs=[pl.BlockSpec((1, d, half), lambda e, c: (e, 0, c)),
                   pl.BlockSpec((1, d, half), lambda e, c: (e, 0, c))],
        out_shape=[out, out],
        compiler_params=_cparams(("arbitrary", "arbitrary")),
        name="split_gate_up",
    )(w1, perm)


def _moe_weights(w_r, b_r, w1, b1, w2, b2):
    wr_t = jnp.pad(w_r.T, ((0, LANES - N_EXPERTS), (0, 0)))
    br = jnp.pad(b_r, (0, LANES - N_EXPERTS))[:, None]
    w1g, w1u = _split_gate_up(w1)
    return (wr_t, br, w1g, w1u, b1[:, None, 0::2], b1[:, None, 1::2], w2.astype(BF16), b2[:, None, :])


def kernel(x, p, ab_w_in, ab_w_out, c_w_in, c_w_out, ln_g, ln_b, router_w, router_b,
           moe_w1, moe_b1, moe_w2, moe_b2, ple_w_proj, ple_w_gate):
    b, s, d = x.shape
    assert b == 1
    depth = p.shape[0]
    alpha = float((2 * depth) ** 0.25)
    cos_t, sin_t = _rope_tables(s, 2 * LANES)
    sb_heads = moba_heads = ab_w_out.shape[1] // (2 * HEAD_DIM)
    dsa_heads = c_w_out.shape[1] // HEAD_DIM
    xs = x[0]
    for i in range(depth):
        j = i // 2
        g = ln_g[i][:, None, :]
        bb = ln_b[i][:, None, :]
        if i % 2 == 0:
            o_a, o_b = _mixer_ab(xs, ab_w_in[j], cos_t, sin_t, sb_heads=sb_heads, moba_heads=moba_heads)
            w_out = ab_w_out[j].astype(BF16)
            wa = sb_heads * HEAD_DIM
            xs = _outproj_ln([o_a, o_b], [w_out[:wa], w_out[wa:]], xs, g[0], bb[0], alpha)
        else:
            o_c = _mixer_c(xs, c_w_in[j], cos_t, sin_t, n_heads=dsa_heads)
            xs = _outproj_ln([o_c], [c_w_out[j].astype(BF16)], xs, g[0], bb[0], alpha)
        mw = _moe_weights(router_w[i], router_b[i], moe_w1[i], moe_b1[i], moe_w2[i], moe_b2[i])
        xs = _moe_ln(xs, *mw, g[1], bb[1], alpha)
        xs = _ple(xs, p[i, 0], ple_w_gate[i].astype(BF16), ple_w_proj[i].astype(BF16))
    return xs[None]
```

```python
import functools

import jax
import jax.numpy as jnp
import numpy as np
from jax import lax
from jax.experimental import pallas as pl
from jax.experimental.pallas import tpu as pltpu

F32 = jnp.float32
BF16 = jnp.bfloat16
I32 = jnp.int32

HEAD_DIM = 64
LANES = 128
MOBA_BLOCK = 256
MOBA_TOPK = 3
DSA_TOPK = 256
IDX_HEADS = 8
N_EXPERTS = 32
TOP_K = 4
SWIGLU_LIMIT = 7.0
SWIGLU_ALPHA = 1.702
ROPE_THETA = 10000.0
LN_EPS = 1e-5
NEG = -1e30
EXP_FLUSH = -88.0
INT_MIN = -(2 ** 31)
LOG2E = 1.4426950408889634
ONES_ROWS = 16
VMEM_LIMIT = 56 * 1024 * 1024


def _cparams(sem):
    return pltpu.CompilerParams(dimension_semantics=sem, vmem_limit_bytes=VMEM_LIMIT)


def _dot(a, b):
    return jnp.dot(a, b, preferred_element_type=F32)


def _dot_nt(a, b):
    return lax.dot_general(a, b, (((1,), (1,)), ((), ())), preferred_element_type=F32)


def _dot_tn(a, b):
    return lax.dot_general(a, b, (((0,), (0,)), ((), ())), preferred_element_type=F32)


def _iota(shape, dim):
    return lax.broadcasted_iota(I32, shape, dim)


def _proj_kernel(x_ref, w_ref, scale_ref, cos_ref, sin_ref, o_ref, xb_ref, *, rope_ranges, tn):
    j = pl.program_id(1)

    @pl.when(j == 0)
    def _():
        xb_ref[...] = x_ref[...].astype(BF16)

    h = _dot(xb_ref[...], w_ref[...]) * scale_ref[...]
    is_rope = functools.reduce(jnp.logical_or, [(j >= a) & (j < b) for a, b in rope_ranges])

    @pl.when(is_rope)
    def _():
        lane = _iota(h.shape, 1)
        first_half = (lane % HEAD_DIM) < (HEAD_DIM // 2)
        partner = jnp.where(first_half, pltpu.roll(h, tn - HEAD_DIM // 2, 1),
                            pltpu.roll(h, HEAD_DIM // 2, 1))
        o_ref[...] = (h * cos_ref[...] + partner * sin_ref[...]).astype(o_ref.dtype)

    @pl.when(jnp.logical_not(is_rope))
    def _():
        o_ref[...] = h.astype(o_ref.dtype)


def _proj(x, w, scale, cos_t, sin_t, rope_ranges, *, tm=512, tn=256):
    s, d = x.shape
    n = w.shape[1]
    return pl.pallas_call(
        functools.partial(_proj_kernel, rope_ranges=rope_ranges, tn=tn),
        grid=(s // tm, n // tn),
        in_specs=[
            pl.BlockSpec((tm, d), lambda i, j: (i, 0)),
            pl.BlockSpec((d, tn), lambda i, j: (0, j)),
            pl.BlockSpec((1, tn), lambda i, j: (0, j)),
            pl.BlockSpec((tm, tn), lambda i, j: (i, 0)),
            pl.BlockSpec((tm, tn), lambda i, j: (i, 0)),
        ],
        out_specs=pl.BlockSpec((tm, tn), lambda i, j: (i, j)),
        out_shape=jax.ShapeDtypeStruct((s, n), BF16),
        scratch_shapes=[pltpu.VMEM((tm, d), BF16)],
        compiler_params=_cparams(("arbitrary", "arbitrary")),
        name="in_proj",
    )(x, w, scale, cos_t, sin_t)


def _proj_t_kernel(x_ref, w_ref, scale_ref, o_ref, *, groups):
    h = _dot(x_ref[...].astype(BF16), w_ref[...]) * scale_ref[...]
    for g in range(groups):
        o_ref[g, 0] = h[:, g * LANES:(g + 1) * LANES].T.astype(o_ref.dtype)


def _proj_t(x, w, scale, *, groups, tm=256):
    s, d = x.shape
    n = w.shape[1]
    tc = groups * LANES
    return pl.pallas_call(
        functools.partial(_proj_t_kernel, groups=groups),
        grid=(s // tm, n // tc),
        in_specs=[
            pl.BlockSpec((tm, d), lambda i, j: (i, 0)),
            pl.BlockSpec((d, tc), lambda i, j: (0, j)),
            pl.BlockSpec((1, tc), lambda i, j: (0, j)),
        ],
        out_specs=pl.BlockSpec((groups, 1, LANES, tm), lambda i, j: (j, i, 0, 0)),
        out_shape=jax.ShapeDtypeStruct((n // LANES, s // tm, LANES, tm), BF16),
        compiler_params=_cparams(("arbitrary", "arbitrary")),
        name="proj_t",
    )(x, w, scale)


def _rope_tables(s, width):
    half = HEAD_DIM // 2
    inv = ROPE_THETA ** (-jnp.arange(half, dtype=F32) / half)
    ang = jnp.arange(s, dtype=F32)[:, None] * inv[None, :]
    cos = jnp.cos(ang)
    sin = jnp.sin(ang)
    reps = width // HEAD_DIM
    cos_t = jnp.tile(jnp.concatenate([cos, cos], axis=1), (1, reps))
    sin_t = jnp.tile(jnp.concatenate([-sin, sin], axis=1), (1, reps))
    return cos_t, sin_t


def _sb_kernel(q_ref, k_ref, v_ref, o_ref, acc_ref, run_ref, *, t):
    i = pl.program_id(1)
    q = q_ref[...]
    lane = _iota((t, LANES), 1)
    lo = lane < HEAD_DIM
    zero = jnp.zeros_like(q)
    qh = (jnp.where(lo, q, zero), jnp.where(lo, zero, q))
    row = _iota((t, t), 0)
    col = _iota((t, t), 1)
    incl = (row >= col).astype(BF16)
    ones = jnp.ones((t, LANES), BF16)
    strict = col < row

    acc_ref[...] = jnp.zeros_like(acc_ref)
    run_ref[...] = jnp.zeros_like(run_ref)

    def tile(kt, diagonal):
        start = pl.multiple_of(kt * t, t)
        kblk = k_ref[pl.ds(start, t), :]
        vblk = v_ref[pl.ds(start, t), :]
        for h in range(2):
            z = _dot_nt(qh[h], kblk)
            log_stay = -(jnp.maximum(z, 0.0) + jnp.log1p(jnp.exp(-jnp.abs(z))))
            if diagonal:
                log_stay = jnp.where(strict, log_stay, 0.0)
            ls16 = log_stay.astype(BF16)
            cum = _dot(ls16, incl)
            a = jnp.exp(z + cum + run_ref[h])
            if diagonal:
                a = jnp.where(strict, a, 0.0)
            acc_ref[h] += _dot(a.astype(BF16), vblk)
            run_ref[h] += _dot(ls16, ones)

    tile(i, True)

    def cond(c):
        kt, live = c
        return jnp.logical_and(kt >= 0, live)

    def body(c):
        kt, _ = c
        tile(kt, False)
        live = jnp.max(jnp.maximum(run_ref[0], run_ref[1])) > EXP_FLUSH
        return kt - 1, live

    lax.while_loop(cond, body, (i - 1, jnp.bool_(True)))
    o_ref[...] = jnp.where(lo, acc_ref[0], acc_ref[1]).astype(o_ref.dtype)


def _sb_attention(h0, *, q_col, k_col, v_col, n_pairs, t=128):
    s = h0.shape[0]
    return pl.pallas_call(
        functools.partial(_sb_kernel, t=t),
        grid=(n_pairs, s // t),
        in_specs=[
            pl.BlockSpec((t, LANES), lambda p, i: (i, q_col + p)),
            pl.BlockSpec((s, LANES), lambda p, i: (0, k_col + p)),
            pl.BlockSpec((s, LANES), lambda p, i: (0, v_col + p)),
        ],
        out_specs=pl.BlockSpec((t, LANES), lambda p, i: (i, p)),
        out_shape=jax.ShapeDtypeStruct((s, n_pairs * LANES), BF16),
        scratch_shapes=[pltpu.VMEM((2, t, LANES), F32), pltpu.VMEM((2, t, LANES), F32)],
        compiler_params=_cparams(("arbitrary", "arbitrary")),
        name="stick_breaking",
    )(h0, h0, h0)


def _moba_prep_kernel(k_ref, kaug_ref, kmean_ref):
    b = pl.program_id(1)
    k = k_ref[...]
    kmean_ref[...] = jnp.mean(k.astype(F32), axis=0, keepdims=True)[None]
    onehot = (_iota(k.shape, 1) == b).astype(BF16)
    kaug_ref[...] = jnp.concatenate([k, onehot], axis=1)


def _moba_prep(h0, *, k_col, n_pairs):
    s = h0.shape[0]
    nb = s // MOBA_BLOCK
    return pl.pallas_call(
        _moba_prep_kernel,
        grid=(n_pairs, nb),
        in_specs=[pl.BlockSpec((MOBA_BLOCK, LANES), lambda p, b: (b, k_col + p))],
        out_specs=[
            pl.BlockSpec((MOBA_BLOCK, 2 * LANES), lambda p, b: (b, p)),
            pl.BlockSpec((1, 1, LANES), lambda p, b: (p * nb + b, 0, 0)),
        ],
        out_shape=[
            jax.ShapeDtypeStruct((s, n_pairs * 2 * LANES), BF16),
            jax.ShapeDtypeStruct((n_pairs * nb, 1, LANES), F32),
        ],
        compiler_params=_cparams(("arbitrary", "arbitrary")),
        name="moba_prep",
    )(h0)


def _first_argmax(x, lane):
    m = jnp.max(x, axis=1, keepdims=True)
    idx = jnp.min(jnp.where(x == m, lane, float(LANES)), axis=1, keepdims=True)
    return m, idx


def _moba_kernel(q_ref, kaug_ref, vt_ref, kmean_ref, o_ref, m_ref, acc_ref, st_ref, *, nb):
    t = MOBA_BLOCK
    i = pl.program_id(1)
    q = q_ref[...]
    lane = _iota((t, LANES), 1)
    lo = lane < HEAD_DIM
    zero = jnp.zeros_like(q)
    qh = (jnp.where(lo, q, zero), jnp.where(lo, zero, q))
    kmean = kmean_ref[...]
    km_hi = kmean.astype(BF16)
    km_lo = (kmean - km_hi.astype(F32)).astype(BF16)
    key_le_query = _iota((t, t), 0) <= _iota((t, t), 1)
    ones = jnp.ones((ONES_ROWS, t), BF16)

    q_aug = []
    for h in range(2):
        gate = _dot_nt(qh[h], km_hi) + _dot_nt(qh[h], km_lo)
        gate = jnp.where(lane < i, gate, -jnp.inf)
        chosen = lane == i
        lane_f = lane.astype(F32)
        for _ in range(MOBA_TOPK):
            m, idx = _first_argmax(gate, lane_f)
            pick = (lane_f == idx) & (m > -jnp.inf)
            chosen = chosen | pick
            gate = jnp.where(lane_f == idx, -jnp.inf, gate)
        bias = jnp.where(chosen | (lane >= nb), 0.0, NEG)
        q_aug.append(jnp.concatenate([qh[h].astype(F32), bias], axis=1).T.astype(BF16))

    def scores(kt, h):
        start = pl.multiple_of(kt * t, t)
        return _dot(kaug_ref[pl.ds(start, t), :], q_aug[h])

    m_ref[...] = jnp.full(m_ref.shape, NEG, F32)
    acc_ref[...] = jnp.zeros_like(acc_ref)
    for h in range(2):
        st_ref[h] = jnp.where(key_le_query, scores(i, h), NEG)

    def body(n, carry):
        cur = jnp.where(n == 0, i, n - 1)
        nxt = jnp.minimum(n, jnp.maximum(i - 1, 0))
        st_next = [scores(nxt, h) for h in range(2)]
        vt = jnp.concatenate([vt_ref[0, cur], ones], axis=0)
        for h in range(2):
            st = st_ref[h]
            m_old = m_ref[h]
            m_new = jnp.maximum(m_old, jnp.max(st, axis=0, keepdims=True))
            alpha = jnp.exp2(m_old - m_new)
            pt = jnp.exp2(st - m_new).astype(BF16)
            acc_ref[h] = alpha * acc_ref[h] + _dot(vt, pt)
            m_ref[h] = m_new
        for h in range(2):
            st_ref[h] = st_next[h]
        return carry

    lax.fori_loop(0, i + 1, body, 0)
    outs = [acc_ref[h][:LANES] / acc_ref[h][LANES:LANES + 1] for h in range(2)]
    first = _iota((LANES, t), 0) < HEAD_DIM
    o_ref[...] = jnp.where(first, outs[0], outs[1]).T.astype(o_ref.dtype)


def _moba_attention(h0, kaug, vt, kmean, *, q_col, n_pairs):
    s = h0.shape[0]
    t = MOBA_BLOCK
    nb = s // t
    assert nb <= LANES
    return pl.pallas_call(
        functools.partial(_moba_kernel, nb=nb),
        grid=(n_pairs, nb),
        in_specs=[
            pl.BlockSpec((t, LANES), lambda p, i: (i, q_col + p)),
            pl.BlockSpec((s, 2 * LANES), lambda p, i: (0, p)),
            pl.BlockSpec((1, nb, LANES, t), lambda p, i: (p, 0, 0, 0)),
            pl.BlockSpec((LANES, LANES), lambda p, i: (p, 0)),
        ],
        out_specs=pl.BlockSpec((t, LANES), lambda p, i: (i, p)),
        out_shape=jax.ShapeDtypeStruct((s, n_pairs * LANES), BF16),
        scratch_shapes=[pltpu.VMEM((2, 1, t), F32),
                        pltpu.VMEM((2, LANES + ONES_ROWS, t), F32),
                        pltpu.VMEM((2, t, t), F32)],
        compiler_params=_cparams(("arbitrary", "arbitrary")),
        name="moba",
    )(h0, kaug, vt, kmean)


def _dsa_kernel(it_ref, jt_ref, qi_ref, wi_ref, ki_ref, q_ref, k_ref, vt_ref, o_ref,
                key_ref, thr_ref, need_ref, seen_ref, m_ref, acc_ref, qt_ref,
                *, t, n_heads, topk):
    step = pl.program_id(0)
    i = it_ref[step]
    j = jt_ref[step]
    lane = _iota((t, LANES), 1)
    lo = lane < HEAD_DIM
    key_le_query = _iota((t, t), 0) <= _iota((t, t), 1)
    n_pairs = n_heads // 2

    def split_t(pair):
        pf = pair.astype(F32)
        return (jnp.where(lo, pf, 0.0).T.astype(BF16), jnp.where(lo, 0.0, pf).T.astype(BF16))

    @pl.when(j == 0)
    def _select():
        qi = qi_ref[...]
        wi = wi_ref[0, 0].astype(F32)
        qit = []
        for p in range(IDX_HEADS // 2):
            qit.extend(split_t(qi[:, p * LANES:(p + 1) * LANES]))
        q = q_ref[...]
        for p in range(n_pairs):
            a, b = split_t(q[:, p * LANES:(p + 1) * LANES])
            qt_ref[2 * p] = a
            qt_ref[2 * p + 1] = b

        def score_tile(kt, carry):
            start = pl.multiple_of(kt * t, t)
            kblk = ki_ref[pl.ds(start, t), :]
            sc = jnp.zeros((t, t), F32)
            for h in range(IDX_HEADS):
                sc = sc + wi[h:h + 1, :] * jnp.maximum(_dot(kblk, qit[h]), 0.0)
            bits = pltpu.bitcast(sc, I32)
            key = jnp.where(bits >= 0, bits, bits ^ jnp.int32(0x7FFFFFFF))
            key = jnp.where(jnp.logical_or(kt < i, key_le_query), key, jnp.int32(INT_MIN))
            key_ref[kt] = key
            return carry

        lax.fori_loop(0, i + 1, score_tile, 0)

        def count(pred):
            def cbody(kt, c):
                hit = jnp.where(pred(key_ref[kt]), 1.0, 0.0)
                return c + hit.reshape(t // 8, 8, t).sum(axis=0)
            c = lax.fori_loop(0, i + 1, cbody, jnp.zeros((8, t), F32))
            return jnp.sum(c, axis=0, keepdims=True)

        thr_u = jnp.zeros((1, t), I32)
        for b in range(31, -1, -1):
            bit = jnp.int32(INT_MIN) if b == 31 else jnp.int32(1 << b)
            cand_u = thr_u | bit
            cand_s = cand_u ^ jnp.int32(INT_MIN)
            cnt = count(lambda kk, c=cand_s: kk >= c)
            thr_u = jnp.where(cnt >= float(topk), cand_u, thr_u)
        thr = thr_u ^ jnp.int32(INT_MIN)
        n_gt = count(lambda kk: kk > thr)
        thr_ref[...] = thr
        need_ref[...] = float(topk) - n_gt
        seen_ref[...] = jnp.zeros_like(seen_ref)
        m_ref[...] = jnp.full(m_ref.shape, NEG, F32)
        acc_ref[...] = jnp.zeros_like(acc_ref)

    key = key_ref[j]
    thr = thr_ref[...]
    eq = key == thr
    eq16 = jnp.where(eq, 1.0, 0.0).astype(BF16)
    earlier = (_iota((t, t), 1) < _iota((t, t), 0)).astype(BF16)
    rank = seen_ref[...] + _dot(earlier, eq16)
    sel = (key > thr) | (eq & (rank < need_ref[...]))
    sel = sel & jnp.logical_or(j < i, key_le_query)
    seen_ref[...] += jnp.sum(jnp.where(eq, 1.0, 0.0), axis=0, keepdims=True)
    bias = jnp.where(sel, 0.0, NEG)

    kb = k_ref[...]
    ones = jnp.ones((ONES_ROWS, t), BF16)

    def scores(h):
        p = h // 2
        return _dot(kb[:, p * LANES:(p + 1) * LANES], qt_ref[h]) + bias

    st_next = scores(0)
    for h in range(n_heads):
        st = st_next
        if h + 1 < n_heads:
            st_next = scores(h + 1)
        vt = jnp.concatenate([vt_ref[h // 2, 0], ones], axis=0)
        m_old = m_ref[h]
        m_new = jnp.maximum(m_old, jnp.max(st, axis=0, keepdims=True))
        alpha = jnp.exp2(m_old - m_new)
        pt = jnp.exp2(st - m_new).astype(BF16)
        acc_ref[h] = alpha * acc_ref[h] + _dot(vt, pt)
        m_ref[h] = m_new

    @pl.when(j == i)
    def _finish():
        first = _iota((LANES, t), 0) < HEAD_DIM
        for p in range(n_pairs):
            a0 = acc_ref[2 * p]
            a1 = acc_ref[2 * p + 1]
            out_t = jnp.where(first, a0[:LANES] / a0[LANES:LANES + 1], a1[:LANES] / a1[LANES:LANES + 1])
            o_ref[:, p * LANES:(p + 1) * LANES] = out_t.T.astype(o_ref.dtype)


def _dsa_attention(h1, vt, *, n_heads, t=256):
    s = h1.shape[0]
    width = n_heads * HEAD_DIM
    nt = s // t
    topk = min(DSA_TOPK, s // 4)
    iw = IDX_HEADS * HEAD_DIM
    qi_blk = 2 * width // iw
    ki_blk = (2 * width + iw) // LANES
    wi_grp = width // LANES
    tri = [(i, j) for i in range(nt) for j in range(i + 1)]
    it = jnp.asarray([a for a, _ in tri], I32)
    jt = jnp.asarray([b for _, b in tri], I32)
    grid_spec = pltpu.PrefetchScalarGridSpec(
        num_scalar_prefetch=2,
        grid=(len(tri),),
        in_specs=[
            pl.BlockSpec((t, iw), lambda n, it, jt: (it[n], qi_blk)),
            pl.BlockSpec((1, 1, ONES_ROWS, t), lambda n, it, jt: (wi_grp, it[n], 0, 0)),
            pl.BlockSpec((s, LANES), lambda n, it, jt: (0, ki_blk)),
            pl.BlockSpec((t, width), lambda n, it, jt: (it[n], 0)),
            pl.BlockSpec((t, width), lambda n, it, jt: (jt[n], 1)),
            pl.BlockSpec((n_heads // 2, 1, LANES, t), lambda n, it, jt: (0, jt[n], 0, 0)),
        ],
        out_specs=pl.BlockSpec((t, width), lambda n, it, jt: (it[n], 0)),
        scratch_shapes=[
            pltpu.VMEM((nt, t, t), I32),
            pltpu.VMEM((1, t), I32),
            pltpu.VMEM((1, t), F32),
            pltpu.VMEM((1, t), F32),
            pltpu.VMEM((n_heads, 1, t), F32),
            pltpu.VMEM((n_heads, LANES + ONES_ROWS, t), F32),
            pltpu.VMEM((n_heads, LANES, t), BF16),
        ],
    )
    return pl.pallas_call(
        functools.partial(_dsa_kernel, t=t, n_heads=n_heads, topk=topk),
        grid_spec=grid_spec,
        out_shape=jax.ShapeDtypeStruct((s, width), BF16),
        compiler_params=_cparams(("arbitrary",)),
        name="dsa",
    )(it, jt, h1, vt, h1, h1, h1, vt)


def _layer_norm(y, g, b):
    mu = jnp.mean(y, axis=-1, keepdims=True)
    yc = y - mu
    var = jnp.mean(yc * yc, axis=-1, keepdims=True)
    return yc * lax.rsqrt(var + LN_EPS) * g + b


def _outproj_kernel(*refs, n_in, alpha):
    o_refs = refs[:n_in]
    w_refs = refs[n_in:2 * n_in]
    x_ref, g_ref, b_ref, out_ref = refs[2 * n_in:]
    y = alpha * x_ref[...]
    for o_ref, w_ref in zip(o_refs, w_refs):
        y = y + _dot(o_ref[...], w_ref[...])
    out_ref[...] = _layer_norm(y, g_ref[...], b_ref[...])


def _outproj_ln(os_, ws, x, g, b, alpha, *, tm=256):
    s, d = x.shape
    n_in = len(os_)
    in_specs = ([pl.BlockSpec((tm, o.shape[1]), lambda i: (i, 0)) for o in os_]
                + [pl.BlockSpec(w.shape, lambda i: (0, 0)) for w in ws]
                + [pl.BlockSpec((tm, d), lambda i: (i, 0)),
                   pl.BlockSpec((1, d), lambda i: (0, 0)),
                   pl.BlockSpec((1, d), lambda i: (0, 0))])
    return pl.pallas_call(
        functools.partial(_outproj_kernel, n_in=n_in, alpha=alpha),
        grid=(s // tm,),
        in_specs=in_specs,
        out_specs=pl.BlockSpec((tm, d), lambda i: (i, 0)),
        out_shape=jax.ShapeDtypeStruct((s, d), F32),
        compiler_params=_cparams(("arbitrary",)),
        name="out_proj_ln",
    )(*os_, *ws, x, g, b)


def _moe_kernel(x_ref, wr_ref, br_ref, w1g_ref, w1u_ref, b1g_ref, b1u_ref, w2_ref, b2_ref,
                g_ref, b_ref, out_ref,
                xb_ref, rank_ref, rank_t_ref, gate_t_ref, cnt_ref, acc_ref,
                *, tt, ch, alpha):
    e = pl.program_id(1)
    n_e = pl.num_programs(1)

    @pl.when(e == 0)
    def _route():
        x = x_ref[...]
        xb_ref[...] = x.astype(BF16)
        logits = lax.dot_general(wr_ref[...], x, (((1,), (1,)), ((), ())),
                                 precision=lax.Precision.HIGHEST,
                                 preferred_element_type=F32) + br_ref[...]
        row = _iota(logits.shape, 0).astype(F32)
        work = jnp.where(row < N_EXPERTS, logits, -jnp.inf)
        chosen = row < 0
        top = None
        for r in range(TOP_K):
            m = jnp.max(work, axis=0, keepdims=True)
            idx = jnp.min(jnp.where(work == m, row, float(LANES)), axis=0, keepdims=True)
            pick = row == idx
            chosen = chosen | pick
            work = jnp.where(pick, -jnp.inf, work)
            if r == 0:
                top = m
        ex = jnp.where(chosen, jnp.exp(logits - top), 0.0)
        gate = ex / jnp.sum(ex, axis=0, keepdims=True)
        sel16 = jnp.where(chosen, 1.0, 0.0).astype(BF16)
        before = (_iota((tt, tt), 0) < _iota((tt, tt), 1)).astype(BF16)
        pos = _dot(sel16, before)
        rank = jnp.where(chosen, pos, -1.0)
        rank_ref[...] = rank
        rank_t_ref[...] = rank.T
        gate_t_ref[...] = gate.T
        cnt = jnp.sum(jnp.where(chosen, 1.0, 0.0), axis=1, keepdims=True)
        cnt_ref[...] = jnp.broadcast_to(cnt, cnt_ref.shape)
        acc_ref[...] = jnp.zeros_like(acc_ref)

    lane = _iota((tt, LANES), 1)
    on_e = lane == e
    rank_col = jnp.sum(jnp.where(on_e, rank_t_ref[...], 0.0), axis=1, keepdims=True)
    gate_col = jnp.sum(jnp.where(on_e, gate_t_ref[...], 0.0), axis=1, keepdims=True)
    rank_row = rank_ref[pl.ds(e, 1), :]
    n_rows = jnp.max(cnt_ref[pl.ds(e, 1), :]).astype(I32)
    n_chunks = lax.shift_right_logical(n_rows + (ch - 1), int(np.log2(ch)))
    w1g = w1g_ref[0]
    w1u = w1u_ref[0]
    w2 = w2_ref[0]
    b1g = b1g_ref[0]
    b1u = b1u_ref[0]
    b2 = b2_ref[0]

    def chunk(c, carry):
        r0 = (c * ch).astype(F32)
        gather = (rank_row == (r0 + _iota((ch, tt), 0).astype(F32))).astype(BF16)
        xg = _dot(gather, xb_ref[...]).astype(BF16)
        hg = jnp.minimum(_dot(xg, w1g) + b1g, SWIGLU_LIMIT)
        hu = jnp.clip(_dot(xg, w1u) + b1u, -SWIGLU_LIMIT, SWIGLU_LIMIT)
        act = (hu + 1.0) * (hg * jax.nn.sigmoid(SWIGLU_ALPHA * hg))
        y = _dot(act.astype(BF16), w2) + b2
        scatter = jnp.where(rank_col == (r0 + _iota((tt, ch), 1).astype(F32)), gate_col, 0.0)
        acc_ref[...] += _dot(scatter.astype(BF16), y.astype(BF16))
        return carry

    lax.fori_loop(0, n_chunks, chunk, 0)

    @pl.when(e == n_e - 1)
    def _finish():
        out_ref[...] = _layer_norm(alpha * x_ref[...] + acc_ref[...], g_ref[...], b_ref[...])


def _moe_ln(x, wr_t, br, w1g, w1u, b1g, b1u, w2, b2, g, b, alpha, *, tt=1024, ch=128):
    t, d = x.shape
    tt = min(tt, t)
    f = w2.shape[1]
    return pl.pallas_call(
        functools.partial(_moe_kernel, tt=tt, ch=ch, alpha=alpha),
        grid=(t // tt, N_EXPERTS),
        in_specs=[
            pl.BlockSpec((tt, d), lambda i, e: (i, 0)),
            pl.BlockSpec((LANES, d), lambda i, e: (0, 0)),
            pl.BlockSpec((LANES, 1), lambda i, e: (0, 0)),
            pl.BlockSpec((1, d, f), lambda i, e: (e, 0, 0)),
            pl.BlockSpec((1, d, f), lambda i, e: (e, 0, 0)),
            pl.BlockSpec((1, 1, f), lambda i, e: (e, 0, 0)),
            pl.BlockSpec((1, 1, f), lambda i, e: (e, 0, 0)),
            pl.BlockSpec((1, f, d), lambda i, e: (e, 0, 0)),
            pl.BlockSpec((1, 1, d), lambda i, e: (e, 0, 0)),
            pl.BlockSpec((1, d), lambda i, e: (0, 0)),
            pl.BlockSpec((1, d), lambda i, e: (0, 0)),
        ],
        out_specs=pl.BlockSpec((tt, d), lambda i, e: (i, 0)),
        out_shape=jax.ShapeDtypeStruct((t, d), F32),
        scratch_shapes=[
            pltpu.VMEM((tt, d), BF16),
            pltpu.VMEM((LANES, tt), F32),
            pltpu.VMEM((tt, LANES), F32),
            pltpu.VMEM((tt, LANES), F32),
            pltpu.VMEM((LANES, LANES), F32),
            pltpu.VMEM((tt, d), F32),
        ],
        compiler_params=_cparams(("arbitrary", "arbitrary")),
        name="moe_ln",
    )(x, wr_t, br, w1g, w1u, b1g, b1u, w2, b2, g, b)


def _ple_kernel(x_ref, p_ref, wg_ref, wp_ref, o_ref):
    x = x_ref[...]
    gate = jax.nn.sigmoid(_dot(x.astype(BF16), wg_ref[...]))
    emb = _dot(p_ref[...].astype(BF16), wp_ref[...])
    o_ref[...] = x + gate * emb


def _ple(x, p, wg, wp, *, tm=512):
    s, d = x.shape
    dp = p.shape[1]
    return pl.pallas_call(
        _ple_kernel,
        grid=(s // tm,),
        in_specs=[
            pl.BlockSpec((tm, d), lambda i: (i, 0)),
            pl.BlockSpec((tm, dp), lambda i: (i, 0)),
            pl.BlockSpec((d, d), lambda i: (0, 0)),
            pl.BlockSpec((dp, d), lambda i: (0, 0)),
        ],
        out_specs=pl.BlockSpec((tm, d), lambda i: (i, 0)),
        out_shape=jax.ShapeDtypeStruct((s, d), F32),
        compiler_params=_cparams(("arbitrary",)),
        name="ple",
    )(x, p, wg, wp)


def _mixer_ab(x, w_in, cos_t, sin_t, *, sb_heads, moba_heads):
    s = x.shape[0]
    wa = sb_heads * HEAD_DIM
    wb = moba_heads * HEAD_DIM
    qa, ka, va, qb, kb, vb = jnp.split(w_in, np.cumsum([wa, wa, wa, wb, wb]).tolist(), axis=1)
    w = jnp.concatenate([qb, kb, qa, ka, va], axis=1).astype(BF16)
    att = HEAD_DIM ** -0.5
    scale = jnp.concatenate([jnp.full((wb,), att * LOG2E), jnp.ones((wb,)), jnp.full((wa,), att),
                             jnp.ones((2 * wa,))]).astype(F32)[None]
    tn = cos_t.shape[1]
    h0 = _proj(x, w, scale, cos_t, sin_t, ((0, 2 * wb // tn),), tn=tn)
    n_pairs = moba_heads // 2
    vt = _proj_t(x, vb.astype(BF16), jnp.ones((1, wb), F32), groups=n_pairs)
    c = lambda off: off // LANES
    o_a = _sb_attention(h0, q_col=c(2 * wb), k_col=c(2 * wb + wa), v_col=c(2 * wb + 2 * wa),
                        n_pairs=sb_heads // 2)
    nb = s // MOBA_BLOCK
    kaug, kmean = _moba_prep(h0, k_col=c(wb), n_pairs=n_pairs)
    kmean = jnp.pad(kmean.reshape(n_pairs, nb, LANES), ((0, 0), (0, LANES - nb), (0, 0)))
    kmean = kmean.reshape(n_pairs * LANES, LANES)
    o_b = _moba_attention(h0, kaug, vt, kmean, q_col=0, n_pairs=n_pairs)
    return o_a, o_b


def _mixer_c(x, w_in, cos_t, sin_t, *, n_heads):
    width = n_heads * HEAD_DIM
    iw = IDX_HEADS * HEAD_DIM
    d = w_in.shape[0]
    q, k, v, qi, ki, wi = jnp.split(w_in, np.cumsum([width, width, width, iw, HEAD_DIM]).tolist(), axis=1)
    tn = cos_t.shape[1]
    ki_pad = jnp.zeros((d, tn - 2 * HEAD_DIM), w_in.dtype)
    w = jnp.concatenate([q, k, qi, ki, ki, ki_pad], axis=1).astype(BF16)
    att = HEAD_DIM ** -0.5
    scale = jnp.concatenate([jnp.full((width,), att * LOG2E), jnp.ones((width,)), jnp.full((iw,), att),
                             jnp.ones((tn,))]).astype(F32)[None]
    h1 = _proj(x, w, scale, cos_t, sin_t, ((0, (2 * width + iw) // tn + 1),), tn=tn)
    wi_pad = jnp.zeros((d, LANES - IDX_HEADS), w_in.dtype)
    wv = jnp.concatenate([v, wi, wi_pad], axis=1).astype(BF16)
    vscale = jnp.concatenate([jnp.ones((width,)), jnp.full((LANES,), IDX_HEADS ** -0.5)]).astype(F32)[None]
    vt = _proj_t(x, wv, vscale, groups=3)
    return _dsa_attention(h1, vt, n_heads=n_heads)


def _split_kernel(w_ref, perm_ref, g_ref, u_ref):
    half = g_ref.shape[2]
    both = _dot(w_ref[0].astype(BF16), perm_ref[...])
    g_ref[0] = both[:, :half].astype(BF16)
    u_ref[0] = both[:, half:].astype(BF16)


def _split_gate_up(w1, *, tc=512):
    n_e, d, f2 = w1.shape
    half = tc // 2
    src = jnp.concatenate([2 * jnp.arange(half), 2 * jnp.arange(half) + 1])
    perm = (jnp.arange(tc)[:, None] == src[None, :]).astype(BF16)
    out = jax.ShapeDtypeStruct((n_e, d, f2 // 2), BF16)
    return pl.pallas_call(
        _split_kernel,
        grid=(n_e, f2 // tc),
        in_specs=[pl.BlockSpec((1, d, tc), lambda e, c: (e, 0, c)),
                  pl.BlockSpec((tc, tc), lambda e, c: (0, 0))],
        out_specs=[pl.BlockSpec((1, d, half), lambda e, c: (e, 0, c)),
                   pl.BlockSpec((1, d, half), lambda e, c: (e, 0, c))],
        out_shape=[out, out],
        compiler_params=_cparams(("arbitrary", "arbitrary")),
        name="split_gate_up",
    )(w1, perm)


def _moe_weights(w_r, b_r, w1, b1, w2, b2):
    wr_t = jnp.pad(w_r.T, ((0, LANES - N_EXPERTS), (0, 0)))
    br = jnp.pad(b_r, (0, LANES - N_EXPERTS))[:, None]
    w1g, w1u = _split_gate_up(w1)
    return (wr_t, br, w1g, w1u, b1[:, None, 0::2], b1[:, None, 1::2], w2.astype(BF16), b2[:, None, :])


def kernel(x, p, ab_w_in, ab_w_out, c_w_in, c_w_out, ln_g, ln_b, router_w, router_b,
           moe_w1, moe_b1, moe_w2, moe_b2, ple_w_proj, ple_w_gate):
    b, s, d = x.shape
    assert b == 1
    depth = p.shape[0]
    alpha = float((2 * depth) ** 0.25)
    cos_t, sin_t = _rope_tables(s, 2 * LANES)
    sb_heads = moba_heads = ab_w_out.shape[1] // (2 * HEAD_DIM)
    dsa_heads = c_w_out.shape[1] // HEAD_DIM
    xs = x[0]
    for i in range(depth):
        j = i // 2
        g = ln_g[i][:, None, :]
        bb = ln_b[i][:, None, :]
        if i % 2 == 0:
            o_a, o_b = _mixer_ab(xs, ab_w_in[j], cos_t, sin_t, sb_heads=sb_heads, moba_heads=moba_heads)
            w_out = ab_w_out[j].astype(BF16)
            wa = sb_heads * HEAD_DIM
            xs = _outproj_ln([o_a, o_b], [w_out[:wa], w_out[wa:]], xs, g[0], bb[0], alpha)
        else:
            o_c = _mixer_c(xs, c_w_in[j], cos_t, sin_t, n_heads=dsa_heads)
            xs = _outproj_ln([o_c], [c_w_out[j].astype(BF16)], xs, g[0], bb[0], alpha)
        mw = _moe_weights(router_w[i], router_b[i], moe_w1[i], moe_b1[i], moe_w2[i], moe_b2[i])
        xs = _moe_ln(xs, *mw, g[1], bb[1], alpha)
        xs = _ple(xs, p[i, 0], ple_w_gate[i].astype(BF16), ple_w_proj[i].astype(BF16))
    return xs[None]
```

```python
import functools

import jax
import jax.numpy as jnp
import numpy as np
from jax import lax
from jax.experimental import pallas as pl
from jax.experimental.pallas import tpu as pltpu

F32 = jnp.float32
BF16 = jnp.bfloat16
I32 = jnp.int32

HEAD_DIM = 64
LANES = 128
MOBA_BLOCK = 256
MOBA_TOPK = 3
DSA_TOPK = 256
IDX_HEADS = 8
N_EXPERTS = 32
TOP_K = 4
SWIGLU_LIMIT = 7.0
SWIGLU_ALPHA = 1.702
ROPE_THETA = 10000.0
LN_EPS = 1e-5
NEG = -1e30
EXP_FLUSH = -88.0
INT_MIN = -(2 ** 31)
LOG2E = 1.4426950408889634
ONES_ROWS = 16
VMEM_LIMIT = 56 * 1024 * 1024


def _cparams(sem, flags=None):
    return pltpu.CompilerParams(dimension_semantics=sem, vmem_limit_bytes=VMEM_LIMIT, flags=flags)


def _dot(a, b):
    return jnp.dot(a, b, preferred_element_type=F32)


def _dot_nt(a, b):
    return lax.dot_general(a, b, (((1,), (1,)), ((), ())), preferred_element_type=F32)


def _dot_tn(a, b):
    return lax.dot_general(a, b, (((0,), (0,)), ((), ())), preferred_element_type=F32)


def _iota(shape, dim):
    return lax.broadcasted_iota(I32, shape, dim)


def _proj_kernel(x_ref, w_ref, scale_ref, cos_ref, sin_ref, o_ref, xb_ref, *, rope_ranges, tn):
    j = pl.program_id(1)

    @pl.when(j == 0)
    def _():
        xb_ref[...] = x_ref[...].astype(BF16)

    h = _dot(xb_ref[...], w_ref[...]) * scale_ref[...]
    is_rope = functools.reduce(jnp.logical_or, [(j >= a) & (j < b) for a, b in rope_ranges])

    @pl.when(is_rope)
    def _():
        lane = _iota(h.shape, 1)
        first_half = (lane % HEAD_DIM) < (HEAD_DIM // 2)
        partner = jnp.where(first_half, pltpu.roll(h, tn - HEAD_DIM // 2, 1),
                            pltpu.roll(h, HEAD_DIM // 2, 1))
        o_ref[...] = (h * cos_ref[...] + partner * sin_ref[...]).astype(o_ref.dtype)

    @pl.when(jnp.logical_not(is_rope))
    def _():
        o_ref[...] = h.astype(o_ref.dtype)


def _proj(x, w, scale, cos_t, sin_t, rope_ranges, *, tm=512, tn=256):
    s, d = x.shape
    n = w.shape[1]
    return pl.pallas_call(
        functools.partial(_proj_kernel, rope_ranges=rope_ranges, tn=tn),
        grid=(s // tm, n // tn),
        in_specs=[
            pl.BlockSpec((tm, d), lambda i, j: (i, 0)),
            pl.BlockSpec((d, tn), lambda i, j: (0, j)),
            pl.BlockSpec((1, tn), lambda i, j: (0, j)),
            pl.BlockSpec((tm, tn), lambda i, j: (i, 0)),
            pl.BlockSpec((tm, tn), lambda i, j: (i, 0)),
        ],
        out_specs=pl.BlockSpec((tm, tn), lambda i, j: (i, j)),
        out_shape=jax.ShapeDtypeStruct((s, n), BF16),
        scratch_shapes=[pltpu.VMEM((tm, d), BF16)],
        compiler_params=_cparams(("arbitrary", "arbitrary")),
        name="in_proj",
    )(x, w, scale, cos_t, sin_t)


def _proj_t_kernel(x_ref, w_ref, scale_ref, o_ref, *, groups):
    h = _dot(x_ref[...].astype(BF16), w_ref[...]) * scale_ref[...]
    for g in range(groups):
        o_ref[g, 0] = h[:, g * LANES:(g + 1) * LANES].T.astype(o_ref.dtype)


def _proj_t(x, w, scale, *, groups, tm=256):
    s, d = x.shape
    n = w.shape[1]
    tc = groups * LANES
    return pl.pallas_call(
        functools.partial(_proj_t_kernel, groups=groups),
        grid=(s // tm, n // tc),
        in_specs=[
            pl.BlockSpec((tm, d), lambda i, j: (i, 0)),
            pl.BlockSpec((d, tc), lambda i, j: (0, j)),
            pl.BlockSpec((1, tc), lambda i, j: (0, j)),
        ],
        out_specs=pl.BlockSpec((groups, 1, LANES, tm), lambda i, j: (j, i, 0, 0)),
        out_shape=jax.ShapeDtypeStruct((n // LANES, s // tm, LANES, tm), BF16),
        compiler_params=_cparams(("arbitrary", "arbitrary")),
        name="proj_t",
    )(x, w, scale)


def _rope_tables(s, width):
    half = HEAD_DIM // 2
    inv = ROPE_THETA ** (-jnp.arange(half, dtype=F32) / half)
    ang = jnp.arange(s, dtype=F32)[:, None] * inv[None, :]
    cos = jnp.cos(ang)
    sin = jnp.sin(ang)
    reps = width // HEAD_DIM
    cos_t = jnp.tile(jnp.concatenate([cos, cos], axis=1), (1, reps))
    sin_t = jnp.tile(jnp.concatenate([-sin, sin], axis=1), (1, reps))
    return cos_t, sin_t


def _sb_kernel(q_ref, k_ref, v_ref, o_ref, acc_ref, run_ref, *, t):
    i = pl.program_id(1)
    q = q_ref[...]
    lane = _iota((t, LANES), 1)
    lo = lane < HEAD_DIM
    zero = jnp.zeros_like(q)
    qh = (jnp.where(lo, q, zero), jnp.where(lo, zero, q))
    row = _iota((t, t), 0)
    col = _iota((t, t), 1)
    incl = (row >= col).astype(BF16)
    ones = jnp.ones((t, LANES), BF16)
    strict = col < row

    acc_ref[...] = jnp.zeros_like(acc_ref)
    run_ref[...] = jnp.zeros_like(run_ref)

    def tile(kt, diagonal):
        start = pl.multiple_of(kt * t, t)
        kblk = k_ref[pl.ds(start, t), :]
        vblk = v_ref[pl.ds(start, t), :]
        for h in range(2):
            z = _dot_nt(qh[h], kblk)
            log_stay = -(jnp.maximum(z, 0.0) + jnp.log1p(jnp.exp(-jnp.abs(z))))
            if diagonal:
                log_stay = jnp.where(strict, log_stay, 0.0)
            ls16 = log_stay.astype(BF16)
            cum = _dot(ls16, incl)
            a = jnp.exp(z + cum + run_ref[h])
            if diagonal:
                a = jnp.where(strict, a, 0.0)
            acc_ref[h] += _dot(a.astype(BF16), vblk)
            run_ref[h] += _dot(ls16, ones)

    tile(i, True)

    def cond(c):
        kt, live = c
        return jnp.logical_and(kt >= 0, live)

    def body(c):
        kt, _ = c
        tile(kt, False)
        live = jnp.max(jnp.maximum(run_ref[0], run_ref[1])) > EXP_FLUSH
        return kt - 1, live

    lax.while_loop(cond, body, (i - 1, jnp.bool_(True)))
    o_ref[...] = jnp.where(lo, acc_ref[0], acc_ref[1]).astype(o_ref.dtype)


def _sb_attention(h0, *, q_col, k_col, v_col, n_pairs, t=128):
    s = h0.shape[0]
    return pl.pallas_call(
        functools.partial(_sb_kernel, t=t),
        grid=(n_pairs, s // t),
        in_specs=[
            pl.BlockSpec((t, LANES), lambda p, i: (i, q_col + p)),
            pl.BlockSpec((s, LANES), lambda p, i: (0, k_col + p)),
            pl.BlockSpec((s, LANES), lambda p, i: (0, v_col + p)),
        ],
        out_specs=pl.BlockSpec((t, LANES), lambda p, i: (i, p)),
        out_shape=jax.ShapeDtypeStruct((s, n_pairs * LANES), BF16),
        scratch_shapes=[pltpu.VMEM((2, t, LANES), F32), pltpu.VMEM((2, t, LANES), F32)],
        compiler_params=_cparams(("arbitrary", "arbitrary")),
        name="stick_breaking",
    )(h0, h0, h0)


def _moba_prep_kernel(k_ref, kaug_ref, kmean_ref):
    b = pl.program_id(1)
    k = k_ref[...]
    kmean_ref[...] = jnp.mean(k.astype(F32), axis=0, keepdims=True)[None]
    onehot = (_iota(k.shape, 1) == b).astype(BF16)
    kaug_ref[...] = jnp.concatenate([k, onehot], axis=1)


def _moba_prep(h0, *, k_col, n_pairs):
    s = h0.shape[0]
    nb = s // MOBA_BLOCK
    return pl.pallas_call(
        _moba_prep_kernel,
        grid=(n_pairs, nb),
        in_specs=[pl.BlockSpec((MOBA_BLOCK, LANES), lambda p, b: (b, k_col + p))],
        out_specs=[
            pl.BlockSpec((MOBA_BLOCK, 2 * LANES), lambda p, b: (b, p)),
            pl.BlockSpec((1, 1, LANES), lambda p, b: (p * nb + b, 0, 0)),
        ],
        out_shape=[
            jax.ShapeDtypeStruct((s, n_pairs * 2 * LANES), BF16),
            jax.ShapeDtypeStruct((n_pairs * nb, 1, LANES), F32),
        ],
        compiler_params=_cparams(("arbitrary", "arbitrary")),
        name="moba_prep",
    )(h0)


def _first_argmax(x, lane):
    m = jnp.max(x, axis=1, keepdims=True)
    idx = jnp.min(jnp.where(x == m, lane, float(LANES)), axis=1, keepdims=True)
    return m, idx


def _moba_kernel(q_ref, kaug_ref, vt_ref, kmean_ref, o_ref, m_ref, acc_ref, st_ref, *, nb):
    t = MOBA_BLOCK
    i = pl.program_id(1)
    q = q_ref[...]
    lane = _iota((t, LANES), 1)
    lo = lane < HEAD_DIM
    zero = jnp.zeros_like(q)
    qh = (jnp.where(lo, q, zero), jnp.where(lo, zero, q))
    kmean = kmean_ref[...]
    km_hi = kmean.astype(BF16)
    km_lo = (kmean - km_hi.astype(F32)).astype(BF16)
    key_le_query = _iota((t, t), 0) <= _iota((t, t), 1)
    ones = jnp.ones((ONES_ROWS, t), BF16)

    q_aug = []
    for h in range(2):
        gate = _dot_nt(qh[h], km_hi) + _dot_nt(qh[h], km_lo)
        gate = jnp.where(lane < i, gate, -jnp.inf)
        chosen = lane == i
        lane_f = lane.astype(F32)
        for _ in range(MOBA_TOPK):
            m, idx = _first_argmax(gate, lane_f)
            pick = (lane_f == idx) & (m > -jnp.inf)
            chosen = chosen | pick
            gate = jnp.where(lane_f == idx, -jnp.inf, gate)
        bias = jnp.where(chosen | (lane >= nb), 0.0, NEG)
        q_aug.append(jnp.concatenate([qh[h].astype(F32), bias], axis=1).T.astype(BF16))
    q_both = jnp.concatenate(q_aug, axis=1)

    def scores(kt):
        start = pl.multiple_of(kt * t, t)
        return _dot(kaug_ref[pl.ds(start, t), :], q_both)

    m_ref[...] = jnp.full(m_ref.shape, NEG, F32)
    acc_ref[...] = jnp.zeros_like(acc_ref)
    st_ref[...] = jnp.where(jnp.concatenate([key_le_query, key_le_query], axis=1), scores(i), NEG)

    def body(n, carry):
        cur = jnp.where(n == 0, i, n - 1)
        nxt = jnp.minimum(n, jnp.maximum(i - 1, 0))
        st_next = scores(nxt)
        vt = jnp.concatenate([vt_ref[0, cur], ones], axis=0)
        st = st_ref[...]
        m_old = m_ref[...]
        m_new = jnp.maximum(m_old, jnp.max(st, axis=0, keepdims=True))
        alpha = jnp.exp2(m_old - m_new)
        pt = jnp.exp2(st - m_new).astype(BF16)
        acc_ref[...] = alpha * acc_ref[...] + _dot(vt, pt)
        m_ref[...] = m_new
        st_ref[...] = st_next
        return carry

    lax.fori_loop(0, i + 1, body, 0)
    acc = acc_ref[...]
    outs = [acc[:LANES, h * t:(h + 1) * t] / acc[LANES:LANES + 1, h * t:(h + 1) * t] for h in range(2)]
    first = _iota((LANES, t), 0) < HEAD_DIM
    o_ref[...] = jnp.where(first, outs[0], outs[1]).T.astype(o_ref.dtype)


def _moba_attention(h0, kaug, vt, kmean, *, q_col, n_pairs):
    s = h0.shape[0]
    t = MOBA_BLOCK
    nb = s // t
    assert nb <= LANES
    return pl.pallas_call(
        functools.partial(_moba_kernel, nb=nb),
        grid=(n_pairs, nb),
        in_specs=[
            pl.BlockSpec((t, LANES), lambda p, i: (i, q_col + p)),
            pl.BlockSpec((s, 2 * LANES), lambda p, i: (0, p)),
            pl.BlockSpec((1, nb, LANES, t), lambda p, i: (p, 0, 0, 0)),
            pl.BlockSpec((LANES, LANES), lambda p, i: (p, 0)),
        ],
        out_specs=pl.BlockSpec((t, LANES), lambda p, i: (i, p)),
        out_shape=jax.ShapeDtypeStruct((s, n_pairs * LANES), BF16),
        scratch_shapes=[pltpu.VMEM((1, 2 * t), F32),
                        pltpu.VMEM((LANES + ONES_ROWS, 2 * t), F32),
                        pltpu.VMEM((t, 2 * t), F32)],
        compiler_params=_cparams(("arbitrary", "arbitrary")),
        name="moba",
    )(h0, kaug, vt, kmean)


def _dsa_kernel(it_ref, jt_ref, qi_ref, wi_ref, ki_ref, q_ref, k_ref, vt_ref, o_ref,
                key_ref, thr_ref, need_ref, seen_ref, m_ref, acc_ref, qt_ref,
                *, t, n_heads, topk):
    step = pl.program_id(0)
    i = it_ref[step]
    j = jt_ref[step]
    lane = _iota((t, LANES), 1)
    lo = lane < HEAD_DIM
    key_le_query = _iota((t, t), 0) <= _iota((t, t), 1)
    n_pairs = n_heads // 2

    def split_t(pair):
        pf = pair.astype(F32)
        return (jnp.where(lo, pf, 0.0).T.astype(BF16), jnp.where(lo, 0.0, pf).T.astype(BF16))

    @pl.when(j == 0)
    def _select():
        qi = qi_ref[...]
        wi = wi_ref[0, 0].astype(F32)
        qit = []
        for p in range(IDX_HEADS // 2):
            qit.extend(split_t(qi[:, p * LANES:(p + 1) * LANES]))
        q = q_ref[...]
        for p in range(n_pairs):
            qt_ref[p] = jnp.concatenate(split_t(q[:, p * LANES:(p + 1) * LANES]), axis=1)

        def score_tile(kt, carry):
            start = pl.multiple_of(kt * t, t)
            kblk = ki_ref[pl.ds(start, t), :]
            sc = jnp.zeros((t, t), F32)
            for h in range(IDX_HEADS):
                sc = sc + wi[h:h + 1, :] * jnp.maximum(_dot(kblk, qit[h]), 0.0)
            bits = pltpu.bitcast(sc, I32)
            key = jnp.where(bits >= 0, bits, bits ^ jnp.int32(0x7FFFFFFF))
            key = jnp.where(jnp.logical_or(kt < i, key_le_query), key, jnp.int32(INT_MIN))
            key_ref[kt] = key
            return carry

        lax.fori_loop(0, i + 1, score_tile, 0)
        key_ref[i + 1] = jnp.full((t, t), INT_MIN, I32)

        def count(pred):
            def cbody(kt2, c):
                for kt in (2 * kt2, 2 * kt2 + 1):
                    hit = jnp.where(pred(key_ref[kt]), 1.0, 0.0)
                    c = c + hit.reshape(t // 8, 8, t).sum(axis=0)
                return c
            c = lax.fori_loop(0, lax.shift_right_logical(i + 2, 1), cbody, jnp.zeros((8, t), F32))
            return jnp.sum(c, axis=0, keepdims=True)

        thr_u = jnp.zeros((1, t), I32)
        for b in range(31, -1, -1):
            bit = jnp.int32(INT_MIN) if b == 31 else jnp.int32(1 << b)
            cand_u = thr_u | bit
            cand_s = cand_u ^ jnp.int32(INT_MIN)
            cnt = count(lambda kk, c=cand_s: kk >= c)
            thr_u = jnp.where(cnt >= float(topk), cand_u, thr_u)
        thr = thr_u ^ jnp.int32(INT_MIN)
        n_gt = count(lambda kk: kk > thr)
        thr_ref[...] = thr
        need_ref[...] = float(topk) - n_gt
        seen_ref[...] = jnp.zeros_like(seen_ref)
        m_ref[...] = jnp.full(m_ref.shape, NEG, F32)
        acc_ref[...] = jnp.zeros_like(acc_ref)

    key = key_ref[j]
    thr = thr_ref[...]
    eq = key == thr
    eq16 = jnp.where(eq, 1.0, 0.0).astype(BF16)
    earlier = (_iota((t, t), 1) < _iota((t, t), 0)).astype(BF16)
    rank = seen_ref[...] + _dot(earlier, eq16)
    sel = (key > thr) | (eq & (rank < need_ref[...]))
    sel = sel & jnp.logical_or(j < i, key_le_query)
    seen_ref[...] += jnp.sum(jnp.where(eq, 1.0, 0.0), axis=0, keepdims=True)
    bias = jnp.where(sel, 0.0, NEG)
    bias = jnp.concatenate([bias, bias], axis=1)

    kb = k_ref[...]
    ones = jnp.ones((ONES_ROWS, t), BF16)

    def scores(p):
        return _dot(kb[:, p * LANES:(p + 1) * LANES], qt_ref[p]) + bias

    st_next = scores(0)
    for p in range(n_pairs):
        st = st_next
        if p + 1 < n_pairs:
            st_next = scores(p + 1)
        vt = jnp.concatenate([vt_ref[p, 0], ones], axis=0)
        m_old = m_ref[p]
        m_new = jnp.maximum(m_old, jnp.max(st, axis=0, keepdims=True))
        alpha = jnp.exp2(m_old - m_new)
        pt = jnp.exp2(st - m_new).astype(BF16)
        acc_ref[p] = alpha * acc_ref[p] + _dot(vt, pt)
        m_ref[p] = m_new

    @pl.when(j == i)
    def _finish():
        first = _iota((LANES, t), 0) < HEAD_DIM
        for p in range(n_pairs):
            acc = acc_ref[p]
            a0, a1 = acc[:, :t], acc[:, t:]
            out_t = jnp.where(first, a0[:LANES] / a0[LANES:LANES + 1], a1[:LANES] / a1[LANES:LANES + 1])
            o_ref[:, p * LANES:(p + 1) * LANES] = out_t.T.astype(o_ref.dtype)


def _dsa_attention(h1, vt, *, n_heads, t=256):
    s = h1.shape[0]
    width = n_heads * HEAD_DIM
    nt = s // t
    topk = min(DSA_TOPK, s // 4)
    iw = IDX_HEADS * HEAD_DIM
    qi_blk = 2 * width // iw
    ki_blk = (2 * width + iw) // LANES
    wi_grp = width // LANES
    tri = [(i, j) for i in range(nt) for j in range(i + 1)]
    it = jnp.asarray([a for a, _ in tri], I32)
    jt = jnp.asarray([b for _, b in tri], I32)
    grid_spec = pltpu.PrefetchScalarGridSpec(
        num_scalar_prefetch=2,
        grid=(len(tri),),
        in_specs=[
            pl.BlockSpec((t, iw), lambda n, it, jt: (it[n], qi_blk)),
            pl.BlockSpec((1, 1, ONES_ROWS, t), lambda n, it, jt: (wi_grp, it[n], 0, 0)),
            pl.BlockSpec((s, LANES), lambda n, it, jt: (0, ki_blk)),
            pl.BlockSpec((t, width), lambda n, it, jt: (it[n], 0)),
            pl.BlockSpec((t, width), lambda n, it, jt: (jt[n], 1)),
            pl.BlockSpec((n_heads // 2, 1, LANES, t), lambda n, it, jt: (0, jt[n], 0, 0)),
        ],
        out_specs=pl.BlockSpec((t, width), lambda n, it, jt: (it[n], 0)),
        scratch_shapes=[
            pltpu.VMEM((nt + 1, t, t), I32),
            pltpu.VMEM((1, t), I32),
            pltpu.VMEM((1, t), F32),
            pltpu.VMEM((1, t), F32),
            pltpu.VMEM((n_heads // 2, 1, 2 * t), F32),
            pltpu.VMEM((n_heads // 2, LANES + ONES_ROWS, 2 * t), F32),
            pltpu.VMEM((n_heads // 2, LANES, 2 * t), BF16),
        ],
    )
    return pl.pallas_call(
        functools.partial(_dsa_kernel, t=t, n_heads=n_heads, topk=topk),
        grid_spec=grid_spec,
        out_shape=jax.ShapeDtypeStruct((s, width), BF16),
        compiler_params=_cparams(("arbitrary",)),
        name="dsa",
    )(it, jt, h1, vt, h1, h1, h1, vt)


def _layer_norm(y, g, b):
    mu = jnp.mean(y, axis=-1, keepdims=True)
    yc = y - mu
    var = jnp.mean(yc * yc, axis=-1, keepdims=True)
    return yc * lax.rsqrt(var + LN_EPS) * g + b


def _outproj_kernel(*refs, n_in, alpha):
    o_refs = refs[:n_in]
    w_refs = refs[n_in:2 * n_in]
    x_ref, g_ref, b_ref, out_ref = refs[2 * n_in:]
    y = alpha * x_ref[...]
    for o_ref, w_ref in zip(o_refs, w_refs):
        y = y + _dot(o_ref[...], w_ref[...])
    out_ref[...] = _layer_norm(y, g_ref[...], b_ref[...])


def _outproj_ln(os_, ws, x, g, b, alpha, *, tm=256):
    s, d = x.shape
    n_in = len(os_)
    in_specs = ([pl.BlockSpec((tm, o.shape[1]), lambda i: (i, 0)) for o in os_]
                + [pl.BlockSpec(w.shape, lambda i: (0, 0)) for w in ws]
                + [pl.BlockSpec((tm, d), lambda i: (i, 0)),
                   pl.BlockSpec((1, d), lambda i: (0, 0)),
                   pl.BlockSpec((1, d), lambda i: (0, 0))])
    return pl.pallas_call(
        functools.partial(_outproj_kernel, n_in=n_in, alpha=alpha),
        grid=(s // tm,),
        in_specs=in_specs,
        out_specs=pl.BlockSpec((tm, d), lambda i: (i, 0)),
        out_shape=jax.ShapeDtypeStruct((s, d), F32),
        compiler_params=_cparams(("arbitrary",)),
        name="out_proj_ln",
    )(*os_, *ws, x, g, b)


def _moe_kernel(x_ref, wr_ref, br_ref, w1g_ref, w1u_ref, b1g_ref, b1u_ref, w2_ref, b2_ref,
                g_ref, b_ref, out_ref,
                xb_ref, rank_ref, rank_t_ref, gate_t_ref, cnt_ref, acc_ref,
                *, tt, ch, alpha):
    e = pl.program_id(1)
    n_e = pl.num_programs(1)

    @pl.when(e == 0)
    def _route():
        x = x_ref[...]
        xb_ref[...] = x.astype(BF16)
        logits = lax.dot_general(wr_ref[...], x, (((1,), (1,)), ((), ())),
                                 precision=lax.Precision.HIGHEST,
                                 preferred_element_type=F32) + br_ref[...]
        row = _iota(logits.shape, 0).astype(F32)
        work = jnp.where(row < N_EXPERTS, logits, -jnp.inf)
        chosen = row < 0
        top = None
        for r in range(TOP_K):
            m = jnp.max(work, axis=0, keepdims=True)
            idx = jnp.min(jnp.where(work == m, row, float(LANES)), axis=0, keepdims=True)
            pick = row == idx
            chosen = chosen | pick
            work = jnp.where(pick, -jnp.inf, work)
            if r == 0:
                top = m
        ex = jnp.where(chosen, jnp.exp(logits - top), 0.0)
        gate = ex / jnp.sum(ex, axis=0, keepdims=True)
        sel16 = jnp.where(chosen, 1.0, 0.0).astype(BF16)
        before = (_iota((tt, tt), 0) < _iota((tt, tt), 1)).astype(BF16)
        pos = _dot(sel16, before)
        rank = jnp.where(chosen, pos, -1.0)
        rank_ref[...] = rank
        rank_t_ref[...] = rank.T
        gate_t_ref[...] = gate.T
        cnt = jnp.sum(jnp.where(chosen, 1.0, 0.0), axis=1, keepdims=True)
        cnt_ref[...] = jnp.broadcast_to(cnt, cnt_ref.shape)
        acc_ref[...] = jnp.zeros_like(acc_ref)

    lane = _iota((tt, LANES), 1)
    on_e = lane == e
    rank_col = jnp.sum(jnp.where(on_e, rank_t_ref[...], 0.0), axis=1, keepdims=True)
    gate_col = jnp.sum(jnp.where(on_e, gate_t_ref[...], 0.0), axis=1, keepdims=True)
    rank_row = rank_ref[pl.ds(e, 1), :]
    n_rows = jnp.max(cnt_ref[pl.ds(e, 1), :]).astype(I32)
    n_chunks = lax.shift_right_logical(n_rows + (ch - 1), int(np.log2(ch)))
    w1g = w1g_ref[0]
    w1u = w1u_ref[0]
    w2 = w2_ref[0]
    b1g = b1g_ref[0]
    b1u = b1u_ref[0]
    b2 = b2_ref[0]

    def chunk(c, carry):
        r0 = (c * ch).astype(F32)
        gather = (rank_row == (r0 + _iota((ch, tt), 0).astype(F32))).astype(BF16)
        xg = _dot(gather, xb_ref[...]).astype(BF16)
        hg = jnp.minimum(_dot(xg, w1g) + b1g, SWIGLU_LIMIT)
        hu = jnp.clip(_dot(xg, w1u) + b1u, -SWIGLU_LIMIT, SWIGLU_LIMIT)
        act = (hu + 1.0) * (hg * jax.nn.sigmoid(SWIGLU_ALPHA * hg))
        y = _dot(act.astype(BF16), w2) + b2
        scatter = jnp.where(rank_col == (r0 + _iota((tt, ch), 1).astype(F32)), gate_col, 0.0)
        acc_ref[...] += _dot(scatter.astype(BF16), y.astype(BF16))
        return carry

    lax.fori_loop(0, n_chunks, chunk, 0)

    @pl.when(e == n_e - 1)
    def _finish():
        out_ref[...] = _layer_norm(alpha * x_ref[...] + acc_ref[...], g_ref[...], b_ref[...])


def _moe_ln(x, wr_t, br, w1g, w1u, b1g, b1u, w2, b2, g, b, alpha, *, tt=1024, ch=128):
    t, d = x.shape
    tt = min(tt, t)
    f = w2.shape[1]
    return pl.pallas_call(
        functools.partial(_moe_kernel, tt=tt, ch=ch, alpha=alpha),
        grid=(t // tt, N_EXPERTS),
        in_specs=[
            pl.BlockSpec((tt, d), lambda i, e: (i, 0)),
            pl.BlockSpec((LANES, d), lambda i, e: (0, 0)),
            pl.BlockSpec((LANES, 1), lambda i, e: (0, 0)),
            pl.BlockSpec((1, d, f), lambda i, e: (e, 0, 0)),
            pl.BlockSpec((1, d, f), lambda i, e: (e, 0, 0)),
            pl.BlockSpec((1, 1, f), lambda i, e: (e, 0, 0)),
            pl.BlockSpec((1, 1, f), lambda i, e: (e, 0, 0)),
            pl.BlockSpec((1, f, d), lambda i, e: (e, 0, 0)),
            pl.BlockSpec((1, 1, d), lambda i, e: (e, 0, 0)),
            pl.BlockSpec((1, d), lambda i, e: (0, 0)),
            pl.BlockSpec((1, d), lambda i, e: (0, 0)),
        ],
        out_specs=pl.BlockSpec((tt, d), lambda i, e: (i, 0)),
        out_shape=jax.ShapeDtypeStruct((t, d), F32),
        scratch_shapes=[
            pltpu.VMEM((tt, d), BF16),
            pltpu.VMEM((LANES, tt), F32),
            pltpu.VMEM((tt, LANES), F32),
            pltpu.VMEM((tt, LANES), F32),
            pltpu.VMEM((LANES, LANES), F32),
            pltpu.VMEM((tt, d), F32),
        ],
        compiler_params=_cparams(("arbitrary", "arbitrary")),
        name="moe_ln",
    )(x, wr_t, br, w1g, w1u, b1g, b1u, w2, b2, g, b)


def _ple_kernel(x_ref, p_ref, wg_ref, wp_ref, o_ref):
    x = x_ref[...]
    gate = jax.nn.sigmoid(_dot(x.astype(BF16), wg_ref[...]))
    emb = _dot(p_ref[...].astype(BF16), wp_ref[...])
    o_ref[...] = x + gate * emb


def _ple(x, p, wg, wp, *, tm=512):
    s, d = x.shape
    dp = p.shape[1]
    return pl.pallas_call(
        _ple_kernel,
        grid=(s // tm,),
        in_specs=[
            pl.BlockSpec((tm, d), lambda i: (i, 0)),
            pl.BlockSpec((tm, dp), lambda i: (i, 0)),
            pl.BlockSpec((d, d), lambda i: (0, 0)),
            pl.BlockSpec((dp, d), lambda i: (0, 0)),
        ],
        out_specs=pl.BlockSpec((tm, d), lambda i: (i, 0)),
        out_shape=jax.ShapeDtypeStruct((s, d), F32),
        compiler_params=_cparams(("arbitrary",)),
        name="ple",
    )(x, p, wg, wp)


def _mixer_ab(x, w_in, cos_t, sin_t, *, sb_heads, moba_heads):
    s = x.shape[0]
    wa = sb_heads * HEAD_DIM
    wb = moba_heads * HEAD_DIM
    qa, ka, va, qb, kb, vb = jnp.split(w_in, np.cumsum([wa, wa, wa, wb, wb]).tolist(), axis=1)
    w = jnp.concatenate([qb, kb, qa, ka, va], axis=1).astype(BF16)
    att = HEAD_DIM ** -0.5
    scale = jnp.concatenate([jnp.full((wb,), att * LOG2E), jnp.ones((wb,)), jnp.full((wa,), att),
                             jnp.ones((2 * wa,))]).astype(F32)[None]
    tn = cos_t.shape[1]
    h0 = _proj(x, w, scale, cos_t, sin_t, ((0, 2 * wb // tn),), tn=tn)
    n_pairs = moba_heads // 2
    vt = _proj_t(x, vb.astype(BF16), jnp.ones((1, wb), F32), groups=n_pairs)
    c = lambda off: off // LANES
    o_a = _sb_attention(h0, q_col=c(2 * wb), k_col=c(2 * wb + wa), v_col=c(2 * wb + 2 * wa),
                        n_pairs=sb_heads // 2)
    nb = s // MOBA_BLOCK
    kaug, kmean = _moba_prep(h0, k_col=c(wb), n_pairs=n_pairs)
    kmean = jnp.pad(kmean.reshape(n_pairs, nb, LANES), ((0, 0), (0, LANES - nb), (0, 0)))
    kmean = kmean.reshape(n_pairs * LANES, LANES)
    o_b = _moba_attention(h0, kaug, vt, kmean, q_col=0, n_pairs=n_pairs)
    return o_a, o_b


def _mixer_c(x, w_in, cos_t, sin_t, *, n_heads):
    width = n_heads * HEAD_DIM
    iw = IDX_HEADS * HEAD_DIM
    d = w_in.shape[0]
    q, k, v, qi, ki, wi = jnp.split(w_in, np.cumsum([width, width, width, iw, HEAD_DIM]).tolist(), axis=1)
    tn = cos_t.shape[1]
    ki_pad = jnp.zeros((d, tn - 2 * HEAD_DIM), w_in.dtype)
    w = jnp.concatenate([q, k, qi, ki, ki, ki_pad], axis=1).astype(BF16)
    att = HEAD_DIM ** -0.5
    scale = jnp.concatenate([jnp.full((width,), att * LOG2E), jnp.ones((width,)), jnp.full((iw,), att),
                             jnp.ones((tn,))]).astype(F32)[None]
    h1 = _proj(x, w, scale, cos_t, sin_t, ((0, (2 * width + iw) // tn + 1),), tn=tn)
    wi_pad = jnp.zeros((d, LANES - IDX_HEADS), w_in.dtype)
    wv = jnp.concatenate([v, wi, wi_pad], axis=1).astype(BF16)
    vscale = jnp.concatenate([jnp.ones((width,)), jnp.full((LANES,), IDX_HEADS ** -0.5)]).astype(F32)[None]
    vt = _proj_t(x, wv, vscale, groups=3)
    return _dsa_attention(h1, vt, n_heads=n_heads)


def _split_kernel(w_ref, perm_ref, g_ref, u_ref):
    half = g_ref.shape[2]
    both = _dot(w_ref[0].astype(BF16), perm_ref[...])
    g_ref[0] = both[:, :half].astype(BF16)
    u_ref[0] = both[:, half:].astype(BF16)


def _split_gate_up(w1, *, tc=512):
    n_e, d, f2 = w1.shape
    half = tc // 2
    src = jnp.concatenate([2 * jnp.arange(half), 2 * jnp.arange(half) + 1])
    perm = (jnp.arange(tc)[:, None] == src[None, :]).astype(BF16)
    out = jax.ShapeDtypeStruct((n_e, d, f2 // 2), BF16)
    return pl.pallas_call(
        _split_kernel,
        grid=(n_e, f2 // tc),
        in_specs=[pl.BlockSpec((1, d, tc), lambda e, c: (e, 0, c)),
                  pl.BlockSpec((tc, tc), lambda e, c: (0, 0))],
        out_specs=[pl.BlockSpec((1, d, half), lambda e, c: (e, 0, c)),
                   pl.BlockSpec((1, d, half), lambda e, c: (e, 0, c))],
        out_shape=[out, out],
        compiler_params=_cparams(("arbitrary", "arbitrary")),
        name="split_gate_up",
    )(w1, perm)


def _moe_weights(w_r, b_r, w1, b1, w2, b2):
    wr_t = jnp.pad(w_r.T, ((0, LANES - N_EXPERTS), (0, 0)))
    br = jnp.pad(b_r, (0, LANES - N_EXPERTS))[:, None]
    w1g, w1u = _split_gate_up(w1)
    return (wr_t, br, w1g, w1u, b1[:, None, 0::2], b1[:, None, 1::2], w2.astype(BF16), b2[:, None, :])


def kernel(x, p, ab_w_in, ab_w_out, c_w_in, c_w_out, ln_g, ln_b, router_w, router_b,
           moe_w1, moe_b1, moe_w2, moe_b2, ple_w_proj, ple_w_gate):
    b, s, d = x.shape
    assert b == 1
    depth = p.shape[0]
    alpha = float((2 * depth) ** 0.25)
    cos_t, sin_t = _rope_tables(s, 2 * LANES)
    sb_heads = moba_heads = ab_w_out.shape[1] // (2 * HEAD_DIM)
    dsa_heads = c_w_out.shape[1] // HEAD_DIM
    xs = x[0]
    for i in range(depth):
        j = i // 2
        g = ln_g[i][:, None, :]
        bb = ln_b[i][:, None, :]
        if i % 2 == 0:
            o_a, o_b = _mixer_ab(xs, ab_w_in[j], cos_t, sin_t, sb_heads=sb_heads, moba_heads=moba_heads)
            w_out = ab_w_out[j].astype(BF16)
            wa = sb_heads * HEAD_DIM
            xs = _outproj_ln([o_a, o_b], [w_out[:wa], w_out[wa:]], xs, g[0], bb[0], alpha)
        else:
            o_c = _mixer_c(xs, c_w_in[j], cos_t, sin_t, n_heads=dsa_heads)
            xs = _outproj_ln([o_c], [c_w_out[j].astype(BF16)], xs, g[0], bb[0], alpha)
        mw = _moe_weights(router_w[i], router_b[i], moe_w1[i], moe_b1[i], moe_w2[i], moe_b2[i])
        xs = _moe_ln(xs, *mw, g[1], bb[1], alpha)
        xs = _ple(xs, p[i, 0], ple_w_gate[i].astype(BF16), ple_w_proj[i].astype(BF16))
    return xs[None]
```

```python
import functools

import jax
import jax.numpy as jnp
import numpy as np
from jax import lax
from jax.experimental import pallas as pl
from jax.experimental.pallas import tpu as pltpu

F32 = jnp.float32
BF16 = jnp.bfloat16
I32 = jnp.int32

HEAD_DIM = 64
LANES = 128
MOBA_BLOCK = 256
MOBA_TOPK = 3
DSA_TOPK = 256
IDX_HEADS = 8
N_EXPERTS = 32
TOP_K = 4
SWIGLU_LIMIT = 7.0
SWIGLU_ALPHA = 1.702
ROPE_THETA = 10000.0
LN_EPS = 1e-5
NEG = -1e30
EXP_FLUSH = -88.0
INT_MIN = -(2 ** 31)
LOG2E = 1.4426950408889634
ONES_ROWS = 16
VMEM_LIMIT = 56 * 1024 * 1024


def _cparams(sem, flags=None):
    return pltpu.CompilerParams(dimension_semantics=sem, vmem_limit_bytes=VMEM_LIMIT, flags=flags)


def _dot(a, b):
    return jnp.dot(a, b, preferred_element_type=F32)


def _dot_nt(a, b):
    return lax.dot_general(a, b, (((1,), (1,)), ((), ())), preferred_element_type=F32)


def _dot_tn(a, b):
    return lax.dot_general(a, b, (((0,), (0,)), ((), ())), preferred_element_type=F32)


def _iota(shape, dim):
    return lax.broadcasted_iota(I32, shape, dim)


def _proj_kernel(x_ref, w_ref, scale_ref, cos_ref, sin_ref, o_ref, xb_ref, *, rope_ranges, tn):
    j = pl.program_id(1)

    @pl.when(j == 0)
    def _():
        xb_ref[...] = x_ref[...].astype(BF16)

    h = _dot(xb_ref[...], w_ref[...]) * scale_ref[...]
    is_rope = functools.reduce(jnp.logical_or, [(j >= a) & (j < b) for a, b in rope_ranges])

    @pl.when(is_rope)
    def _():
        lane = _iota(h.shape, 1)
        first_half = (lane % HEAD_DIM) < (HEAD_DIM // 2)
        partner = jnp.where(first_half, pltpu.roll(h, tn - HEAD_DIM // 2, 1),
                            pltpu.roll(h, HEAD_DIM // 2, 1))
        o_ref[...] = (h * cos_ref[...] + partner * sin_ref[...]).astype(o_ref.dtype)

    @pl.when(jnp.logical_not(is_rope))
    def _():
        o_ref[...] = h.astype(o_ref.dtype)


def _proj(x, w, scale, cos_t, sin_t, rope_ranges, *, tm=512, tn=256):
    s, d = x.shape
    n = w.shape[1]
    return pl.pallas_call(
        functools.partial(_proj_kernel, rope_ranges=rope_ranges, tn=tn),
        grid=(s // tm, n // tn),
        in_specs=[
            pl.BlockSpec((tm, d), lambda i, j: (i, 0)),
            pl.BlockSpec((d, tn), lambda i, j: (0, j)),
            pl.BlockSpec((1, tn), lambda i, j: (0, j)),
            pl.BlockSpec((tm, tn), lambda i, j: (i, 0)),
            pl.BlockSpec((tm, tn), lambda i, j: (i, 0)),
        ],
        out_specs=pl.BlockSpec((tm, tn), lambda i, j: (i, j)),
        out_shape=jax.ShapeDtypeStruct((s, n), BF16),
        scratch_shapes=[pltpu.VMEM((tm, d), BF16)],
        compiler_params=_cparams(("arbitrary", "arbitrary")),
        name="in_proj",
    )(x, w, scale, cos_t, sin_t)


def _proj_t_kernel(x_ref, w_ref, scale_ref, o_ref, *, groups):
    h = _dot(x_ref[...].astype(BF16), w_ref[...]) * scale_ref[...]
    for g in range(groups):
        o_ref[g, 0] = h[:, g * LANES:(g + 1) * LANES].T.astype(o_ref.dtype)


def _proj_t(x, w, scale, *, groups, tm=256):
    s, d = x.shape
    n = w.shape[1]
    tc = groups * LANES
    return pl.pallas_call(
        functools.partial(_proj_t_kernel, groups=groups),
        grid=(s // tm, n // tc),
        in_specs=[
            pl.BlockSpec((tm, d), lambda i, j: (i, 0)),
            pl.BlockSpec((d, tc), lambda i, j: (0, j)),
            pl.BlockSpec((1, tc), lambda i, j: (0, j)),
        ],
        out_specs=pl.BlockSpec((groups, 1, LANES, tm), lambda i, j: (j, i, 0, 0)),
        out_shape=jax.ShapeDtypeStruct((n // LANES, s // tm, LANES, tm), BF16),
        compiler_params=_cparams(("arbitrary", "arbitrary")),
        name="proj_t",
    )(x, w, scale)


def _rope_tables(s, width):
    half = HEAD_DIM // 2
    inv = ROPE_THETA ** (-jnp.arange(half, dtype=F32) / half)
    ang = jnp.arange(s, dtype=F32)[:, None] * inv[None, :]
    cos = jnp.cos(ang)
    sin = jnp.sin(ang)
    reps = width // HEAD_DIM
    cos_t = jnp.tile(jnp.concatenate([cos, cos], axis=1), (1, reps))
    sin_t = jnp.tile(jnp.concatenate([-sin, sin], axis=1), (1, reps))
    return cos_t, sin_t


def _sb_kernel(q_ref, k_ref, v_ref, o_ref, acc_ref, run_ref, *, t):
    i = pl.program_id(1)
    q = q_ref[...]
    lane = _iota((t, LANES), 1)
    lo = lane < HEAD_DIM
    zero = jnp.zeros_like(q)
    qh = (jnp.where(lo, q, zero), jnp.where(lo, zero, q))
    row = _iota((t, t), 0)
    col = _iota((t, t), 1)
    incl = (row >= col).astype(BF16)
    ones = jnp.ones((t, LANES), BF16)
    strict = col < row

    acc_ref[...] = jnp.zeros_like(acc_ref)
    run_ref[...] = jnp.zeros_like(run_ref)

    def tile(kt, diagonal):
        start = pl.multiple_of(kt * t, t)
        kblk = k_ref[pl.ds(start, t), :]
        vblk = v_ref[pl.ds(start, t), :]
        for h in range(2):
            z = _dot_nt(qh[h], kblk)
            log_stay = -(jnp.maximum(z, 0.0) + jnp.log1p(jnp.exp(-jnp.abs(z))))
            if diagonal:
                log_stay = jnp.where(strict, log_stay, 0.0)
            ls16 = log_stay.astype(BF16)
            cum = _dot(ls16, incl)
            a = jnp.exp(z + cum + run_ref[h])
            if diagonal:
                a = jnp.where(strict, a, 0.0)
            acc_ref[h] += _dot(a.astype(BF16), vblk)
            run_ref[h] += _dot(ls16, ones)

    tile(i, True)

    def cond(c):
        kt, live = c
        return jnp.logical_and(kt >= 0, live)

    def body(c):
        kt, _ = c
        tile(kt, False)
        live = jnp.max(jnp.maximum(run_ref[0], run_ref[1])) > EXP_FLUSH
        return kt - 1, live

    lax.while_loop(cond, body, (i - 1, jnp.bool_(True)))
    o_ref[...] = jnp.where(lo, acc_ref[0], acc_ref[1]).astype(o_ref.dtype)


def _sb_attention(h0, *, q_col, k_col, v_col, n_pairs, t=128):
    s = h0.shape[0]
    return pl.pallas_call(
        functools.partial(_sb_kernel, t=t),
        grid=(n_pairs, s // t),
        in_specs=[
            pl.BlockSpec((t, LANES), lambda p, i: (i, q_col + p)),
            pl.BlockSpec((s, LANES), lambda p, i: (0, k_col + p)),
            pl.BlockSpec((s, LANES), lambda p, i: (0, v_col + p)),
        ],
        out_specs=pl.BlockSpec((t, LANES), lambda p, i: (i, p)),
        out_shape=jax.ShapeDtypeStruct((s, n_pairs * LANES), BF16),
        scratch_shapes=[pltpu.VMEM((2, t, LANES), F32), pltpu.VMEM((2, t, LANES), F32)],
        compiler_params=_cparams(("arbitrary", "arbitrary")),
        name="stick_breaking",
    )(h0, h0, h0)


def _moba_prep_kernel(k_ref, kaug_ref, kmean_ref):
    b = pl.program_id(1)
    k = k_ref[...]
    kmean_ref[...] = jnp.mean(k.astype(F32), axis=0, keepdims=True)[None]
    onehot = (_iota(k.shape, 1) == b).astype(BF16)
    kaug_ref[...] = jnp.concatenate([k, onehot], axis=1)


def _moba_prep(h0, *, k_col, n_pairs):
    s = h0.shape[0]
    nb = s // MOBA_BLOCK
    return pl.pallas_call(
        _moba_prep_kernel,
        grid=(n_pairs, nb),
        in_specs=[pl.BlockSpec((MOBA_BLOCK, LANES), lambda p, b: (b, k_col + p))],
        out_specs=[
            pl.BlockSpec((MOBA_BLOCK, 2 * LANES), lambda p, b: (b, p)),
            pl.BlockSpec((1, 1, LANES), lambda p, b: (p * nb + b, 0, 0)),
        ],
        out_shape=[
            jax.ShapeDtypeStruct((s, n_pairs * 2 * LANES), BF16),
            jax.ShapeDtypeStruct((n_pairs * nb, 1, LANES), F32),
        ],
        compiler_params=_cparams(("arbitrary", "arbitrary")),
        name="moba_prep",
    )(h0)


def _first_argmax(x, lane):
    m = jnp.max(x, axis=1, keepdims=True)
    idx = jnp.min(jnp.where(x == m, lane, float(LANES)), axis=1, keepdims=True)
    return m, idx


def _moba_kernel(q_ref, kaug_ref, vt_ref, kmean_ref, o_ref, m_ref, acc_ref, st_ref, *, nb, g):
    t = MOBA_BLOCK
    i = pl.program_id(1)
    q = q_ref[...]
    lane = _iota((t, LANES), 1)
    lo = lane < HEAD_DIM
    zero = jnp.zeros_like(q)
    qh = (jnp.where(lo, q, zero), jnp.where(lo, zero, q))
    kmean = kmean_ref[...]
    km_hi = kmean.astype(BF16)
    km_lo = (kmean - km_hi.astype(F32)).astype(BF16)
    q_aug = []
    for h in range(2):
        gate = _dot_nt(qh[h], km_hi) + _dot_nt(qh[h], km_lo)
        gate = jnp.where(lane < i, gate, -jnp.inf)
        chosen = lane == i
        lane_f = lane.astype(F32)
        for _ in range(MOBA_TOPK):
            m, idx = _first_argmax(gate, lane_f)
            pick = (lane_f == idx) & (m > -jnp.inf)
            chosen = chosen | pick
            gate = jnp.where(lane_f == idx, -jnp.inf, gate)
        bias = jnp.where(chosen | (lane >= nb), 0.0, NEG)
        q_aug.append(jnp.concatenate([qh[h].astype(F32), bias], axis=1).T.astype(BF16))
    q_both = jnp.concatenate(q_aug, axis=1)

    tk = g * t
    last = lax.shift_right_logical(i, int(np.log2(g)))
    ones = jnp.ones((ONES_ROWS, tk), BF16)

    def scores(u):
        start = pl.multiple_of(u * tk, tk)
        return _dot(kaug_ref[pl.ds(start, tk), :], q_both)

    m_ref[...] = jnp.full(m_ref.shape, NEG, F32)
    acc_ref[...] = jnp.zeros_like(acc_ref)
    key_pos = _iota((tk, 2 * t), 0) + (last * g - i) * t
    query_pos = _iota((tk, 2 * t), 1) & (t - 1)
    st_ref[...] = jnp.where(key_pos <= query_pos, scores(last), NEG)

    def body(n, carry):
        cur = jnp.where(n == 0, last, n - 1)
        nxt = jnp.minimum(n, jnp.maximum(last - 1, 0))
        st_next = scores(nxt)
        vt = jnp.concatenate([vt_ref[0, cur * g + b] for b in range(g)], axis=1)
        vt = jnp.concatenate([vt, ones], axis=0)
        st = st_ref[...]
        m_old = m_ref[...]
        m_new = jnp.maximum(m_old, jnp.max(st, axis=0, keepdims=True))
        alpha = jnp.exp2(m_old - m_new)
        pt = jnp.exp2(st - m_new).astype(BF16)
        acc_ref[...] = alpha * acc_ref[...] + _dot(vt, pt)
        m_ref[...] = m_new
        st_ref[...] = st_next
        return carry

    lax.fori_loop(0, last + 1, body, 0)
    acc = acc_ref[...]
    outs = [acc[:LANES, h * t:(h + 1) * t] / acc[LANES:LANES + 1, h * t:(h + 1) * t] for h in range(2)]
    first = _iota((LANES, t), 0) < HEAD_DIM
    o_ref[...] = jnp.where(first, outs[0], outs[1]).T.astype(o_ref.dtype)


def _moba_attention(h0, kaug, vt, kmean, *, q_col, n_pairs, g=4):
    s = h0.shape[0]
    t = MOBA_BLOCK
    nb = s // t
    g = min(g, nb)
    assert nb <= LANES and nb % g == 0
    return pl.pallas_call(
        functools.partial(_moba_kernel, nb=nb, g=g),
        grid=(n_pairs, nb),
        in_specs=[
            pl.BlockSpec((t, LANES), lambda p, i: (i, q_col + p)),
            pl.BlockSpec((s, 2 * LANES), lambda p, i: (0, p)),
            pl.BlockSpec((1, nb, LANES, t), lambda p, i: (p, 0, 0, 0)),
            pl.BlockSpec((LANES, LANES), lambda p, i: (p, 0)),
        ],
        out_specs=pl.BlockSpec((t, LANES), lambda p, i: (i, p)),
        out_shape=jax.ShapeDtypeStruct((s, n_pairs * LANES), BF16),
        scratch_shapes=[pltpu.VMEM((1, 2 * t), F32),
                        pltpu.VMEM((LANES + ONES_ROWS, 2 * t), F32),
                        pltpu.VMEM((g * t, 2 * t), F32)],
        compiler_params=_cparams(("arbitrary", "arbitrary")),
        name="moba",
    )(h0, kaug, vt, kmean)


def _dsa_kernel(it_ref, jt_ref, qi_ref, wi_ref, ki_ref, q_ref, k_ref, vt_ref, o_ref,
                key_ref, thr_ref, need_ref, seen_ref, m_ref, acc_ref, qt_ref,
                *, t, g, n_heads, topk):
    step = pl.program_id(0)
    i = it_ref[step]
    j = jt_ref[step]
    lane = _iota((t, LANES), 1)
    lo = lane < HEAD_DIM
    key_le_query = _iota((t, t), 0) <= _iota((t, t), 1)
    n_pairs = n_heads // 2

    def split_t(pair):
        pf = pair.astype(F32)
        return (jnp.where(lo, pf, 0.0).T.astype(BF16), jnp.where(lo, 0.0, pf).T.astype(BF16))

    @pl.when(j == 0)
    def _select():
        qi = qi_ref[...]
        wi = wi_ref[0, 0].astype(F32)
        qit = []
        for p in range(IDX_HEADS // 2):
            qit.extend(split_t(qi[:, p * LANES:(p + 1) * LANES]))
        q = q_ref[...]
        for p in range(n_pairs):
            qt_ref[p] = jnp.concatenate(split_t(q[:, p * LANES:(p + 1) * LANES]), axis=1)

        def score_tile(kt, carry):
            start = pl.multiple_of(kt * t, t)
            kblk = ki_ref[pl.ds(start, t), :]
            sc = jnp.zeros((t, t), F32)
            for h in range(IDX_HEADS):
                sc = sc + wi[h:h + 1, :] * jnp.maximum(_dot(kblk, qit[h]), 0.0)
            bits = pltpu.bitcast(sc, I32)
            key = jnp.where(bits >= 0, bits, bits ^ jnp.int32(0x7FFFFFFF))
            key = jnp.where(jnp.logical_or(kt < i, key_le_query), key, jnp.int32(INT_MIN))
            key_ref[kt] = key
            return carry

        lax.fori_loop(0, i + 1, score_tile, 0)
        for b in range(1, g + 1):
            key_ref[i + b] = jnp.full((t, t), INT_MIN, I32)

        def count(pred):
            def cbody(kt2, c):
                for kt in (2 * kt2, 2 * kt2 + 1):
                    hit = jnp.where(pred(key_ref[kt]), 1.0, 0.0)
                    c = c + hit.reshape(t // 8, 8, t).sum(axis=0)
                return c
            c = lax.fori_loop(0, lax.shift_right_logical(i + 2, 1), cbody, jnp.zeros((8, t), F32))
            return jnp.sum(c, axis=0, keepdims=True)

        thr_u = jnp.zeros((1, t), I32)
        for b in range(31, -1, -1):
            bit = jnp.int32(INT_MIN) if b == 31 else jnp.int32(1 << b)
            cand_u = thr_u | bit
            cand_s = cand_u ^ jnp.int32(INT_MIN)
            cnt = count(lambda kk, c=cand_s: kk >= c)
            thr_u = jnp.where(cnt >= float(topk), cand_u, thr_u)
        thr = thr_u ^ jnp.int32(INT_MIN)
        n_gt = count(lambda kk: kk > thr)
        thr_ref[...] = thr
        need_ref[...] = float(topk) - n_gt
        seen_ref[...] = jnp.zeros_like(seen_ref)
        m_ref[...] = jnp.full(m_ref.shape, NEG, F32)
        acc_ref[...] = jnp.zeros_like(acc_ref)

    thr = thr_ref[...]
    need = need_ref[...]
    earlier = (_iota((t, t), 1) < _iota((t, t), 0)).astype(BF16)
    seen = seen_ref[...]
    biases = []
    for b in range(g):
        blk = j * g + b
        key = key_ref[blk]
        eq = key == thr
        rank = seen + _dot(earlier, jnp.where(eq, 1.0, 0.0).astype(BF16))
        sel = (key > thr) | (eq & (rank < need))
        sel = sel & jnp.logical_or(blk < i, jnp.logical_and(blk == i, key_le_query))
        seen = seen + jnp.sum(jnp.where(eq, 1.0, 0.0), axis=0, keepdims=True)
        biases.append(jnp.where(sel, 0.0, NEG))
    seen_ref[...] = seen
    bias = jnp.concatenate(biases, axis=0)
    bias = jnp.concatenate([bias, bias], axis=1)

    kb = k_ref[...]
    ones = jnp.ones((ONES_ROWS, g * t), BF16)

    def scores(p):
        return _dot(kb[:, p * LANES:(p + 1) * LANES], qt_ref[p]) + bias

    st_next = scores(0)
    for p in range(n_pairs):
        st = st_next
        if p + 1 < n_pairs:
            st_next = scores(p + 1)
        vt = jnp.concatenate([vt_ref[p, b] for b in range(g)], axis=1)
        vt = jnp.concatenate([vt, ones], axis=0)
        m_old = m_ref[p]
        m_new = jnp.maximum(m_old, jnp.max(st, axis=0, keepdims=True))
        alpha = jnp.exp2(m_old - m_new)
        pt = jnp.exp2(st - m_new).astype(BF16)
        acc_ref[p] = alpha * acc_ref[p] + _dot(vt, pt)
        m_ref[p] = m_new

    @pl.when(j == lax.shift_right_logical(i, int(np.log2(g))))
    def _finish():
        first = _iota((LANES, t), 0) < HEAD_DIM
        for p in range(n_pairs):
            acc = acc_ref[p]
            a0, a1 = acc[:, :t], acc[:, t:]
            out_t = jnp.where(first, a0[:LANES] / a0[LANES:LANES + 1], a1[:LANES] / a1[LANES:LANES + 1])
            o_ref[:, p * LANES:(p + 1) * LANES] = out_t.T.astype(o_ref.dtype)


def _dsa_attention(h1, vt, *, n_heads, t=256, g=4):
    s = h1.shape[0]
    width = n_heads * HEAD_DIM
    nt = s // t
    g = min(g, nt)
    assert nt % g == 0
    topk = min(DSA_TOPK, s // 4)
    iw = IDX_HEADS * HEAD_DIM
    qi_blk = 2 * width // iw
    ki_blk = (2 * width + iw) // LANES
    wi_grp = width // LANES
    tri = [(i, j) for i in range(nt) for j in range(i // g + 1)]
    it = jnp.asarray([a for a, _ in tri], I32)
    jt = jnp.asarray([b for _, b in tri], I32)
    grid_spec = pltpu.PrefetchScalarGridSpec(
        num_scalar_prefetch=2,
        grid=(len(tri),),
        in_specs=[
            pl.BlockSpec((t, iw), lambda n, it, jt: (it[n], qi_blk)),
            pl.BlockSpec((1, 1, ONES_ROWS, t), lambda n, it, jt: (wi_grp, it[n], 0, 0)),
            pl.BlockSpec((s, LANES), lambda n, it, jt: (0, ki_blk)),
            pl.BlockSpec((t, width), lambda n, it, jt: (it[n], 0)),
            pl.BlockSpec((g * t, width), lambda n, it, jt: (jt[n], 1)),
            pl.BlockSpec((n_heads // 2, g, LANES, t), lambda n, it, jt: (0, jt[n], 0, 0)),
        ],
        out_specs=pl.BlockSpec((t, width), lambda n, it, jt: (it[n], 0)),
        scratch_shapes=[
            pltpu.VMEM((nt + g, t, t), I32),
            pltpu.VMEM((1, t), I32),
            pltpu.VMEM((1, t), F32),
            pltpu.VMEM((1, t), F32),
            pltpu.VMEM((n_heads // 2, 1, 2 * t), F32),
            pltpu.VMEM((n_heads // 2, LANES + ONES_ROWS, 2 * t), F32),
            pltpu.VMEM((n_heads // 2, LANES, 2 * t), BF16),
        ],
    )
    return pl.pallas_call(
        functools.partial(_dsa_kernel, t=t, g=g, n_heads=n_heads, topk=topk),
        grid_spec=grid_spec,
        out_shape=jax.ShapeDtypeStruct((s, width), BF16),
        compiler_params=_cparams(("arbitrary",)),
        name="dsa",
    )(it, jt, h1, vt, h1, h1, h1, vt)


def _layer_norm(y, g, b):
    mu = jnp.mean(y, axis=-1, keepdims=True)
    yc = y - mu
    var = jnp.mean(yc * yc, axis=-1, keepdims=True)
    return yc * lax.rsqrt(var + LN_EPS) * g + b


def _outproj_kernel(*refs, n_in, alpha):
    o_refs = refs[:n_in]
    w_refs = refs[n_in:2 * n_in]
    x_ref, g_ref, b_ref, out_ref = refs[2 * n_in:]
    y = alpha * x_ref[...]
    for o_ref, w_ref in zip(o_refs, w_refs):
        y = y + _dot(o_ref[...], w_ref[...])
    out_ref[...] = _layer_norm(y, g_ref[...], b_ref[...])


def _outproj_ln(os_, ws, x, g, b, alpha, *, tm=256):
    s, d = x.shape
    n_in = len(os_)
    in_specs = ([pl.BlockSpec((tm, o.shape[1]), lambda i: (i, 0)) for o in os_]
                + [pl.BlockSpec(w.shape, lambda i: (0, 0)) for w in ws]
                + [pl.BlockSpec((tm, d), lambda i: (i, 0)),
                   pl.BlockSpec((1, d), lambda i: (0, 0)),
                   pl.BlockSpec((1, d), lambda i: (0, 0))])
    return pl.pallas_call(
        functools.partial(_outproj_kernel, n_in=n_in, alpha=alpha),
        grid=(s // tm,),
        in_specs=in_specs,
        out_specs=pl.BlockSpec((tm, d), lambda i: (i, 0)),
        out_shape=jax.ShapeDtypeStruct((s, d), F32),
        compiler_params=_cparams(("arbitrary",)),
        name="out_proj_ln",
    )(*os_, *ws, x, g, b)


def _moe_kernel(x_ref, wr_ref, br_ref, w1g_ref, w1u_ref, b1g_ref, b1u_ref, w2_ref, b2_ref,
                g_ref, b_ref, out_ref,
                xb_ref, rank_ref, rank_t_ref, gate_t_ref, cnt_ref, acc_ref,
                *, tt, ch, alpha):
    e = pl.program_id(1)
    n_e = pl.num_programs(1)

    @pl.when(e == 0)
    def _route():
        x = x_ref[...]
        xb_ref[...] = x.astype(BF16)
        logits = lax.dot_general(wr_ref[...], x, (((1,), (1,)), ((), ())),
                                 precision=lax.Precision.HIGHEST,
                                 preferred_element_type=F32) + br_ref[...]
        row = _iota(logits.shape, 0).astype(F32)
        work = jnp.where(row < N_EXPERTS, logits, -jnp.inf)
        chosen = row < 0
        top = None
        for r in range(TOP_K):
            m = jnp.max(work, axis=0, keepdims=True)
            idx = jnp.min(jnp.where(work == m, row, float(LANES)), axis=0, keepdims=True)
            pick = row == idx
            chosen = chosen | pick
            work = jnp.where(pick, -jnp.inf, work)
            if r == 0:
                top = m
        ex = jnp.where(chosen, jnp.exp(logits - top), 0.0)
        gate = ex / jnp.sum(ex, axis=0, keepdims=True)
        sel16 = jnp.where(chosen, 1.0, 0.0).astype(BF16)
        before = (_iota((tt, tt), 0) < _iota((tt, tt), 1)).astype(BF16)
        pos = _dot(sel16, before)
        rank = jnp.where(chosen, pos, -1.0)
        rank_ref[...] = rank
        rank_t_ref[...] = rank.T
        gate_t_ref[...] = gate.T
        cnt = jnp.sum(jnp.where(chosen, 1.0, 0.0), axis=1, keepdims=True)
        cnt_ref[...] = jnp.broadcast_to(cnt, cnt_ref.shape)
        acc_ref[...] = jnp.zeros_like(acc_ref)

    lane = _iota((tt, LANES), 1)
    on_e = lane == e
    rank_col = jnp.sum(jnp.where(on_e, rank_t_ref[...], 0.0), axis=1, keepdims=True)
    gate_col = jnp.sum(jnp.where(on_e, gate_t_ref[...], 0.0), axis=1, keepdims=True)
    rank_row = rank_ref[pl.ds(e, 1), :]
    n_rows = jnp.max(cnt_ref[pl.ds(e, 1), :]).astype(I32)
    n_chunks = (n_rows + (ch - 1)) // ch
    pad_rows = jnp.zeros((2 * LANES - ch, x_ref.shape[1]), BF16)
    slot = _iota((1, 2 * LANES), 1)
    slot = jnp.where(slot < ch, slot.astype(F32), -1e9)
    w1g = w1g_ref[0]
    w1u = w1u_ref[0]
    w2 = w2_ref[0]
    b1g = b1g_ref[0]
    b1u = b1u_ref[0]
    b2 = b2_ref[0]

    def chunk(c, carry):
        r0 = (c * ch).astype(F32)
        gather = (rank_row == (r0 + _iota((ch, tt), 0).astype(F32))).astype(BF16)
        xg = _dot(gather, xb_ref[...]).astype(BF16)
        hg = jnp.minimum(_dot(xg, w1g) + b1g, SWIGLU_LIMIT)
        hu = jnp.clip(_dot(xg, w1u) + b1u, -SWIGLU_LIMIT, SWIGLU_LIMIT)
        act = (hu + 1.0) * (hg * jax.nn.sigmoid(SWIGLU_ALPHA * hg))
        y = _dot(act.astype(BF16), w2) + b2
        scatter = jnp.where(rank_col == (r0 + slot), gate_col, 0.0).astype(BF16)
        acc_ref[...] += _dot(scatter, jnp.concatenate([y.astype(BF16), pad_rows], axis=0))
        return carry

    lax.fori_loop(0, n_chunks, chunk, 0)

    @pl.when(e == n_e - 1)
    def _finish():
        out_ref[...] = _layer_norm(alpha * x_ref[...] + acc_ref[...], g_ref[...], b_ref[...])


def _moe_ln(x, wr_t, br, w1g, w1u, b1g, b1u, w2, b2, g, b, alpha, *, tt=1024, ch=160):
    t, d = x.shape
    tt = min(tt, t)
    assert ch % 16 == 0 and ch <= 2 * LANES
    f = w2.shape[1]
    return pl.pallas_call(
        functools.partial(_moe_kernel, tt=tt, ch=ch, alpha=alpha),
        grid=(t // tt, N_EXPERTS),
        in_specs=[
            pl.BlockSpec((tt, d), lambda i, e: (i, 0)),
            pl.BlockSpec((LANES, d), lambda i, e: (0, 0)),
            pl.BlockSpec((LANES, 1), lambda i, e: (0, 0)),
            pl.BlockSpec((1, d, f), lambda i, e: (e, 0, 0)),
            pl.BlockSpec((1, d, f), lambda i, e: (e, 0, 0)),
            pl.BlockSpec((1, 1, f), lambda i, e: (e, 0, 0)),
            pl.BlockSpec((1, 1, f), lambda i, e: (e, 0, 0)),
            pl.BlockSpec((1, f, d), lambda i, e: (e, 0, 0)),
            pl.BlockSpec((1, 1, d), lambda i, e: (e, 0, 0)),
            pl.BlockSpec((1, d), lambda i, e: (0, 0)),
            pl.BlockSpec((1, d), lambda i, e: (0, 0)),
        ],
        out_specs=pl.BlockSpec((tt, d), lambda i, e: (i, 0)),
        out_shape=jax.ShapeDtypeStruct((t, d), F32),
        scratch_shapes=[
            pltpu.VMEM((tt, d), BF16),
            pltpu.VMEM((LANES, tt), F32),
            pltpu.VMEM((tt, LANES), F32),
            pltpu.VMEM((tt, LANES), F32),
            pltpu.VMEM((LANES, LANES), F32),
            pltpu.VMEM((tt, d), F32),
        ],
        compiler_params=_cparams(("arbitrary", "arbitrary")),
        name="moe_ln",
    )(x, wr_t, br, w1g, w1u, b1g, b1u, w2, b2, g, b)


def _ple_kernel(x_ref, p_ref, wg_ref, wp_ref, o_ref):
    x = x_ref[...]
    gate = jax.nn.sigmoid(_dot(x.astype(BF16), wg_ref[...]))
    emb = _dot(p_ref[...].astype(BF16), wp_ref[...])
    o_ref[...] = x + gate * emb


def _ple(x, p, wg, wp, *, tm=512):
    s, d = x.shape
    dp = p.shape[1]
    return pl.pallas_call(
        _ple_kernel,
        grid=(s // tm,),
        in_specs=[
            pl.BlockSpec((tm, d), lambda i: (i, 0)),
            pl.BlockSpec((tm, dp), lambda i: (i, 0)),
            pl.BlockSpec((d, d), lambda i: (0, 0)),
            pl.BlockSpec((dp, d), lambda i: (0, 0)),
        ],
        out_specs=pl.BlockSpec((tm, d), lambda i: (i, 0)),
        out_shape=jax.ShapeDtypeStruct((s, d), F32),
        compiler_params=_cparams(("arbitrary",)),
        name="ple",
    )(x, p, wg, wp)


def _mixer_ab(x, w_in, cos_t, sin_t, *, sb_heads, moba_heads):
    s = x.shape[0]
    wa = sb_heads * HEAD_DIM
    wb = moba_heads * HEAD_DIM
    qa, ka, va, qb, kb, vb = jnp.split(w_in, np.cumsum([wa, wa, wa, wb, wb]).tolist(), axis=1)
    w = jnp.concatenate([qb, kb, qa, ka, va], axis=1).astype(BF16)
    att = HEAD_DIM ** -0.5
    scale = jnp.concatenate([jnp.full((wb,), att * LOG2E), jnp.ones((wb,)), jnp.full((wa,), att),
                             jnp.ones((2 * wa,))]).astype(F32)[None]
    tn = cos_t.shape[1]
    h0 = _proj(x, w, scale, cos_t, sin_t, ((0, 2 * wb // tn),), tn=tn)
    n_pairs = moba_heads // 2
    vt = _proj_t(x, vb.astype(BF16), jnp.ones((1, wb), F32), groups=n_pairs)
    c = lambda off: off // LANES
    o_a = _sb_attention(h0, q_col=c(2 * wb), k_col=c(2 * wb + wa), v_col=c(2 * wb + 2 * wa),
                        n_pairs=sb_heads // 2)
    nb = s // MOBA_BLOCK
    kaug, kmean = _moba_prep(h0, k_col=c(wb), n_pairs=n_pairs)
    kmean = jnp.pad(kmean.reshape(n_pairs, nb, LANES), ((0, 0), (0, LANES - nb), (0, 0)))
    kmean = kmean.reshape(n_pairs * LANES, LANES)
    o_b = _moba_attention(h0, kaug, vt, kmean, q_col=0, n_pairs=n_pairs)
    return o_a, o_b


def _mixer_c(x, w_in, cos_t, sin_t, *, n_heads):
    width = n_heads * HEAD_DIM
    iw = IDX_HEADS * HEAD_DIM
    d = w_in.shape[0]
    q, k, v, qi, ki, wi = jnp.split(w_in, np.cumsum([width, width, width, iw, HEAD_DIM]).tolist(), axis=1)
    tn = cos_t.shape[1]
    ki_pad = jnp.zeros((d, tn - 2 * HEAD_DIM), w_in.dtype)
    w = jnp.concatenate([q, k, qi, ki, ki, ki_pad], axis=1).astype(BF16)
    att = HEAD_DIM ** -0.5
    scale = jnp.concatenate([jnp.full((width,), att * LOG2E), jnp.ones((width,)), jnp.full((iw,), att),
                             jnp.ones((tn,))]).astype(F32)[None]
    h1 = _proj(x, w, scale, cos_t, sin_t, ((0, (2 * width + iw) // tn + 1),), tn=tn)
    wi_pad = jnp.zeros((d, LANES - IDX_HEADS), w_in.dtype)
    wv = jnp.concatenate([v, wi, wi_pad], axis=1).astype(BF16)
    vscale = jnp.concatenate([jnp.ones((width,)), jnp.full((LANES,), IDX_HEADS ** -0.5)]).astype(F32)[None]
    vt = _proj_t(x, wv, vscale, groups=3)
    return _dsa_attention(h1, vt, n_heads=n_heads)


def _split_kernel(w_ref, perm_ref, g_ref, u_ref):
    half = g_ref.shape[2]
    both = _dot(w_ref[0].astype(BF16), perm_ref[...])
    g_ref[0] = both[:, :half].astype(BF16)
    u_ref[0] = both[:, half:].astype(BF16)


def _split_gate_up(w1, *, tc=512):
    n_e, d, f2 = w1.shape
    half = tc // 2
    src = jnp.concatenate([2 * jnp.arange(half), 2 * jnp.arange(half) + 1])
    perm = (jnp.arange(tc)[:, None] == src[None, :]).astype(BF16)
    out = jax.ShapeDtypeStruct((n_e, d, f2 // 2), BF16)
    return pl.pallas_call(
        _split_kernel,
        grid=(n_e, f2 // tc),
        in_specs=[pl.BlockSpec((1, d, tc), lambda e, c: (e, 0, c)),
                  pl.BlockSpec((tc, tc), lambda e, c: (0, 0))],
        out_specs=[pl.BlockSpec((1, d, half), lambda e, c: (e, 0, c)),
                   pl.BlockSpec((1, d, half), lambda e, c: (e, 0, c))],
        out_shape=[out, out],
        compiler_params=_cparams(("arbitrary", "arbitrary")),
        name="split_gate_up",
    )(w1, perm)


def _moe_weights(w_r, b_r, w1, b1, w2, b2):
    wr_t = jnp.pad(w_r.T, ((0, LANES - N_EXPERTS), (0, 0)))
    br = jnp.pad(b_r, (0, LANES - N_EXPERTS))[:, None]
    w1g, w1u = _split_gate_up(w1)
    return (wr_t, br, w1g, w1u, b1[:, None, 0::2], b1[:, None, 1::2], w2.astype(BF16), b2[:, None, :])


def kernel(x, p, ab_w_in, ab_w_out, c_w_in, c_w_out, ln_g, ln_b, router_w, router_b,
           moe_w1, moe_b1, moe_w2, moe_b2, ple_w_proj, ple_w_gate):
    b, s, d = x.shape
    assert b == 1
    depth = p.shape[0]
    alpha = float((2 * depth) ** 0.25)
    cos_t, sin_t = _rope_tables(s, 2 * LANES)
    sb_heads = moba_heads = ab_w_out.shape[1] // (2 * HEAD_DIM)
    dsa_heads = c_w_out.shape[1] // HEAD_DIM
    xs = x[0]
    for i in range(depth):
        j = i // 2
        g = ln_g[i][:, None, :]
        bb = ln_b[i][:, None, :]
        if i % 2 == 0:
            o_a, o_b = _mixer_ab(xs, ab_w_in[j], cos_t, sin_t, sb_heads=sb_heads, moba_heads=moba_heads)
            w_out = ab_w_out[j].astype(BF16)
            wa = sb_heads * HEAD_DIM
            xs = _outproj_ln([o_a, o_b], [w_out[:wa], w_out[wa:]], xs, g[0], bb[0], alpha)
        else:
            o_c = _mixer_c(xs, c_w_in[j], cos_t, sin_t, n_heads=dsa_heads)
            xs = _outproj_ln([o_c], [c_w_out[j].astype(BF16)], xs, g[0], bb[0], alpha)
        mw = _moe_weights(router_w[i], router_b[i], moe_w1[i], moe_b1[i], moe_w2[i], moe_b2[i])
        xs = _moe_ln(xs, *mw, g[1], bb[1], alpha)
        xs = _ple(xs, p[i, 0], ple_w_gate[i].astype(BF16), ple_w_proj[i].astype(BF16))
    return xs[None]
```

```python
import functools

import jax
import jax.numpy as jnp
import numpy as np
from jax import lax
from jax.experimental import pallas as pl
from jax.experimental.pallas import tpu as pltpu

F32 = jnp.float32
BF16 = jnp.bfloat16
I32 = jnp.int32
I16 = jnp.int16

HEAD_DIM = 64
LANES = 128
MOBA_BLOCK = 256
MOBA_TOPK = 3
DSA_TOPK = 256
IDX_HEADS = 8
N_EXPERTS = 32
TOP_K = 4
SWIGLU_LIMIT = 7.0
SWIGLU_ALPHA = 1.702
ROPE_THETA = 10000.0
LN_EPS = 1e-5
NEG = -1e30
EXP_FLUSH = -88.0
INT_MIN = -(2 ** 31)
I16_MIN = -(2 ** 15)
COUNT_UNROLL = 4
LOG2E = 1.4426950408889634
ONES_ROWS = 16
VMEM_LIMIT = 56 * 1024 * 1024


def _cparams(sem, flags=None):
    return pltpu.CompilerParams(dimension_semantics=sem, vmem_limit_bytes=VMEM_LIMIT, flags=flags)


def _dot(a, b):
    return jnp.dot(a, b, preferred_element_type=F32)


def _dot_nt(a, b):
    return lax.dot_general(a, b, (((1,), (1,)), ((), ())), preferred_element_type=F32)


def _dot_tn(a, b):
    return lax.dot_general(a, b, (((0,), (0,)), ((), ())), preferred_element_type=F32)


def _iota(shape, dim):
    return lax.broadcasted_iota(I32, shape, dim)


def _proj_kernel(x_ref, w_ref, scale_ref, cos_ref, sin_ref, o_ref, xb_ref, *, rope_ranges, tn):
    j = pl.program_id(1)

    @pl.when(j == 0)
    def _():
        xb_ref[...] = x_ref[...].astype(BF16)

    h = _dot(xb_ref[...], w_ref[...]) * scale_ref[...]
    is_rope = functools.reduce(jnp.logical_or, [(j >= a) & (j < b) for a, b in rope_ranges])

    @pl.when(is_rope)
    def _():
        lane = _iota(h.shape, 1)
        first_half = (lane % HEAD_DIM) < (HEAD_DIM // 2)
        partner = jnp.where(first_half, pltpu.roll(h, tn - HEAD_DIM // 2, 1),
                            pltpu.roll(h, HEAD_DIM // 2, 1))
        o_ref[...] = (h * cos_ref[...] + partner * sin_ref[...]).astype(o_ref.dtype)

    @pl.when(jnp.logical_not(is_rope))
    def _():
        o_ref[...] = h.astype(o_ref.dtype)


def _proj(x, w, scale, cos_t, sin_t, rope_ranges, *, tm=512, tn=256):
    s, d = x.shape
    n = w.shape[1]
    return pl.pallas_call(
        functools.partial(_proj_kernel, rope_ranges=rope_ranges, tn=tn),
        grid=(s // tm, n // tn),
        in_specs=[
            pl.BlockSpec((tm, d), lambda i, j: (i, 0)),
            pl.BlockSpec((d, tn), lambda i, j: (0, j)),
            pl.BlockSpec((1, tn), lambda i, j: (0, j)),
            pl.BlockSpec((tm, tn), lambda i, j: (i, 0)),
            pl.BlockSpec((tm, tn), lambda i, j: (i, 0)),
        ],
        out_specs=pl.BlockSpec((tm, tn), lambda i, j: (i, j)),
        out_shape=jax.ShapeDtypeStruct((s, n), BF16),
        scratch_shapes=[pltpu.VMEM((tm, d), BF16)],
        compiler_params=_cparams(("arbitrary", "arbitrary")),
        name="in_proj",
    )(x, w, scale, cos_t, sin_t)


def _proj_t_kernel(x_ref, w_ref, scale_ref, o_ref, *, groups):
    h = _dot(x_ref[...].astype(BF16), w_ref[...]) * scale_ref[...]
    for g in range(groups):
        o_ref[g, 0] = h[:, g * LANES:(g + 1) * LANES].T.astype(o_ref.dtype)


def _proj_t(x, w, scale, *, groups, tm=256):
    s, d = x.shape
    n = w.shape[1]
    tc = groups * LANES
    return pl.pallas_call(
        functools.partial(_proj_t_kernel, groups=groups),
        grid=(s // tm, n // tc),
        in_specs=[
            pl.BlockSpec((tm, d), lambda i, j: (i, 0)),
            pl.BlockSpec((d, tc), lambda i, j: (0, j)),
            pl.BlockSpec((1, tc), lambda i, j: (0, j)),
        ],
        out_specs=pl.BlockSpec((groups, 1, LANES, tm), lambda i, j: (j, i, 0, 0)),
        out_shape=jax.ShapeDtypeStruct((n // LANES, s // tm, LANES, tm), BF16),
        compiler_params=_cparams(("arbitrary", "arbitrary")),
        name="proj_t",
    )(x, w, scale)


def _rope_tables(s, width):
    half = HEAD_DIM // 2
    inv = ROPE_THETA ** (-jnp.arange(half, dtype=F32) / half)
    ang = jnp.arange(s, dtype=F32)[:, None] * inv[None, :]
    cos = jnp.cos(ang)
    sin = jnp.sin(ang)
    reps = width // HEAD_DIM
    cos_t = jnp.tile(jnp.concatenate([cos, cos], axis=1), (1, reps))
    sin_t = jnp.tile(jnp.concatenate([-sin, sin], axis=1), (1, reps))
    return cos_t, sin_t


def _sb_kernel(q_ref, k_ref, v_ref, o_ref, acc_ref, run_ref, *, t):
    i = pl.program_id(1)
    q = q_ref[...]
    lane = _iota((t, LANES), 1)
    lo = lane < HEAD_DIM
    zero = jnp.zeros_like(q)
    qh = (jnp.where(lo, q, zero), jnp.where(lo, zero, q))
    row = _iota((t, t), 0)
    col = _iota((t, t), 1)
    incl = (row >= col).astype(BF16)
    ones = jnp.ones((t, LANES), BF16)
    strict = col < row

    acc_ref[...] = jnp.zeros_like(acc_ref)
    run_ref[...] = jnp.zeros_like(run_ref)

    def tile(kt, diagonal):
        start = pl.multiple_of(kt * t, t)
        kblk = k_ref[pl.ds(start, t), :]
        vblk = v_ref[pl.ds(start, t), :]
        for h in range(2):
            z = _dot_nt(qh[h], kblk)
            log_stay = -(jnp.maximum(z, 0.0) + jnp.log1p(jnp.exp(-jnp.abs(z))))
            if diagonal:
                log_stay = jnp.where(strict, log_stay, 0.0)
            ls16 = log_stay.astype(BF16)
            cum = _dot(ls16, incl)
            a = jnp.exp(z + cum + run_ref[h])
            if diagonal:
                a = jnp.where(strict, a, 0.0)
            acc_ref[h] += _dot(a.astype(BF16), vblk)
            run_ref[h] += _dot(ls16, ones)

    tile(i, True)

    def cond(c):
        kt, live = c
        return jnp.logical_and(kt >= 0, live)

    def body(c):
        kt, _ = c
        tile(kt, False)
        live = jnp.max(jnp.maximum(run_ref[0], run_ref[1])) > EXP_FLUSH
        return kt - 1, live

    lax.while_loop(cond, body, (i - 1, jnp.bool_(True)))
    o_ref[...] = jnp.where(lo, acc_ref[0], acc_ref[1]).astype(o_ref.dtype)


def _sb_attention(h0, *, q_col, k_col, v_col, n_pairs, t=128):
    s = h0.shape[0]
    return pl.pallas_call(
        functools.partial(_sb_kernel, t=t),
        grid=(n_pairs, s // t),
        in_specs=[
            pl.BlockSpec((t, LANES), lambda p, i: (i, q_col + p)),
            pl.BlockSpec((s, LANES), lambda p, i: (0, k_col + p)),
            pl.BlockSpec((s, LANES), lambda p, i: (0, v_col + p)),
        ],
        out_specs=pl.BlockSpec((t, LANES), lambda p, i: (i, p)),
        out_shape=jax.ShapeDtypeStruct((s, n_pairs * LANES), BF16),
        scratch_shapes=[pltpu.VMEM((2, t, LANES), F32), pltpu.VMEM((2, t, LANES), F32)],
        compiler_params=_cparams(("arbitrary", "arbitrary")),
        name="stick_breaking",
    )(h0, h0, h0)


def _moba_prep_kernel(k_ref, kaug_ref, kmean_ref):
    b = pl.program_id(1)
    k = k_ref[...]
    kmean_ref[...] = jnp.mean(k.astype(F32), axis=0, keepdims=True)[None]
    onehot = (_iota(k.shape, 1) == b).astype(BF16)
    kaug_ref[...] = jnp.concatenate([k, onehot], axis=1)


def _moba_prep(h0, *, k_col, n_pairs):
    s = h0.shape[0]
    nb = s // MOBA_BLOCK
    return pl.pallas_call(
        _moba_prep_kernel,
        grid=(n_pairs, nb),
        in_specs=[pl.BlockSpec((MOBA_BLOCK, LANES), lambda p, b: (b, k_col + p))],
        out_specs=[
            pl.BlockSpec((MOBA_BLOCK, 2 * LANES), lambda p, b: (b, p)),
            pl.BlockSpec((1, 1, LANES), lambda p, b: (p * nb + b, 0, 0)),
        ],
        out_shape=[
            jax.ShapeDtypeStruct((s, n_pairs * 2 * LANES), BF16),
            jax.ShapeDtypeStruct((n_pairs * nb, 1, LANES), F32),
        ],
        compiler_params=_cparams(("arbitrary", "arbitrary")),
        name="moba_prep",
    )(h0)


def _first_argmax(x, lane):
    m = jnp.max(x, axis=1, keepdims=True)
    idx = jnp.min(jnp.where(x == m, lane, float(LANES)), axis=1, keepdims=True)
    return m, idx


def _moba_kernel(q_ref, kaug_ref, vt_ref, kmean_ref, o_ref, m_ref, acc_ref, st_ref, *, nb, g):
    t = MOBA_BLOCK
    i = pl.program_id(1)
    q = q_ref[...]
    lane = _iota((t, LANES), 1)
    lo = lane < HEAD_DIM
    zero = jnp.zeros_like(q)
    qh = (jnp.where(lo, q, zero), jnp.where(lo, zero, q))
    kmean = kmean_ref[...]
    km_hi = kmean.astype(BF16)
    km_lo = (kmean - km_hi.astype(F32)).astype(BF16)
    q_aug = []
    for h in range(2):
        gate = _dot_nt(qh[h], km_hi) + _dot_nt(qh[h], km_lo)
        gate = jnp.where(lane < i, gate, -jnp.inf)
        chosen = lane == i
        lane_f = lane.astype(F32)
        for _ in range(MOBA_TOPK):
            m, idx = _first_argmax(gate, lane_f)
            pick = (lane_f == idx) & (m > -jnp.inf)
            chosen = chosen | pick
            gate = jnp.where(lane_f == idx, -jnp.inf, gate)
        bias = jnp.where(chosen | (lane >= nb), 0.0, NEG)
        q_aug.append(jnp.concatenate([qh[h].astype(F32), bias], axis=1).T.astype(BF16))
    q_both = jnp.concatenate(q_aug, axis=1)

    tk = g * t
    last = lax.shift_right_logical(i, int(np.log2(g)))
    ones = jnp.ones((ONES_ROWS, tk), BF16)

    def scores(u):
        start = pl.multiple_of(u * tk, tk)
        return _dot(kaug_ref[pl.ds(start, tk), :], q_both)

    m_ref[...] = jnp.full(m_ref.shape, NEG, F32)
    acc_ref[...] = jnp.zeros_like(acc_ref)
    key_pos = _iota((tk, 2 * t), 0) + (last * g - i) * t
    query_pos = _iota((tk, 2 * t), 1) & (t - 1)
    st_ref[...] = jnp.where(key_pos <= query_pos, scores(last), NEG)

    def body(n, carry):
        cur = jnp.where(n == 0, last, n - 1)
        nxt = jnp.minimum(n, jnp.maximum(last - 1, 0))
        st_next = scores(nxt)
        vt = jnp.concatenate([vt_ref[0, cur * g + b] for b in range(g)], axis=1)
        vt = jnp.concatenate([vt, ones], axis=0)
        st = st_ref[...]
        m_old = m_ref[...]
        m_new = jnp.maximum(m_old, jnp.max(st, axis=0, keepdims=True))
        alpha = jnp.exp2(m_old - m_new)
        pt = jnp.exp2(st - m_new).astype(BF16)
        acc_ref[...] = alpha * acc_ref[...] + _dot(vt, pt)
        m_ref[...] = m_new
        st_ref[...] = st_next
        return carry

    lax.fori_loop(0, last + 1, body, 0)
    acc = acc_ref[...]
    outs = [acc[:LANES, h * t:(h + 1) * t] / acc[LANES:LANES + 1, h * t:(h + 1) * t] for h in range(2)]
    first = _iota((LANES, t), 0) < HEAD_DIM
    o_ref[...] = jnp.where(first, outs[0], outs[1]).T.astype(o_ref.dtype)


def _moba_attention(h0, kaug, vt, kmean, *, q_col, n_pairs, g=4):
    s = h0.shape[0]
    t = MOBA_BLOCK
    nb = s // t
    g = min(g, nb)
    assert nb <= LANES and nb % g == 0
    return pl.pallas_call(
        functools.partial(_moba_kernel, nb=nb, g=g),
        grid=(n_pairs, nb),
        in_specs=[
            pl.BlockSpec((t, LANES), lambda p, i: (i, q_col + p)),
            pl.BlockSpec((s, 2 * LANES), lambda p, i: (0, p)),
            pl.BlockSpec((1, nb, LANES, t), lambda p, i: (p, 0, 0, 0)),
            pl.BlockSpec((LANES, LANES), lambda p, i: (p, 0)),
        ],
        out_specs=pl.BlockSpec((t, LANES), lambda p, i: (i, p)),
        out_shape=jax.ShapeDtypeStruct((s, n_pairs * LANES), BF16),
        scratch_shapes=[pltpu.VMEM((1, 2 * t), F32),
                        pltpu.VMEM((LANES + ONES_ROWS, 2 * t), F32),
                        pltpu.VMEM((g * t, 2 * t), F32)],
        compiler_params=_cparams(("arbitrary", "arbitrary")),
        name="moba",
    )(h0, kaug, vt, kmean)


def _dsa_kernel(it_ref, jt_ref, qi_ref, wi_ref, ki_ref, q_ref, k_ref, vt_ref, o_ref,
                key_ref, half_ref, thr_ref, need_ref, seen_ref, m_ref, acc_ref, qt_ref,
                *, t, g, n_heads, topk):
    step = pl.program_id(0)
    i = it_ref[step]
    j = jt_ref[step]
    lane = _iota((t, LANES), 1)
    lo = lane < HEAD_DIM
    key_le_query = _iota((t, t), 0) <= _iota((t, t), 1)
    n_pairs = n_heads // 2

    def split_t(pair):
        pf = pair.astype(F32)
        return (jnp.where(lo, pf, 0.0).T.astype(BF16), jnp.where(lo, 0.0, pf).T.astype(BF16))

    @pl.when(j == 0)
    def _select():
        qi = qi_ref[...]
        wi = wi_ref[0, 0].astype(F32)
        qit = []
        for p in range(IDX_HEADS // 2):
            qit.extend(split_t(qi[:, p * LANES:(p + 1) * LANES]))
        q = q_ref[...]
        for p in range(n_pairs):
            qt_ref[p] = jnp.concatenate(split_t(q[:, p * LANES:(p + 1) * LANES]), axis=1)

        def score_tile(kt, carry):
            start = pl.multiple_of(kt * t, t)
            kblk = ki_ref[pl.ds(start, t), :]
            sc = jnp.zeros((t, t), F32)
            for h in range(IDX_HEADS):
                sc = sc + wi[h:h + 1, :] * jnp.maximum(_dot(kblk, qit[h]), 0.0)
            bits = pltpu.bitcast(sc, I32)
            key = jnp.where(bits >= 0, bits, bits ^ jnp.int32(0x7FFFFFFF))
            key = jnp.where(jnp.logical_or(kt < i, key_le_query), key, jnp.int32(INT_MIN))
            key_ref[kt] = key
            half_ref[kt] = lax.shift_right_arithmetic(key, jnp.int32(16)).astype(I16)
            return carry

        lax.fori_loop(0, i + 1, score_tile, 0)
        for b in range(1, g + 1):
            key_ref[i + b] = jnp.full((t, t), INT_MIN, I32)
        for b in range(1, COUNT_UNROLL):
            half_ref[i + b] = jnp.full((t, t), I16_MIN, I16)
        n_trips = (i + COUNT_UNROLL) // COUNT_UNROLL

        def count(pred):
            def cbody(trip, c):
                for kt in [COUNT_UNROLL * trip + u for u in range(COUNT_UNROLL)]:
                    hit = jnp.where(pred(half_ref[kt]), jnp.int16(1), jnp.int16(0))
                    for r in range(t // 16):
                        c = c + hit[r * 16:(r + 1) * 16]
                return c
            c = lax.fori_loop(0, n_trips, cbody, jnp.zeros((16, t), I16))
            return jnp.sum(c.astype(F32), axis=0, keepdims=True)

        def search(base):
            v = jnp.zeros((1, t), I32)
            for b in range(15, -1, -1):
                cand = v | jnp.int32(1 << b)
                cand16 = (cand + I16_MIN).astype(I16)
                cnt = base + count(lambda hh, c=cand16: hh >= c)
                v = jnp.where(cnt >= float(topk), cand, v)
            return v

        hi = search(0.0) + I16_MIN
        hi16 = hi.astype(I16)
        n_above = count(lambda hh: hh > hi16)

        def narrow(kt, carry):
            low = (key_ref[kt] & jnp.int32(0xFFFF)) + I16_MIN
            half_ref[kt] = jnp.where(half_ref[kt] == hi16, low.astype(I16), jnp.int16(I16_MIN))
            return carry

        lax.fori_loop(0, i + 1, narrow, 0)
        lo = search(n_above)
        lo16 = (lo + I16_MIN).astype(I16)
        n_gt = n_above + count(lambda hh: hh > lo16)
        thr_ref[...] = hi * 65536 + lo
        need_ref[...] = float(topk) - n_gt
        seen_ref[...] = jnp.zeros_like(seen_ref)
        m_ref[...] = jnp.full(m_ref.shape, NEG, F32)
        acc_ref[...] = jnp.zeros_like(acc_ref)

    thr = thr_ref[...]
    need = need_ref[...]
    earlier = (_iota((t, t), 1) < _iota((t, t), 0)).astype(BF16)
    seen = seen_ref[...]
    biases = []
    for b in range(g):
        blk = j * g + b
        key = key_ref[blk]
        eq = key == thr
        rank = seen + _dot(earlier, jnp.where(eq, 1.0, 0.0).astype(BF16))
        sel = (key > thr) | (eq & (rank < need))
        sel = sel & jnp.logical_or(blk < i, jnp.logical_and(blk == i, key_le_query))
        seen = seen + jnp.sum(jnp.where(eq, 1.0, 0.0), axis=0, keepdims=True)
        biases.append(jnp.where(sel, 0.0, NEG))
    seen_ref[...] = seen
    bias = jnp.concatenate(biases, axis=0)
    bias = jnp.concatenate([bias, bias], axis=1)

    kb = k_ref[...]
    ones = jnp.ones((ONES_ROWS, g * t), BF16)

    def scores(p):
        return _dot(kb[:, p * LANES:(p + 1) * LANES], qt_ref[p]) + bias

    st_next = scores(0)
    for p in range(n_pairs):
        st = st_next
        if p + 1 < n_pairs:
            st_next = scores(p + 1)
        vt = jnp.concatenate([vt_ref[p, b] for b in range(g)], axis=1)
        vt = jnp.concatenate([vt, ones], axis=0)
        m_old = m_ref[p]
        m_new = jnp.maximum(m_old, jnp.max(st, axis=0, keepdims=True))
        alpha = jnp.exp2(m_old - m_new)
        pt = jnp.exp2(st - m_new).astype(BF16)
        acc_ref[p] = alpha * acc_ref[p] + _dot(vt, pt)
        m_ref[p] = m_new

    @pl.when(j == lax.shift_right_logical(i, int(np.log2(g))))
    def _finish():
        first = _iota((LANES, t), 0) < HEAD_DIM
        for p in range(n_pairs):
            acc = acc_ref[p]
            a0, a1 = acc[:, :t], acc[:, t:]
            out_t = jnp.where(first, a0[:LANES] / a0[LANES:LANES + 1], a1[:LANES] / a1[LANES:LANES + 1])
            o_ref[:, p * LANES:(p + 1) * LANES] = out_t.T.astype(o_ref.dtype)


def _dsa_attention(h1, vt, *, n_heads, t=256, g=4):
    s = h1.shape[0]
    width = n_heads * HEAD_DIM
    nt = s // t
    g = min(g, nt)
    assert nt % g == 0
    topk = min(DSA_TOPK, s // 4)
    iw = IDX_HEADS * HEAD_DIM
    qi_blk = 2 * width // iw
    ki_blk = (2 * width + iw) // LANES
    wi_grp = width // LANES
    tri = [(i, j) for i in range(nt) for j in range(i // g + 1)]
    it = jnp.asarray([a for a, _ in tri], I32)
    jt = jnp.asarray([b for _, b in tri], I32)
    grid_spec = pltpu.PrefetchScalarGridSpec(
        num_scalar_prefetch=2,
        grid=(len(tri),),
        in_specs=[
            pl.BlockSpec((t, iw), lambda n, it, jt: (it[n], qi_blk)),
            pl.BlockSpec((1, 1, ONES_ROWS, t), lambda n, it, jt: (wi_grp, it[n], 0, 0)),
            pl.BlockSpec((s, LANES), lambda n, it, jt: (0, ki_blk)),
            pl.BlockSpec((t, width), lambda n, it, jt: (it[n], 0)),
            pl.BlockSpec((g * t, width), lambda n, it, jt: (jt[n], 1)),
            pl.BlockSpec((n_heads // 2, g, LANES, t), lambda n, it, jt: (0, jt[n], 0, 0)),
        ],
        out_specs=pl.BlockSpec((t, width), lambda n, it, jt: (it[n], 0)),
        scratch_shapes=[
            pltpu.VMEM((nt + g, t, t), I32),
            pltpu.VMEM((nt + COUNT_UNROLL - 1, t, t), I16),
            pltpu.VMEM((1, t), I32),
            pltpu.VMEM((1, t), F32),
            pltpu.VMEM((1, t), F32),
            pltpu.VMEM((n_heads // 2, 1, 2 * t), F32),
            pltpu.VMEM((n_heads // 2, LANES + ONES_ROWS, 2 * t), F32),
            pltpu.VMEM((n_heads // 2, LANES, 2 * t), BF16),
        ],
    )
    return pl.pallas_call(
        functools.partial(_dsa_kernel, t=t, g=g, n_heads=n_heads, topk=topk),
        grid_spec=grid_spec,
        out_shape=jax.ShapeDtypeStruct((s, width), BF16),
        compiler_params=_cparams(("arbitrary",)),
        name="dsa",
    )(it, jt, h1, vt, h1, h1, h1, vt)


def _layer_norm(y, g, b):
    mu = jnp.mean(y, axis=-1, keepdims=True)
    yc = y - mu
    var = jnp.mean(yc * yc, axis=-1, keepdims=True)
    return yc * lax.rsqrt(var + LN_EPS) * g + b


def _outproj_kernel(*refs, n_in, alpha):
    o_refs = refs[:n_in]
    w_refs = refs[n_in:2 * n_in]
    x_ref, g_ref, b_ref, out_ref = refs[2 * n_in:]
    y = alpha * x_ref[...]
    for o_ref, w_ref in zip(o_refs, w_refs):
        y = y + _dot(o_ref[...], w_ref[...])
    out_ref[...] = _layer_norm(y, g_ref[...], b_ref[...])


def _outproj_ln(os_, ws, x, g, b, alpha, *, tm=256):
    s, d = x.shape
    n_in = len(os_)
    in_specs = ([pl.BlockSpec((tm, o.shape[1]), lambda i: (i, 0)) for o in os_]
                + [pl.BlockSpec(w.shape, lambda i: (0, 0)) for w in ws]
                + [pl.BlockSpec((tm, d), lambda i: (i, 0)),
                   pl.BlockSpec((1, d), lambda i: (0, 0)),
                   pl.BlockSpec((1, d), lambda i: (0, 0))])
    return pl.pallas_call(
        functools.partial(_outproj_kernel, n_in=n_in, alpha=alpha),
        grid=(s // tm,),
        in_specs=in_specs,
        out_specs=pl.BlockSpec((tm, d), lambda i: (i, 0)),
        out_shape=jax.ShapeDtypeStruct((s, d), F32),
        compiler_params=_cparams(("arbitrary",)),
        name="out_proj_ln",
    )(*os_, *ws, x, g, b)


def _moe_kernel(x_ref, wr_ref, br_ref, w1g_ref, w1u_ref, b1g_ref, b1u_ref, w2_ref, b2_ref,
                g_ref, b_ref, out_ref,
                xb_ref, rank_ref, rank_t_ref, gate_t_ref, cnt_ref, acc_ref,
                *, tt, ch, alpha):
    e = pl.program_id(1)
    n_e = pl.num_programs(1)

    @pl.when(e == 0)
    def _route():
        x = x_ref[...]
        xb_ref[...] = x.astype(BF16)
        logits = lax.dot_general(wr_ref[...], x, (((1,), (1,)), ((), ())),
                                 precision=lax.Precision.HIGHEST,
                                 preferred_element_type=F32) + br_ref[...]
        row = _iota(logits.shape, 0).astype(F32)
        work = jnp.where(row < N_EXPERTS, logits, -jnp.inf)
        chosen = row < 0
        top = None
        for r in range(TOP_K):
            m = jnp.max(work, axis=0, keepdims=True)
            idx = jnp.min(jnp.where(work == m, row, float(LANES)), axis=0, keepdims=True)
            pick = row == idx
            chosen = chosen | pick
            work = jnp.where(pick, -jnp.inf, work)
            if r == 0:
                top = m
        ex = jnp.where(chosen, jnp.exp(logits - top), 0.0)
        gate = ex / jnp.sum(ex, axis=0, keepdims=True)
        sel16 = jnp.where(chosen, 1.0, 0.0).astype(BF16)
        before = (_iota((tt, tt), 0) < _iota((tt, tt), 1)).astype(BF16)
        pos = _dot(sel16, before)
        rank = jnp.where(chosen, pos, -1.0)
        rank_ref[...] = rank
        rank_t_ref[...] = rank.T
        gate_t_ref[...] = gate.T
        cnt = jnp.sum(jnp.where(chosen, 1.0, 0.0), axis=1, keepdims=True)
        cnt_ref[...] = jnp.broadcast_to(cnt, cnt_ref.shape)
        acc_ref[...] = jnp.zeros_like(acc_ref)

    lane = _iota((tt, LANES), 1)
    on_e = lane == e
    rank_col = jnp.sum(jnp.where(on_e, rank_t_ref[...], 0.0), axis=1, keepdims=True)
    gate_col = jnp.sum(jnp.where(on_e, gate_t_ref[...], 0.0), axis=1, keepdims=True)
    rank_row = rank_ref[pl.ds(e, 1), :]
    n_rows = jnp.max(cnt_ref[pl.ds(e, 1), :]).astype(I32)
    n_chunks = (n_rows + (ch - 1)) // ch
    pad_rows = jnp.zeros((2 * LANES - ch, x_ref.shape[1]), BF16)
    slot = _iota((1, 2 * LANES), 1)
    slot = jnp.where(slot < ch, slot.astype(F32), -1e9)
    w1g = w1g_ref[0]
    w1u = w1u_ref[0]
    w2 = w2_ref[0]
    b1g = b1g_ref[0]
    b1u = b1u_ref[0]
    b2 = b2_ref[0]

    def chunk(c, carry):
        r0 = (c * ch).astype(F32)
        gather = (rank_row == (r0 + _iota((ch, tt), 0).astype(F32))).astype(BF16)
        xg = _dot(gather, xb_ref[...]).astype(BF16)
        hg = jnp.minimum(_dot(xg, w1g) + b1g, SWIGLU_LIMIT)
        hu = jnp.clip(_dot(xg, w1u) + b1u, -SWIGLU_LIMIT, SWIGLU_LIMIT)
        act = (hu + 1.0) * (hg * jax.nn.sigmoid(SWIGLU_ALPHA * hg))
        y = _dot(act.astype(BF16), w2) + b2
        scatter = jnp.where(rank_col == (r0 + slot), gate_col, 0.0).astype(BF16)
        acc_ref[...] += _dot(scatter, jnp.concatenate([y.astype(BF16), pad_rows], axis=0))
        return carry

    lax.fori_loop(0, n_chunks, chunk, 0)

    @pl.when(e == n_e - 1)
    def _finish():
        out_ref[...] = _layer_norm(alpha * x_ref[...] + acc_ref[...], g_ref[...], b_ref[...])


def _moe_ln(x, wr_t, br, w1g, w1u, b1g, b1u, w2, b2, g, b, alpha, *, tt=1024, ch=160):
    t, d = x.shape
    tt = min(tt, t)
    assert ch % 16 == 0 and ch <= 2 * LANES
    f = w2.shape[1]
    return pl.pallas_call(
        functools.partial(_moe_kernel, tt=tt, ch=ch, alpha=alpha),
        grid=(t // tt, N_EXPERTS),
        in_specs=[
            pl.BlockSpec((tt, d), lambda i, e: (i, 0)),
            pl.BlockSpec((LANES, d), lambda i, e: (0, 0)),
            pl.BlockSpec((LANES, 1), lambda i, e: (0, 0)),
            pl.BlockSpec((1, d, f), lambda i, e: (e, 0, 0)),
            pl.BlockSpec((1, d, f), lambda i, e: (e, 0, 0)),
            pl.BlockSpec((1, 1, f), lambda i, e: (e, 0, 0)),
            pl.BlockSpec((1, 1, f), lambda i, e: (e, 0, 0)),
            pl.BlockSpec((1, f, d), lambda i, e: (e, 0, 0)),
            pl.BlockSpec((1, 1, d), lambda i, e: (e, 0, 0)),
            pl.BlockSpec((1, d), lambda i, e: (0, 0)),
            pl.BlockSpec((1, d), lambda i, e: (0, 0)),
        ],
        out_specs=pl.BlockSpec((tt, d), lambda i, e: (i, 0)),
        out_shape=jax.ShapeDtypeStruct((t, d), F32),
        scratch_shapes=[
            pltpu.VMEM((tt, d), BF16),
            pltpu.VMEM((LANES, tt), F32),
            pltpu.VMEM((tt, LANES), F32),
            pltpu.VMEM((tt, LANES), F32),
            pltpu.VMEM((LANES, LANES), F32),
            pltpu.VMEM((tt, d), F32),
        ],
        compiler_params=_cparams(("arbitrary", "arbitrary")),
        name="moe_ln",
    )(x, wr_t, br, w1g, w1u, b1g, b1u, w2, b2, g, b)


def _ple_kernel(x_ref, p_ref, wg_ref, wp_ref, o_ref):
    x = x_ref[...]
    gate = jax.nn.sigmoid(_dot(x.astype(BF16), wg_ref[...]))
    emb = _dot(p_ref[...].astype(BF16), wp_ref[...])
    o_ref[...] = x + gate * emb


def _ple(x, p, wg, wp, *, tm=512):
    s, d = x.shape
    dp = p.shape[1]
    return pl.pallas_call(
        _ple_kernel,
        grid=(s // tm,),
        in_specs=[
            pl.BlockSpec((tm, d), lambda i: (i, 0)),
            pl.BlockSpec((tm, dp), lambda i: (i, 0)),
            pl.BlockSpec((d, d), lambda i: (0, 0)),
            pl.BlockSpec((dp, d), lambda i: (0, 0)),
        ],
        out_specs=pl.BlockSpec((tm, d), lambda i: (i, 0)),
        out_shape=jax.ShapeDtypeStruct((s, d), F32),
        compiler_params=_cparams(("arbitrary",)),
        name="ple",
    )(x, p, wg, wp)


def _mixer_ab(x, w_in, cos_t, sin_t, *, sb_heads, moba_heads):
    s = x.shape[0]
    wa = sb_heads * HEAD_DIM
    wb = moba_heads * HEAD_DIM
    qa, ka, va, qb, kb, vb = jnp.split(w_in, np.cumsum([wa, wa, wa, wb, wb]).tolist(), axis=1)
    w = jnp.concatenate([qb, kb, qa, ka, va], axis=1).astype(BF16)
    att = HEAD_DIM ** -0.5
    scale = jnp.concatenate([jnp.full((wb,), att * LOG2E), jnp.ones((wb,)), jnp.full((wa,), att),
                             jnp.ones((2 * wa,))]).astype(F32)[None]
    tn = cos_t.shape[1]
    h0 = _proj(x, w, scale, cos_t, sin_t, ((0, 2 * wb // tn),), tn=tn)
    n_pairs = moba_heads // 2
    vt = _proj_t(x, vb.astype(BF16), jnp.ones((1, wb), F32), groups=n_pairs)
    c = lambda off: off // LANES
    o_a = _sb_attention(h0, q_col=c(2 * wb), k_col=c(2 * wb + wa), v_col=c(2 * wb + 2 * wa),
                        n_pairs=sb_heads // 2)
    nb = s // MOBA_BLOCK
    kaug, kmean = _moba_prep(h0, k_col=c(wb), n_pairs=n_pairs)
    kmean = jnp.pad(kmean.reshape(n_pairs, nb, LANES), ((0, 0), (0, LANES - nb), (0, 0)))
    kmean = kmean.reshape(n_pairs * LANES, LANES)
    o_b = _moba_attention(h0, kaug, vt, kmean, q_col=0, n_pairs=n_pairs)
    return o_a, o_b


def _mixer_c(x, w_in, cos_t, sin_t, *, n_heads):
    width = n_heads * HEAD_DIM
    iw = IDX_HEADS * HEAD_DIM
    d = w_in.shape[0]
    q, k, v, qi, ki, wi = jnp.split(w_in, np.cumsum([width, width, width, iw, HEAD_DIM]).tolist(), axis=1)
    tn = cos_t.shape[1]
    ki_pad = jnp.zeros((d, tn - 2 * HEAD_DIM), w_in.dtype)
    w = jnp.concatenate([q, k, qi, ki, ki, ki_pad], axis=1).astype(BF16)
    att = HEAD_DIM ** -0.5
    scale = jnp.concatenate([jnp.full((width,), att * LOG2E), jnp.ones((width,)), jnp.full((iw,), att),
                             jnp.ones((tn,))]).astype(F32)[None]
    h1 = _proj(x, w, scale, cos_t, sin_t, ((0, (2 * width + iw) // tn + 1),), tn=tn)
    wi_pad = jnp.zeros((d, LANES - IDX_HEADS), w_in.dtype)
    wv = jnp.concatenate([v, wi, wi_pad], axis=1).astype(BF16)
    vscale = jnp.concatenate([jnp.ones((width,)), jnp.full((LANES,), IDX_HEADS ** -0.5)]).astype(F32)[None]
    vt = _proj_t(x, wv, vscale, groups=3)
    return _dsa_attention(h1, vt, n_heads=n_heads)


def _split_kernel(w_ref, perm_ref, g_ref, u_ref):
    half = g_ref.shape[2]
    both = _dot(w_ref[0, 0].astype(BF16), perm_ref[...])
    g_ref[0] = both[:, :half].astype(BF16)
    u_ref[0] = both[:, half:].astype(BF16)


def _split_gate_up(w1_all, layer, *, tc=512):
    _, n_e, d, f2 = w1_all.shape
    half = tc // 2
    src = jnp.concatenate([2 * jnp.arange(half), 2 * jnp.arange(half) + 1])
    perm = (jnp.arange(tc)[:, None] == src[None, :]).astype(BF16)
    out = jax.ShapeDtypeStruct((n_e, d, f2 // 2), BF16)
    return pl.pallas_call(
        _split_kernel,
        grid=(n_e, f2 // tc),
        in_specs=[pl.BlockSpec((1, 1, d, tc), lambda e, c: (layer, e, 0, c)),
                  pl.BlockSpec((tc, tc), lambda e, c: (0, 0))],
        out_specs=[pl.BlockSpec((1, d, half), lambda e, c: (e, 0, c)),
                   pl.BlockSpec((1, d, half), lambda e, c: (e, 0, c))],
        out_shape=[out, out],
        compiler_params=_cparams(("arbitrary", "arbitrary")),
        name="split_gate_up",
    )(w1_all, perm)


def _cast_kernel(w_ref, o_ref):
    o_ref[0] = w_ref[0, 0].astype(o_ref.dtype)


def _cast_layer(w_all, layer):
    _, n_e, f, d = w_all.shape
    return pl.pallas_call(
        _cast_kernel,
        grid=(n_e,),
        in_specs=[pl.BlockSpec((1, 1, f, d), lambda e: (layer, e, 0, 0))],
        out_specs=pl.BlockSpec((1, f, d), lambda e: (e, 0, 0)),
        out_shape=jax.ShapeDtypeStruct((n_e, f, d), BF16),
        compiler_params=_cparams(("arbitrary",)),
        name="cast_w2",
    )(w_all)


def _moe_weights(w_r, b_r, w1_all, b1, w2_all, b2, layer):
    wr_t = jnp.pad(w_r.T, ((0, LANES - N_EXPERTS), (0, 0)))
    br = jnp.pad(b_r, (0, LANES - N_EXPERTS))[:, None]
    w1g, w1u = _split_gate_up(w1_all, layer)
    return (wr_t, br, w1g, w1u, b1[:, None, 0::2], b1[:, None, 1::2], _cast_layer(w2_all, layer),
            b2[:, None, :])


def kernel(x, p, ab_w_in, ab_w_out, c_w_in, c_w_out, ln_g, ln_b, router_w, router_b,
           moe_w1, moe_b1, moe_w2, moe_b2, ple_w_proj, ple_w_gate):
    b, s, d = x.shape
    assert b == 1
    depth = p.shape[0]
    alpha = float((2 * depth) ** 0.25)
    cos_t, sin_t = _rope_tables(s, 2 * LANES)
    sb_heads = moba_heads = ab_w_out.shape[1] // (2 * HEAD_DIM)
    dsa_heads = c_w_out.shape[1] // HEAD_DIM
    xs = x[0]
    for i in range(depth):
        j = i // 2
        g = ln_g[i][:, None, :]
        bb = ln_b[i][:, None, :]
        if i % 2 == 0:
            o_a, o_b = _mixer_ab(xs, ab_w_in[j], cos_t, sin_t, sb_heads=sb_heads, moba_heads=moba_heads)
            w_out = ab_w_out[j].astype(BF16)
            wa = sb_heads * HEAD_DIM
            xs = _outproj_ln([o_a, o_b], [w_out[:wa], w_out[wa:]], xs, g[0], bb[0], alpha)
        else:
            o_c = _mixer_c(xs, c_w_in[j], cos_t, sin_t, n_heads=dsa_heads)
            xs = _outproj_ln([o_c], [c_w_out[j].astype(BF16)], xs, g[0], bb[0], alpha)
        mw = _moe_weights(router_w[i], router_b[i], moe_w1, moe_b1[i], moe_w2, moe_b2[i], i)
        xs = _moe_ln(xs, *mw, g[1], bb[1], alpha)
        xs = _ple(xs, p[i, 0], ple_w_gate[i].astype(BF16), ple_w_proj[i].astype(BF16))
    return xs[None]
```

```python
import functools

import jax
import jax.numpy as jnp
import numpy as np
from jax import lax
from jax.experimental import pallas as pl
from jax.experimental.pallas import tpu as pltpu

F32 = jnp.float32
BF16 = jnp.bfloat16
I32 = jnp.int32
I16 = jnp.int16

HEAD_DIM = 64
LANES = 128
MOBA_BLOCK = 256
MOBA_TOPK = 3
DSA_TOPK = 256
IDX_HEADS = 8
N_EXPERTS = 32
TOP_K = 4
SWIGLU_LIMIT = 7.0
SWIGLU_ALPHA = 1.702
ROPE_THETA = 10000.0
LN_EPS = 1e-5
NEG = -1e30
EXP_FLUSH = -88.0
INT_MIN = -(2 ** 31)
I16_MIN = -(2 ** 15)
COUNT_UNROLL = 4
LOG2E = 1.4426950408889634
ONES_ROWS = 16
VMEM_LIMIT = 56 * 1024 * 1024


def _cparams(sem, flags=None):
    return pltpu.CompilerParams(dimension_semantics=sem, vmem_limit_bytes=VMEM_LIMIT, flags=flags)


def _dot(a, b):
    return jnp.dot(a, b, preferred_element_type=F32)


def _dot_nt(a, b):
    return lax.dot_general(a, b, (((1,), (1,)), ((), ())), preferred_element_type=F32)


def _dot_tn(a, b):
    return lax.dot_general(a, b, (((0,), (0,)), ((), ())), preferred_element_type=F32)


def _iota(shape, dim):
    return lax.broadcasted_iota(I32, shape, dim)


def _proj_kernel(x_ref, w_ref, scale_ref, cos_ref, sin_ref, o_ref, xb_ref, *, rope_ranges, tn):
    j = pl.program_id(1)

    @pl.when(j == 0)
    def _():
        xb_ref[...] = x_ref[...].astype(BF16)

    h = _dot(xb_ref[...], w_ref[...]) * scale_ref[...]
    is_rope = functools.reduce(jnp.logical_or, [(j >= a) & (j < b) for a, b in rope_ranges])

    @pl.when(is_rope)
    def _():
        lane = _iota(h.shape, 1)
        first_half = (lane % HEAD_DIM) < (HEAD_DIM // 2)
        partner = jnp.where(first_half, pltpu.roll(h, tn - HEAD_DIM // 2, 1),
                            pltpu.roll(h, HEAD_DIM // 2, 1))
        o_ref[...] = (h * cos_ref[...] + partner * sin_ref[...]).astype(o_ref.dtype)

    @pl.when(jnp.logical_not(is_rope))
    def _():
        o_ref[...] = h.astype(o_ref.dtype)


def _proj(x, w, scale, cos_t, sin_t, rope_ranges, *, tm=512, tn=256):
    s, d = x.shape
    n = w.shape[1]
    return pl.pallas_call(
        functools.partial(_proj_kernel, rope_ranges=rope_ranges, tn=tn),
        grid=(s // tm, n // tn),
        in_specs=[
            pl.BlockSpec((tm, d), lambda i, j: (i, 0)),
            pl.BlockSpec((d, tn), lambda i, j: (0, j)),
            pl.BlockSpec((1, tn), lambda i, j: (0, j)),
            pl.BlockSpec((tm, tn), lambda i, j: (i, 0)),
            pl.BlockSpec((tm, tn), lambda i, j: (i, 0)),
        ],
        out_specs=pl.BlockSpec((tm, tn), lambda i, j: (i, j)),
        out_shape=jax.ShapeDtypeStruct((s, n), BF16),
        scratch_shapes=[pltpu.VMEM((tm, d), BF16)],
        compiler_params=_cparams(("arbitrary", "arbitrary")),
        name="in_proj",
    )(x, w, scale, cos_t, sin_t)


def _proj_t_kernel(x_ref, w_ref, scale_ref, o_ref, *, groups):
    h = _dot(x_ref[...].astype(BF16), w_ref[...]) * scale_ref[...]
    for g in range(groups):
        o_ref[g, 0] = h[:, g * LANES:(g + 1) * LANES].T.astype(o_ref.dtype)


def _proj_t(x, w, scale, *, groups, tm=256):
    s, d = x.shape
    n = w.shape[1]
    tc = groups * LANES
    return pl.pallas_call(
        functools.partial(_proj_t_kernel, groups=groups),
        grid=(s // tm, n // tc),
        in_specs=[
            pl.BlockSpec((tm, d), lambda i, j: (i, 0)),
            pl.BlockSpec((d, tc), lambda i, j: (0, j)),
            pl.BlockSpec((1, tc), lambda i, j: (0, j)),
        ],
        out_specs=pl.BlockSpec((groups, 1, LANES, tm), lambda i, j: (j, i, 0, 0)),
        out_shape=jax.ShapeDtypeStruct((n // LANES, s // tm, LANES, tm), BF16),
        compiler_params=_cparams(("arbitrary", "arbitrary")),
        name="proj_t",
    )(x, w, scale)


def _rope_tables(s, width):
    half = HEAD_DIM // 2
    inv = ROPE_THETA ** (-jnp.arange(half, dtype=F32) / half)
    ang = jnp.arange(s, dtype=F32)[:, None] * inv[None, :]
    cos = jnp.cos(ang)
    sin = jnp.sin(ang)
    reps = width // HEAD_DIM
    cos_t = jnp.tile(jnp.concatenate([cos, cos], axis=1), (1, reps))
    sin_t = jnp.tile(jnp.concatenate([-sin, sin], axis=1), (1, reps))
    return cos_t, sin_t


def _sb_kernel(q_ref, k_ref, v_ref, o_ref, acc_ref, run_ref, *, t, pp):
    i = pl.program_id(1)
    lane = _iota((t, LANES), 1)
    lo = lane < HEAD_DIM
    qh = []
    for p in range(pp):
        q = q_ref[:, p * LANES:(p + 1) * LANES]
        zero = jnp.zeros_like(q)
        qh.extend([jnp.where(lo, q, zero), jnp.where(lo, zero, q)])
    row = _iota((t, t), 0)
    col = _iota((t, t), 1)
    incl = (row >= col).astype(BF16)
    ones = jnp.ones((t, LANES), BF16)
    strict = col < row

    acc_ref[...] = jnp.zeros_like(acc_ref)
    run_ref[...] = jnp.zeros_like(run_ref)

    def tile(kt, diagonal):
        start = pl.multiple_of(kt * t, t)
        heads = range(2 * pp)
        sls = [slice((h // 2) * LANES, (h // 2 + 1) * LANES) for h in heads]
        zs = [_dot_nt(qh[h], k_ref[pl.ds(start, t), sls[h]]) for h in heads]
        ls16s = []
        for h in heads:
            log_stay = -(jnp.maximum(zs[h], 0.0) + jnp.log1p(jnp.exp(-jnp.abs(zs[h]))))
            if diagonal:
                log_stay = jnp.where(strict, log_stay, 0.0)
            ls16s.append(log_stay.astype(BF16))
        cums = [_dot(ls16s[h], incl) for h in heads]
        tots = [_dot(ls16s[h], ones) for h in heads]
        a16s = []
        for h in heads:
            a = jnp.exp(zs[h] + cums[h] + run_ref[h])
            if diagonal:
                a = jnp.where(strict, a, 0.0)
            a16s.append(a.astype(BF16))
        pvs = [_dot(a16s[h], v_ref[pl.ds(start, t), sls[h]]) for h in heads]
        for h in heads:
            acc_ref[h] += pvs[h]
            run_ref[h] += tots[h]

    tile(i, True)

    def cond(c):
        kt, live = c
        return jnp.logical_and(kt >= 0, live)

    def body(c):
        kt, _ = c
        tile(kt, False)
        live = jnp.max(run_ref[...]) > EXP_FLUSH
        return kt - 1, live

    lax.while_loop(cond, body, (i - 1, jnp.bool_(True)))
    for p in range(pp):
        o_ref[:, p * LANES:(p + 1) * LANES] = jnp.where(lo, acc_ref[2 * p], acc_ref[2 * p + 1]).astype(o_ref.dtype)


def _sb_attention(h0, *, q_col, k_col, v_col, n_pairs, t=128, pp=4):
    s = h0.shape[0]
    assert n_pairs % pp == 0 and q_col % pp == 0 and k_col % pp == 0 and v_col % pp == 0
    w = pp * LANES
    resident = pl.Buffered(1)
    return pl.pallas_call(
        functools.partial(_sb_kernel, t=t, pp=pp),
        grid=(n_pairs // pp, s // t),
        in_specs=[
            pl.BlockSpec((t, w), lambda p, i: (i, q_col // pp + p)),
            pl.BlockSpec((s, w), lambda p, i: (0, k_col // pp + p), pipeline_mode=resident),
            pl.BlockSpec((s, w), lambda p, i: (0, v_col // pp + p), pipeline_mode=resident),
        ],
        out_specs=pl.BlockSpec((t, w), lambda p, i: (i, p)),
        out_shape=jax.ShapeDtypeStruct((s, n_pairs * LANES), BF16),
        scratch_shapes=[pltpu.VMEM((2 * pp, t, LANES), F32), pltpu.VMEM((2 * pp, t, LANES), F32)],
        compiler_params=_cparams(("arbitrary", "arbitrary")),
        name="stick_breaking",
    )(h0, h0, h0)


def _moba_prep_kernel(k_ref, kaug_ref, kmean_ref):
    b = pl.program_id(1)
    k = k_ref[...]
    kmean_ref[...] = jnp.mean(k.astype(F32), axis=0, keepdims=True)[None]
    onehot = (_iota(k.shape, 1) == b).astype(BF16)
    kaug_ref[...] = jnp.concatenate([k, onehot], axis=1)


def _moba_prep(h0, *, k_col, n_pairs):
    s = h0.shape[0]
    nb = s // MOBA_BLOCK
    return pl.pallas_call(
        _moba_prep_kernel,
        grid=(n_pairs, nb),
        in_specs=[pl.BlockSpec((MOBA_BLOCK, LANES), lambda p, b: (b, k_col + p))],
        out_specs=[
            pl.BlockSpec((MOBA_BLOCK, 2 * LANES), lambda p, b: (b, p)),
            pl.BlockSpec((1, 1, LANES), lambda p, b: (p * nb + b, 0, 0)),
        ],
        out_shape=[
            jax.ShapeDtypeStruct((s, n_pairs * 2 * LANES), BF16),
            jax.ShapeDtypeStruct((n_pairs * nb, 1, LANES), F32),
        ],
        compiler_params=_cparams(("arbitrary", "arbitrary")),
        name="moba_prep",
    )(h0)


def _first_argmax(x, lane):
    m = jnp.max(x, axis=1, keepdims=True)
    idx = jnp.min(jnp.where(x == m, lane, float(LANES)), axis=1, keepdims=True)
    return m, idx


def _moba_kernel(q_ref, kaug_ref, vt_ref, kmean_ref, o_ref, m_ref, acc_ref, st_ref, *, nb, g):
    t = MOBA_BLOCK
    i = pl.program_id(1)
    q = q_ref[...]
    lane = _iota((t, LANES), 1)
    lo = lane < HEAD_DIM
    zero = jnp.zeros_like(q)
    qh = (jnp.where(lo, q, zero), jnp.where(lo, zero, q))
    kmean = kmean_ref[...]
    km_hi = kmean.astype(BF16)
    km_lo = (kmean - km_hi.astype(F32)).astype(BF16)
    q2 = jnp.concatenate(qh, axis=0)
    gate = _dot_nt(q2, km_hi) + _dot_nt(q2, km_lo)
    lane2 = _iota((2 * t, LANES), 1)
    lane_f = lane2.astype(F32)
    gate = jnp.where(lane2 < i, gate, -jnp.inf)
    chosen = lane2 == i
    for _ in range(MOBA_TOPK):
        m, idx = _first_argmax(gate, lane_f)
        pick = lane_f == idx
        chosen = chosen | (pick & (m > -jnp.inf))
        gate = jnp.where(pick, -jnp.inf, gate)
    bias = jnp.where(chosen | (lane2 >= nb), 0.0, NEG)
    q_aug = [jnp.concatenate([qh[h].astype(F32), bias[h * t:(h + 1) * t]], axis=1).T.astype(BF16)
             for h in range(2)]
    q_both = jnp.concatenate(q_aug, axis=1)

    tk = g * t
    last = lax.shift_right_logical(i, int(np.log2(g)))
    ones = jnp.ones((ONES_ROWS, tk), BF16)

    def scores(u):
        start = pl.multiple_of(u * tk, tk)
        return _dot(kaug_ref[pl.ds(start, tk), :], q_both)

    m_ref[...] = jnp.full(m_ref.shape, NEG, F32)
    acc_ref[...] = jnp.zeros_like(acc_ref)
    key_pos = _iota((tk, 2 * t), 0) + (last * g - i) * t
    query_pos = _iota((tk, 2 * t), 1) & (t - 1)
    st_ref[...] = jnp.where(key_pos <= query_pos, scores(last), NEG)

    def body(n, carry):
        cur = jnp.where(n == 0, last, n - 1)
        nxt = jnp.minimum(n, jnp.maximum(last - 1, 0))
        st_next = scores(nxt)
        vt = jnp.concatenate([vt_ref[0, cur * g + b] for b in range(g)], axis=1)
        vt = jnp.concatenate([vt, ones], axis=0)
        st = st_ref[...]
        m_old = m_ref[...]
        m_new = jnp.maximum(m_old, jnp.max(st, axis=0, keepdims=True))
        alpha = jnp.exp2(m_old - m_new)
        pt = jnp.exp2(st - m_new).astype(BF16)
        acc_ref[...] = alpha * acc_ref[...] + _dot(vt, pt)
        m_ref[...] = m_new
        st_ref[...] = st_next
        return carry

    lax.fori_loop(0, last + 1, body, 0)
    acc = acc_ref[...]
    outs = [acc[:LANES, h * t:(h + 1) * t] / acc[LANES:LANES + 1, h * t:(h + 1) * t] for h in range(2)]
    first = _iota((LANES, t), 0) < HEAD_DIM
    o_ref[...] = jnp.where(first, outs[0], outs[1]).T.astype(o_ref.dtype)


def _moba_attention(h0, kaug, vt, kmean, *, q_col, n_pairs, g=4):
    s = h0.shape[0]
    t = MOBA_BLOCK
    nb = s // t
    g = min(g, nb)
    assert nb <= LANES and nb % g == 0
    return pl.pallas_call(
        functools.partial(_moba_kernel, nb=nb, g=g),
        grid=(n_pairs, nb),
        in_specs=[
            pl.BlockSpec((t, LANES), lambda p, i: (i, q_col + p)),
            pl.BlockSpec((s, 2 * LANES), lambda p, i: (0, p)),
            pl.BlockSpec((1, nb, LANES, t), lambda p, i: (p, 0, 0, 0)),
            pl.BlockSpec((LANES, LANES), lambda p, i: (p, 0)),
        ],
        out_specs=pl.BlockSpec((t, LANES), lambda p, i: (i, p)),
        out_shape=jax.ShapeDtypeStruct((s, n_pairs * LANES), BF16),
        scratch_shapes=[pltpu.VMEM((1, 2 * t), F32),
                        pltpu.VMEM((LANES + ONES_ROWS, 2 * t), F32),
                        pltpu.VMEM((g * t, 2 * t), F32)],
        compiler_params=_cparams(("arbitrary", "arbitrary")),
        name="moba",
    )(h0, kaug, vt, kmean)


def _dsa_kernel(it_ref, jt_ref, qi_ref, wi_ref, ki_ref, q_ref, k_ref, vt_ref, o_ref,
                key_ref, half_ref, thr_ref, need_ref, seen_ref, m_ref, acc_ref, qt_ref,
                *, t, g, n_heads, topk):
    step = pl.program_id(0)
    i = it_ref[step]
    j = jt_ref[step]
    lane = _iota((t, LANES), 1)
    lo = lane < HEAD_DIM
    key_le_query = _iota((t, t), 0) <= _iota((t, t), 1)
    n_pairs = n_heads // 2

    def split_t(pair):
        pf = pair.astype(F32)
        return (jnp.where(lo, pf, 0.0).T.astype(BF16), jnp.where(lo, 0.0, pf).T.astype(BF16))

    @pl.when(j == 0)
    def _select():
        qi = qi_ref[...]
        wi = wi_ref[0, 0].astype(F32)
        qit = [jnp.concatenate(split_t(qi[:, p * LANES:(p + 1) * LANES]), axis=1)
               for p in range(IDX_HEADS // 2)]
        q = q_ref[...]
        for p in range(n_pairs):
            qt_ref[p] = jnp.concatenate(split_t(q[:, p * LANES:(p + 1) * LANES]), axis=1)

        def score_tile(kt, carry):
            start = pl.multiple_of(kt * t, t)
            kblk = ki_ref[pl.ds(start, t), :]
            logits = [_dot(kblk, qit[p]) for p in range(IDX_HEADS // 2)]
            sc = jnp.zeros((t, t), F32)
            for h in range(IDX_HEADS):
                lg = logits[h // 2][:, (h % 2) * t:(h % 2 + 1) * t]
                sc = sc + wi[h:h + 1, :] * jnp.maximum(lg, 0.0)
            bits = pltpu.bitcast(sc, I32)
            key = jnp.where(bits >= 0, bits, bits ^ jnp.int32(0x7FFFFFFF))
            key = jnp.where(jnp.logical_or(kt < i, key_le_query), key, jnp.int32(INT_MIN))
            key_ref[kt] = key
            half_ref[kt] = lax.shift_right_arithmetic(key, jnp.int32(16)).astype(I16)
            return carry

        lax.fori_loop(0, i + 1, score_tile, 0)
        for b in range(1, g + 1):
            key_ref[i + b] = jnp.full((t, t), INT_MIN, I32)
        for b in range(1, COUNT_UNROLL):
            half_ref[i + b] = jnp.full((t, t), I16_MIN, I16)
        n_trips = (i + COUNT_UNROLL) // COUNT_UNROLL

        def count(pred):
            def cbody(trip, c):
                for kt in [COUNT_UNROLL * trip + u for u in range(COUNT_UNROLL)]:
                    hit = jnp.where(pred(half_ref[kt]), jnp.int16(1), jnp.int16(0))
                    for r in range(t // 16):
                        c = c + hit[r * 16:(r + 1) * 16]
                return c
            c = lax.fori_loop(0, n_trips, cbody, jnp.zeros((16, t), I16))
            return jnp.sum(c.astype(F32), axis=0, keepdims=True)

        def search(base):
            v = jnp.zeros((1, t), I32)
            for b in range(15, -1, -1):
                cand = v | jnp.int32(1 << b)
                cand16 = (cand + I16_MIN).astype(I16)
                cnt = base + count(lambda hh, c=cand16: hh >= c)
                v = jnp.where(cnt >= float(topk), cand, v)
            return v

        hi = search(0.0) + I16_MIN
        hi16 = hi.astype(I16)
        n_above = count(lambda hh: hh > hi16)

        def narrow(kt, carry):
            low = (key_ref[kt] & jnp.int32(0xFFFF)) + I16_MIN
            half_ref[kt] = jnp.where(half_ref[kt] == hi16, low.astype(I16), jnp.int16(I16_MIN))
            return carry

        lax.fori_loop(0, i + 1, narrow, 0)
        lo = search(n_above)
        lo16 = (lo + I16_MIN).astype(I16)
        n_gt = n_above + count(lambda hh: hh > lo16)
        thr_ref[...] = hi * 65536 + lo
        need_ref[...] = float(topk) - n_gt
        seen_ref[...] = jnp.zeros_like(seen_ref)
        m_ref[...] = jnp.full(m_ref.shape, NEG, F32)
        acc_ref[...] = jnp.zeros_like(acc_ref)

    thr = thr_ref[...]
    need = need_ref[...]
    earlier = (_iota((t, t), 1) < _iota((t, t), 0)).astype(BF16)
    seen = seen_ref[...]
    biases = []
    for b in range(g):
        blk = j * g + b
        key = key_ref[blk]
        eq = key == thr
        rank = seen + _dot(earlier, jnp.where(eq, 1.0, 0.0).astype(BF16))
        sel = (key > thr) | (eq & (rank < need))
        sel = sel & jnp.logical_or(blk < i, jnp.logical_and(blk == i, key_le_query))
        seen = seen + jnp.sum(jnp.where(eq, 1.0, 0.0), axis=0, keepdims=True)
        biases.append(jnp.where(sel, 0.0, NEG))
    seen_ref[...] = seen
    bias = jnp.concatenate(biases, axis=0)
    bias = jnp.concatenate([bias, bias], axis=1)

    kb = k_ref[...]
    ones = jnp.ones((ONES_ROWS, g * t), BF16)

    def scores(p):
        return _dot(kb[:, p * LANES:(p + 1) * LANES], qt_ref[p]) + bias

    def softmax(p, st):
        m_old = m_ref[p]
        m_new = jnp.maximum(m_old, jnp.max(st, axis=0, keepdims=True))
        m_ref[p] = m_new
        return jnp.exp2(m_old - m_new), jnp.exp2(st - m_new).astype(BF16)

    def accumulate(p, alpha, pt):
        vt = jnp.concatenate([vt_ref[p, b] for b in range(g)], axis=1)
        vt = jnp.concatenate([vt, ones], axis=0)
        acc_ref[p] = alpha * acc_ref[p] + _dot(vt, pt)

    st_next = scores(0)
    for p in range(n_pairs):
        st = st_next
        if p + 1 < n_pairs:
            st_next = scores(p + 1)
        accumulate(p, *softmax(p, st))

    @pl.when(j == lax.shift_right_logical(i, int(np.log2(g))))
    def _finish():
        first = _iota((LANES, t), 0) < HEAD_DIM
        for p in range(n_pairs):
            acc = acc_ref[p]
            a0, a1 = acc[:, :t], acc[:, t:]
            out_t = jnp.where(first, a0[:LANES] / a0[LANES:LANES + 1], a1[:LANES] / a1[LANES:LANES + 1])
            o_ref[:, p * LANES:(p + 1) * LANES] = out_t.T.astype(o_ref.dtype)


def _dsa_attention(h1, vt, *, n_heads, t=256, g=4):
    s = h1.shape[0]
    width = n_heads * HEAD_DIM
    nt = s // t
    g = min(g, nt)
    assert nt % g == 0
    topk = min(DSA_TOPK, s // 4)
    iw = IDX_HEADS * HEAD_DIM
    qi_blk = 2 * width // iw
    ki_blk = (2 * width + iw) // LANES
    wi_grp = width // LANES
    tri = [(i, j) for i in range(nt) for j in range(i // g + 1)]
    it = jnp.asarray([a for a, _ in tri], I32)
    jt = jnp.asarray([b for _, b in tri], I32)
    grid_spec = pltpu.PrefetchScalarGridSpec(
        num_scalar_prefetch=2,
        grid=(len(tri),),
        in_specs=[
            pl.BlockSpec((t, iw), lambda n, it, jt: (it[n], qi_blk)),
            pl.BlockSpec((1, 1, ONES_ROWS, t), lambda n, it, jt: (wi_grp, it[n], 0, 0)),
            pl.BlockSpec((s, LANES), lambda n, it, jt: (0, ki_blk)),
            pl.BlockSpec((t, width), lambda n, it, jt: (it[n], 0)),
            pl.BlockSpec((g * t, width), lambda n, it, jt: (jt[n], 1)),
            pl.BlockSpec((n_heads // 2, g, LANES, t), lambda n, it, jt: (0, jt[n], 0, 0)),
        ],
        out_specs=pl.BlockSpec((t, width), lambda n, it, jt: (it[n], 0)),
        scratch_shapes=[
            pltpu.VMEM((nt + g, t, t), I32),
            pltpu.VMEM((nt + COUNT_UNROLL - 1, t, t), I16),
            pltpu.VMEM((1, t), I32),
            pltpu.VMEM((1, t), F32),
            pltpu.VMEM((1, t), F32),
            pltpu.VMEM((n_heads // 2, 1, 2 * t), F32),
            pltpu.VMEM((n_heads // 2, LANES + ONES_ROWS, 2 * t), F32),
            pltpu.VMEM((n_heads // 2, LANES, 2 * t), BF16),
        ],
    )
    return pl.pallas_call(
        functools.partial(_dsa_kernel, t=t, g=g, n_heads=n_heads, topk=topk),
        grid_spec=grid_spec,
        out_shape=jax.ShapeDtypeStruct((s, width), BF16),
        compiler_params=_cparams(("arbitrary",)),
        name="dsa",
    )(it, jt, h1, vt, h1, h1, h1, vt)


def _layer_norm(y, g, b):
    mu = jnp.mean(y, axis=-1, keepdims=True)
    yc = y - mu
    var = jnp.mean(yc * yc, axis=-1, keepdims=True)
    return yc * lax.rsqrt(var + LN_EPS) * g + b


def _outproj_kernel(*refs, n_in, alpha):
    o_refs = refs[:n_in]
    w_refs = refs[n_in:2 * n_in]
    x_ref, g_ref, b_ref, out_ref = refs[2 * n_in:]
    y = alpha * x_ref[...]
    for o_ref, w_ref in zip(o_refs, w_refs):
        y = y + _dot(o_ref[...], w_ref[...])
    out_ref[...] = _layer_norm(y, g_ref[...], b_ref[...])


def _outproj_ln(os_, ws, x, g, b, alpha, *, tm=256):
    s, d = x.shape
    n_in = len(os_)
    in_specs = ([pl.BlockSpec((tm, o.shape[1]), lambda i: (i, 0)) for o in os_]
                + [pl.BlockSpec(w.shape, lambda i: (0, 0)) for w in ws]
                + [pl.BlockSpec((tm, d), lambda i: (i, 0)),
                   pl.BlockSpec((1, d), lambda i: (0, 0)),
                   pl.BlockSpec((1, d), lambda i: (0, 0))])
    return pl.pallas_call(
        functools.partial(_outproj_kernel, n_in=n_in, alpha=alpha),
        grid=(s // tm,),
        in_specs=in_specs,
        out_specs=pl.BlockSpec((tm, d), lambda i: (i, 0)),
        out_shape=jax.ShapeDtypeStruct((s, d), F32),
        compiler_params=_cparams(("arbitrary",)),
        name="out_proj_ln",
    )(*os_, *ws, x, g, b)


def _moe_kernel(x_ref, wr_ref, br_ref, w1g_ref, w1u_ref, b1g_ref, b1u_ref, w2_ref, b2_ref,
                g_ref, b_ref, out_ref,
                xb_ref, rank_ref, rank_t_ref, gate_t_ref, cnt_ref, acc_ref,
                *, tt, ch, alpha):
    e = pl.program_id(1)
    n_e = pl.num_programs(1)

    @pl.when(e == 0)
    def _route():
        x = x_ref[...]
        xb_ref[...] = x.astype(BF16)
        logits = lax.dot_general(wr_ref[...], x, (((1,), (1,)), ((), ())),
                                 precision=lax.Precision.HIGHEST,
                                 preferred_element_type=F32) + br_ref[...]
        row = _iota(logits.shape, 0).astype(F32)
        work = jnp.where(row < N_EXPERTS, logits, -jnp.inf)
        chosen = row < 0
        top = None
        for r in range(TOP_K):
            m = jnp.max(work, axis=0, keepdims=True)
            idx = jnp.min(jnp.where(work == m, row, float(LANES)), axis=0, keepdims=True)
            pick = row == idx
            chosen = chosen | pick
            work = jnp.where(pick, -jnp.inf, work)
            if r == 0:
                top = m
        ex = jnp.where(chosen, jnp.exp(logits - top), 0.0)
        gate = ex / jnp.sum(ex, axis=0, keepdims=True)
        sel16 = jnp.where(chosen, 1.0, 0.0).astype(BF16)
        before = (_iota((tt, tt), 0) < _iota((tt, tt), 1)).astype(BF16)
        pos = _dot(sel16, before)
        rank = jnp.where(chosen, pos, -1.0)
        rank_ref[...] = rank
        rank_t_ref[...] = rank.T
        gate_t_ref[...] = gate.T
        cnt = jnp.sum(jnp.where(chosen, 1.0, 0.0), axis=1, keepdims=True)
        cnt_ref[...] = jnp.broadcast_to(cnt, cnt_ref.shape)
        acc_ref[...] = jnp.zeros_like(acc_ref)

    lane = _iota((tt, LANES), 1)
    on_e = lane == e
    rank_col = jnp.sum(jnp.where(on_e, rank_t_ref[...], 0.0), axis=1, keepdims=True)
    gate_col = jnp.sum(jnp.where(on_e, gate_t_ref[...], 0.0), axis=1, keepdims=True)
    rank_row = rank_ref[pl.ds(e, 1), :]
    n_rows = jnp.max(cnt_ref[pl.ds(e, 1), :]).astype(I32)
    n_chunks = (n_rows + (ch - 1)) // ch
    pad_rows = jnp.zeros((2 * LANES - ch, x_ref.shape[1]), BF16)
    slot = _iota((1, 2 * LANES), 1)
    slot = jnp.where(slot < ch, slot.astype(F32), -1e9)
    w1g = w1g_ref[0]
    w1u = w1u_ref[0]
    w2 = w2_ref[0]
    b1g = b1g_ref[0]
    b1u = b1u_ref[0]
    b2 = b2_ref[0]

    def chunk(c, carry):
        r0 = (c * ch).astype(F32)
        gather = (rank_row == (r0 + _iota((ch, tt), 0).astype(F32))).astype(BF16)
        xg = _dot(gather, xb_ref[...]).astype(BF16)
        hg = jnp.minimum(_dot(xg, w1g) + b1g, SWIGLU_LIMIT)
        hu = jnp.clip(_dot(xg, w1u) + b1u, -SWIGLU_LIMIT, SWIGLU_LIMIT)
        act = (hu + 1.0) * (hg * jax.nn.sigmoid(SWIGLU_ALPHA * hg))
        y = _dot(act.astype(BF16), w2) + b2
        scatter = jnp.where(rank_col == (r0 + slot), gate_col, 0.0).astype(BF16)
        acc_ref[...] += _dot(scatter, jnp.concatenate([y.astype(BF16), pad_rows], axis=0))
        return carry

    lax.fori_loop(0, n_chunks, chunk, 0)

    @pl.when(e == n_e - 1)
    def _finish():
        out_ref[...] = _layer_norm(alpha * x_ref[...] + acc_ref[...], g_ref[...], b_ref[...])


def _moe_ln(x, wr_t, br, w1g, w1u, b1g, b1u, w2, b2, g, b, alpha, *, tt=1024, ch=160):
    t, d = x.shape
    tt = min(tt, t)
    assert ch % 16 == 0 and ch <= 2 * LANES
    f = w2.shape[1]
    return pl.pallas_call(
        functools.partial(_moe_kernel, tt=tt, ch=ch, alpha=alpha),
        grid=(t // tt, N_EXPERTS),
        in_specs=[
            pl.BlockSpec((tt, d), lambda i, e: (i, 0)),
            pl.BlockSpec((LANES, d), lambda i, e: (0, 0)),
            pl.BlockSpec((LANES, 1), lambda i, e: (0, 0)),
            pl.BlockSpec((1, d, f), lambda i, e: (e, 0, 0)),
            pl.BlockSpec((1, d, f), lambda i, e: (e, 0, 0)),
            pl.BlockSpec((1, 1, f), lambda i, e: (e, 0, 0)),
            pl.BlockSpec((1, 1, f), lambda i, e: (e, 0, 0)),
            pl.BlockSpec((1, f, d), lambda i, e: (e, 0, 0)),
            pl.BlockSpec((1, 1, d), lambda i, e: (e, 0, 0)),
            pl.BlockSpec((1, d), lambda i, e: (0, 0)),
            pl.BlockSpec((1, d), lambda i, e: (0, 0)),
        ],
        out_specs=pl.BlockSpec((tt, d), lambda i, e: (i, 0)),
        out_shape=jax.ShapeDtypeStruct((t, d), F32),
        scratch_shapes=[
            pltpu.VMEM((tt, d), BF16),
            pltpu.VMEM((LANES, tt), F32),
            pltpu.VMEM((tt, LANES), F32),
            pltpu.VMEM((tt, LANES), F32),
            pltpu.VMEM((LANES, LANES), F32),
            pltpu.VMEM((tt, d), F32),
        ],
        compiler_params=_cparams(("arbitrary", "arbitrary")),
        name="moe_ln",
    )(x, wr_t, br, w1g, w1u, b1g, b1u, w2, b2, g, b)


def _ple_kernel(x_ref, p_ref, wg_ref, wp_ref, o_ref):
    x = x_ref[...]
    gate = jax.nn.sigmoid(_dot(x.astype(BF16), wg_ref[...]))
    emb = _dot(p_ref[...].astype(BF16), wp_ref[...])
    o_ref[...] = x + gate * emb


def _ple(x, p, wg, wp, *, tm=512):
    s, d = x.shape
    dp = p.shape[1]
    return pl.pallas_call(
        _ple_kernel,
        grid=(s // tm,),
        in_specs=[
            pl.BlockSpec((tm, d), lambda i: (i, 0)),
            pl.BlockSpec((tm, dp), lambda i: (i, 0)),
            pl.BlockSpec((d, d), lambda i: (0, 0)),
            pl.BlockSpec((dp, d), lambda i: (0, 0)),
        ],
        out_specs=pl.BlockSpec((tm, d), lambda i: (i, 0)),
        out_shape=jax.ShapeDtypeStruct((s, d), F32),
        compiler_params=_cparams(("arbitrary",)),
        name="ple",
    )(x, p, wg, wp)


def _mixer_ab(x, w_in, cos_t, sin_t, *, sb_heads, moba_heads):
    s = x.shape[0]
    wa = sb_heads * HEAD_DIM
    wb = moba_heads * HEAD_DIM
    qa, ka, va, qb, kb, vb = jnp.split(w_in, np.cumsum([wa, wa, wa, wb, wb]).tolist(), axis=1)
    w = jnp.concatenate([qb, kb, qa, ka, va], axis=1).astype(BF16)
    att = HEAD_DIM ** -0.5
    scale = jnp.concatenate([jnp.full((wb,), att * LOG2E), jnp.ones((wb,)), jnp.full((wa,), att),
                             jnp.ones((2 * wa,))]).astype(F32)[None]
    tn = cos_t.shape[1]
    h0 = _proj(x, w, scale, cos_t, sin_t, ((0, 2 * wb // tn),), tn=tn)
    n_pairs = moba_heads // 2
    vt = _proj_t(x, vb.astype(BF16), jnp.ones((1, wb), F32), groups=n_pairs)
    c = lambda off: off // LANES
    o_a = _sb_attention(h0, q_col=c(2 * wb), k_col=c(2 * wb + wa), v_col=c(2 * wb + 2 * wa),
                        n_pairs=sb_heads // 2)
    nb = s // MOBA_BLOCK
    kaug, kmean = _moba_prep(h0, k_col=c(wb), n_pairs=n_pairs)
    kmean = jnp.pad(kmean.reshape(n_pairs, nb, LANES), ((0, 0), (0, LANES - nb), (0, 0)))
    kmean = kmean.reshape(n_pairs * LANES, LANES)
    o_b = _moba_attention(h0, kaug, vt, kmean, q_col=0, n_pairs=n_pairs)
    return o_a, o_b


def _mixer_c(x, w_in, cos_t, sin_t, *, n_heads):
    width = n_heads * HEAD_DIM
    iw = IDX_HEADS * HEAD_DIM
    d = w_in.shape[0]
    q, k, v, qi, ki, wi = jnp.split(w_in, np.cumsum([width, width, width, iw, HEAD_DIM]).tolist(), axis=1)
    tn = cos_t.shape[1]
    ki_pad = jnp.zeros((d, tn - 2 * HEAD_DIM), w_in.dtype)
    w = jnp.concatenate([q, k, qi, ki, ki, ki_pad], axis=1).astype(BF16)
    att = HEAD_DIM ** -0.5
    scale = jnp.concatenate([jnp.full((width,), att * LOG2E), jnp.ones((width,)), jnp.full((iw,), att),
                             jnp.ones((tn,))]).astype(F32)[None]
    h1 = _proj(x, w, scale, cos_t, sin_t, ((0, (2 * width + iw) // tn + 1),), tn=tn)
    wi_pad = jnp.zeros((d, LANES - IDX_HEADS), w_in.dtype)
    wv = jnp.concatenate([v, wi, wi_pad], axis=1).astype(BF16)
    vscale = jnp.concatenate([jnp.ones((width,)), jnp.full((LANES,), IDX_HEADS ** -0.5)]).astype(F32)[None]
    vt = _proj_t(x, wv, vscale, groups=3)
    return _dsa_attention(h1, vt, n_heads=n_heads)


def _split_kernel(w_ref, perm_ref, g_ref, u_ref):
    half = g_ref.shape[2]
    both = _dot(w_ref[0, 0].astype(BF16), perm_ref[...])
    g_ref[0] = both[:, :half].astype(BF16)
    u_ref[0] = both[:, half:].astype(BF16)


def _split_gate_up(w1_all, layer, *, tc=512):
    _, n_e, d, f2 = w1_all.shape
    half = tc // 2
    src = jnp.concatenate([2 * jnp.arange(half), 2 * jnp.arange(half) + 1])
    perm = (jnp.arange(tc)[:, None] == src[None, :]).astype(BF16)
    out = jax.ShapeDtypeStruct((n_e, d, f2 // 2), BF16)
    return pl.pallas_call(
        _split_kernel,
        grid=(n_e, f2 // tc),
        in_specs=[pl.BlockSpec((1, 1, d, tc), lambda e, c: (layer, e, 0, c)),
                  pl.BlockSpec((tc, tc), lambda e, c: (0, 0))],
        out_specs=[pl.BlockSpec((1, d, half), lambda e, c: (e, 0, c)),
                   pl.BlockSpec((1, d, half), lambda e, c: (e, 0, c))],
        out_shape=[out, out],
        compiler_params=_cparams(("arbitrary", "arbitrary")),
        name="split_gate_up",
    )(w1_all, perm)


def _cast_kernel(w_ref, o_ref):
    o_ref[0] = w_ref[0, 0].astype(o_ref.dtype)


def _cast_layer(w_all, layer):
    _, n_e, f, d = w_all.shape
    return pl.pallas_call(
        _cast_kernel,
        grid=(n_e,),
        in_specs=[pl.BlockSpec((1, 1, f, d), lambda e: (layer, e, 0, 0))],
        out_specs=pl.BlockSpec((1, f, d), lambda e: (e, 0, 0)),
        out_shape=jax.ShapeDtypeStruct((n_e, f, d), BF16),
        compiler_params=_cparams(("arbitrary",)),
        name="cast_w2",
    )(w_all)


def _moe_weights(w_r, b_r, w1_all, b1, w2_all, b2, layer):
    wr_t = jnp.pad(w_r.T, ((0, LANES - N_EXPERTS), (0, 0)))
    br = jnp.pad(b_r, (0, LANES - N_EXPERTS))[:, None]
    w1g, w1u = _split_gate_up(w1_all, layer)
    return (wr_t, br, w1g, w1u, b1[:, None, 0::2], b1[:, None, 1::2], _cast_layer(w2_all, layer),
            b2[:, None, :])


def kernel(x, p, ab_w_in, ab_w_out, c_w_in, c_w_out, ln_g, ln_b, router_w, router_b,
           moe_w1, moe_b1, moe_w2, moe_b2, ple_w_proj, ple_w_gate):
    b, s, d = x.shape
    assert b == 1
    depth = p.shape[0]
    alpha = float((2 * depth) ** 0.25)
    cos_t, sin_t = _rope_tables(s, 2 * LANES)
    sb_heads = moba_heads = ab_w_out.shape[1] // (2 * HEAD_DIM)
    dsa_heads = c_w_out.shape[1] // HEAD_DIM
    xs = x[0]
    for i in range(depth):
        j = i // 2
        g = ln_g[i][:, None, :]
        bb = ln_b[i][:, None, :]
        if i % 2 == 0:
            o_a, o_b = _mixer_ab(xs, ab_w_in[j], cos_t, sin_t, sb_heads=sb_heads, moba_heads=moba_heads)
            w_out = ab_w_out[j].astype(BF16)
            wa = sb_heads * HEAD_DIM
            xs = _outproj_ln([o_a, o_b], [w_out[:wa], w_out[wa:]], xs, g[0], bb[0], alpha)
        else:
            o_c = _mixer_c(xs, c_w_in[j], cos_t, sin_t, n_heads=dsa_heads)
            xs = _outproj_ln([o_c], [c_w_out[j].astype(BF16)], xs, g[0], bb[0], alpha)
        mw = _moe_weights(router_w[i], router_b[i], moe_w1, moe_b1[i], moe_w2, moe_b2[i], i)
        xs = _moe_ln(xs, *mw, g[1], bb[1], alpha)
        xs = _ple(xs, p[i, 0], ple_w_gate[i].astype(BF16), ple_w_proj[i].astype(BF16))
    return xs[None]
```

```python
import functools

import jax
import jax.numpy as jnp
import numpy as np
from jax import lax
from jax.experimental import pallas as pl
from jax.experimental.pallas import tpu as pltpu

F32 = jnp.float32
BF16 = jnp.bfloat16
I32 = jnp.int32
I16 = jnp.int16

HEAD_DIM = 64
LANES = 128
MOBA_BLOCK = 256
MOBA_TOPK = 3
DSA_TOPK = 256
IDX_HEADS = 8
N_EXPERTS = 32
TOP_K = 4
SWIGLU_LIMIT = 7.0
SWIGLU_ALPHA = 1.702
ROPE_THETA = 10000.0
LN_EPS = 1e-5
NEG = -1e30
EXP_FLUSH = -88.0
INT_MIN = -(2 ** 31)
I16_MIN = -(2 ** 15)
COUNT_UNROLL = 4
LOG2E = 1.4426950408889634
ONES_ROWS = 16
VMEM_LIMIT = 56 * 1024 * 1024


def _cparams(sem, flags=None):
    return pltpu.CompilerParams(dimension_semantics=sem, vmem_limit_bytes=VMEM_LIMIT, flags=flags)


def _dot(a, b):
    return jnp.dot(a, b, preferred_element_type=F32)


def _dot_nt(a, b):
    return lax.dot_general(a, b, (((1,), (1,)), ((), ())), preferred_element_type=F32)


def _dot_tn(a, b):
    return lax.dot_general(a, b, (((0,), (0,)), ((), ())), preferred_element_type=F32)


def _iota(shape, dim):
    return lax.broadcasted_iota(I32, shape, dim)


def _proj_kernel(x_ref, w_ref, scale_ref, cos_ref, sin_ref, o_ref, xb_ref, *, rope_ranges, tn):
    j = pl.program_id(1)

    @pl.when(j == 0)
    def _():
        xb_ref[...] = x_ref[...].astype(BF16)

    h = _dot(xb_ref[...], w_ref[...]) * scale_ref[...]
    is_rope = functools.reduce(jnp.logical_or, [(j >= a) & (j < b) for a, b in rope_ranges])

    @pl.when(is_rope)
    def _():
        lane = _iota(h.shape, 1)
        first_half = (lane % HEAD_DIM) < (HEAD_DIM // 2)
        partner = jnp.where(first_half, pltpu.roll(h, tn - HEAD_DIM // 2, 1),
                            pltpu.roll(h, HEAD_DIM // 2, 1))
        o_ref[...] = (h * cos_ref[...] + partner * sin_ref[...]).astype(o_ref.dtype)

    @pl.when(jnp.logical_not(is_rope))
    def _():
        o_ref[...] = h.astype(o_ref.dtype)


def _proj(x, w, scale, cos_t, sin_t, rope_ranges, *, tm=512, tn=256):
    s, d = x.shape
    n = w.shape[1]
    return pl.pallas_call(
        functools.partial(_proj_kernel, rope_ranges=rope_ranges, tn=tn),
        grid=(s // tm, n // tn),
        in_specs=[
            pl.BlockSpec((tm, d), lambda i, j: (i, 0)),
            pl.BlockSpec((d, tn), lambda i, j: (0, j)),
            pl.BlockSpec((1, tn), lambda i, j: (0, j)),
            pl.BlockSpec((tm, tn), lambda i, j: (i, 0)),
            pl.BlockSpec((tm, tn), lambda i, j: (i, 0)),
        ],
        out_specs=pl.BlockSpec((tm, tn), lambda i, j: (i, j)),
        out_shape=jax.ShapeDtypeStruct((s, n), BF16),
        scratch_shapes=[pltpu.VMEM((tm, d), BF16)],
        compiler_params=_cparams(("arbitrary", "arbitrary")),
        name="in_proj",
    )(x, w, scale, cos_t, sin_t)


def _proj_t_kernel(x_ref, w_ref, scale_ref, o_ref, *, groups):
    h = _dot(x_ref[...].astype(BF16), w_ref[...]) * scale_ref[...]
    for g in range(groups):
        o_ref[g, 0] = h[:, g * LANES:(g + 1) * LANES].T.astype(o_ref.dtype)


def _proj_t(x, w, scale, *, groups, tm=256):
    s, d = x.shape
    n = w.shape[1]
    tc = groups * LANES
    return pl.pallas_call(
        functools.partial(_proj_t_kernel, groups=groups),
        grid=(s // tm, n // tc),
        in_specs=[
            pl.BlockSpec((tm, d), lambda i, j: (i, 0)),
            pl.BlockSpec((d, tc), lambda i, j: (0, j)),
            pl.BlockSpec((1, tc), lambda i, j: (0, j)),
        ],
        out_specs=pl.BlockSpec((groups, 1, LANES, tm), lambda i, j: (j, i, 0, 0)),
        out_shape=jax.ShapeDtypeStruct((n // LANES, s // tm, LANES, tm), BF16),
        compiler_params=_cparams(("arbitrary", "arbitrary")),
        name="proj_t",
    )(x, w, scale)


def _rope_tables(s, width):
    half = HEAD_DIM // 2
    inv = ROPE_THETA ** (-jnp.arange(half, dtype=F32) / half)
    ang = jnp.arange(s, dtype=F32)[:, None] * inv[None, :]
    cos = jnp.cos(ang)
    sin = jnp.sin(ang)
    reps = width // HEAD_DIM
    cos_t = jnp.tile(jnp.concatenate([cos, cos], axis=1), (1, reps))
    sin_t = jnp.tile(jnp.concatenate([-sin, sin], axis=1), (1, reps))
    return cos_t, sin_t


def _sb_kernel(q_ref, k_ref, v_ref, o_ref, acc_ref, run_ref, *, t, pp):
    i = pl.program_id(1)
    lane = _iota((t, LANES), 1)
    lo = lane < HEAD_DIM
    qh = []
    for p in range(pp):
        q = q_ref[:, p * LANES:(p + 1) * LANES]
        zero = jnp.zeros_like(q)
        qh.extend([jnp.where(lo, q, zero), jnp.where(lo, zero, q)])
    row = _iota((t, t), 0)
    col = _iota((t, t), 1)
    incl = (row >= col).astype(BF16)
    ones = jnp.ones((t, LANES), BF16)
    strict = col < row

    acc_ref[...] = jnp.zeros_like(acc_ref)
    run_ref[...] = jnp.zeros_like(run_ref)

    def tile(kt, diagonal):
        start = pl.multiple_of(kt * t, t)
        heads = range(2 * pp)
        sls = [slice((h // 2) * LANES, (h // 2 + 1) * LANES) for h in heads]
        zs = [_dot_nt(qh[h], k_ref[pl.ds(start, t), sls[h]]) for h in heads]
        ls16s = []
        for h in heads:
            log_stay = -(jnp.maximum(zs[h], 0.0) + jnp.log1p(jnp.exp(-jnp.abs(zs[h]))))
            if diagonal:
                log_stay = jnp.where(strict, log_stay, 0.0)
            ls16s.append(log_stay.astype(BF16))
        cums = [_dot(ls16s[h], incl) for h in heads]
        tots = [_dot(ls16s[h], ones) for h in heads]
        a16s = []
        for h in heads:
            a = jnp.exp(zs[h] + cums[h] + run_ref[h])
            if diagonal:
                a = jnp.where(strict, a, 0.0)
            a16s.append(a.astype(BF16))
        pvs = [_dot(a16s[h], v_ref[pl.ds(start, t), sls[h]]) for h in heads]
        for h in heads:
            acc_ref[h] += pvs[h]
            run_ref[h] += tots[h]

    tile(i, True)

    def cond(c):
        kt, live = c
        return jnp.logical_and(kt >= 0, live)

    def body(c):
        kt, _ = c
        tile(kt, False)
        live = jnp.max(run_ref[...]) > EXP_FLUSH
        return kt - 1, live

    lax.while_loop(cond, body, (i - 1, jnp.bool_(True)))
    for p in range(pp):
        o_ref[:, p * LANES:(p + 1) * LANES] = jnp.where(lo, acc_ref[2 * p], acc_ref[2 * p + 1]).astype(o_ref.dtype)


def _sb_attention(h0, *, q_col, k_col, v_col, n_pairs, t=128, pp=4):
    s = h0.shape[0]
    assert n_pairs % pp == 0 and q_col % pp == 0 and k_col % pp == 0 and v_col % pp == 0
    w = pp * LANES
    resident = pl.Buffered(1)
    return pl.pallas_call(
        functools.partial(_sb_kernel, t=t, pp=pp),
        grid=(n_pairs // pp, s // t),
        in_specs=[
            pl.BlockSpec((t, w), lambda p, i: (i, q_col // pp + p)),
            pl.BlockSpec((s, w), lambda p, i: (0, k_col // pp + p), pipeline_mode=resident),
            pl.BlockSpec((s, w), lambda p, i: (0, v_col // pp + p), pipeline_mode=resident),
        ],
        out_specs=pl.BlockSpec((t, w), lambda p, i: (i, p)),
        out_shape=jax.ShapeDtypeStruct((s, n_pairs * LANES), BF16),
        scratch_shapes=[pltpu.VMEM((2 * pp, t, LANES), F32), pltpu.VMEM((2 * pp, t, LANES), F32)],
        compiler_params=_cparams(("arbitrary", "arbitrary")),
        name="stick_breaking",
    )(h0, h0, h0)


def _moba_prep_kernel(k_ref, kaug_ref, kmean_ref):
    b = pl.program_id(1)
    k = k_ref[...]
    kmean_ref[...] = jnp.mean(k.astype(F32), axis=0, keepdims=True)[None]
    onehot = (_iota(k.shape, 1) == b).astype(BF16)
    kaug_ref[...] = jnp.concatenate([k, onehot], axis=1)


def _moba_prep(h0, *, k_col, n_pairs):
    s = h0.shape[0]
    nb = s // MOBA_BLOCK
    return pl.pallas_call(
        _moba_prep_kernel,
        grid=(n_pairs, nb),
        in_specs=[pl.BlockSpec((MOBA_BLOCK, LANES), lambda p, b: (b, k_col + p))],
        out_specs=[
            pl.BlockSpec((MOBA_BLOCK, 2 * LANES), lambda p, b: (b, p)),
            pl.BlockSpec((1, 1, LANES), lambda p, b: (p * nb + b, 0, 0)),
        ],
        out_shape=[
            jax.ShapeDtypeStruct((s, n_pairs * 2 * LANES), BF16),
            jax.ShapeDtypeStruct((n_pairs * nb, 1, LANES), F32),
        ],
        compiler_params=_cparams(("arbitrary", "arbitrary")),
        name="moba_prep",
    )(h0)


def _moba_kernel(q_ref, kaug_ref, vt_ref, kmean_ref, o_ref, m_ref, acc_ref, st_ref, *, nb, g):
    t = MOBA_BLOCK
    i = pl.program_id(1)
    qf = q_ref[...].astype(F32)
    lo = _iota((t, LANES), 1) < HEAD_DIM
    qt = jnp.concatenate([jnp.where(lo, qf, 0.0).T, jnp.where(lo, 0.0, qf).T], axis=1).astype(BF16)
    kmean = kmean_ref[...]
    km_hi = kmean.astype(BF16)
    km_lo = (kmean - km_hi.astype(F32)).astype(BF16)
    gate = _dot(km_hi, qt) + _dot(km_lo, qt)
    blk = _iota((LANES, 2 * t), 0)
    blk_f = blk.astype(F32)
    gate = jnp.where(blk < i, gate, -jnp.inf)
    chosen = blk == i
    for _ in range(MOBA_TOPK):
        m = jnp.max(gate, axis=0, keepdims=True)
        first = jnp.min(jnp.where(gate == m, blk_f, float(LANES)), axis=0, keepdims=True)
        pick = blk_f == first
        chosen = chosen | (pick & (m > -jnp.inf))
        gate = jnp.where(pick, -jnp.inf, gate)
    bias = jnp.where(chosen | (blk >= nb), 0.0, NEG).astype(BF16)
    q_both = jnp.concatenate([qt, bias], axis=0)

    tk = g * t
    last = lax.shift_right_logical(i, int(np.log2(g)))
    ones = jnp.ones((ONES_ROWS, tk), BF16)

    def scores(u):
        start = pl.multiple_of(u * tk, tk)
        return _dot(kaug_ref[pl.ds(start, tk), :], q_both)

    m_ref[...] = jnp.full(m_ref.shape, NEG, F32)
    acc_ref[...] = jnp.zeros_like(acc_ref)
    key_pos = _iota((tk, 2 * t), 0) + (last * g - i) * t
    query_pos = _iota((tk, 2 * t), 1) & (t - 1)
    st_ref[...] = jnp.where(key_pos <= query_pos, scores(last), NEG)

    def body(n, carry):
        cur = jnp.where(n == 0, last, n - 1)
        nxt = jnp.minimum(n, jnp.maximum(last - 1, 0))
        st_next = scores(nxt)
        vt = jnp.concatenate([vt_ref[0, cur * g + b] for b in range(g)], axis=1)
        vt = jnp.concatenate([vt, ones], axis=0)
        st = st_ref[...]
        m_old = m_ref[...]
        m_new = jnp.maximum(m_old, jnp.max(st, axis=0, keepdims=True))
        alpha = jnp.exp2(m_old - m_new)
        pt = jnp.exp2(st - m_new).astype(BF16)
        acc_ref[...] = alpha * acc_ref[...] + _dot(vt, pt)
        m_ref[...] = m_new
        st_ref[...] = st_next
        return carry

    lax.fori_loop(0, last + 1, body, 0)
    acc = acc_ref[...]
    outs = [acc[:LANES, h * t:(h + 1) * t] / acc[LANES:LANES + 1, h * t:(h + 1) * t] for h in range(2)]
    first = _iota((LANES, t), 0) < HEAD_DIM
    o_ref[...] = jnp.where(first, outs[0], outs[1]).T.astype(o_ref.dtype)


def _moba_attention(h0, kaug, vt, kmean, *, q_col, n_pairs, g=4):
    s = h0.shape[0]
    t = MOBA_BLOCK
    nb = s // t
    g = min(g, nb)
    assert nb <= LANES and nb % g == 0
    return pl.pallas_call(
        functools.partial(_moba_kernel, nb=nb, g=g),
        grid=(n_pairs, nb),
        in_specs=[
            pl.BlockSpec((t, LANES), lambda p, i: (i, q_col + p)),
            pl.BlockSpec((s, 2 * LANES), lambda p, i: (0, p)),
            pl.BlockSpec((1, nb, LANES, t), lambda p, i: (p, 0, 0, 0)),
            pl.BlockSpec((LANES, LANES), lambda p, i: (p, 0)),
        ],
        out_specs=pl.BlockSpec((t, LANES), lambda p, i: (i, p)),
        out_shape=jax.ShapeDtypeStruct((s, n_pairs * LANES), BF16),
        scratch_shapes=[pltpu.VMEM((1, 2 * t), F32),
                        pltpu.VMEM((LANES + ONES_ROWS, 2 * t), F32),
                        pltpu.VMEM((g * t, 2 * t), F32)],
        compiler_params=_cparams(("arbitrary", "arbitrary")),
        name="moba",
    )(h0, kaug, vt, kmean)


def _dsa_kernel(it_ref, jt_ref, qi_ref, wi_ref, ki_ref, q_ref, k_ref, vt_ref, o_ref,
                key_ref, half_ref, thr_ref, need_ref, seen_ref, m_ref, acc_ref, qt_ref,
                *, t, g, n_heads, topk):
    step = pl.program_id(0)
    i = it_ref[step]
    j = jt_ref[step]
    lane = _iota((t, LANES), 1)
    lo = lane < HEAD_DIM
    key_le_query = _iota((t, t), 0) <= _iota((t, t), 1)
    n_pairs = n_heads // 2

    def split_t(pair):
        pf = pair.astype(F32)
        return (jnp.where(lo, pf, 0.0).T.astype(BF16), jnp.where(lo, 0.0, pf).T.astype(BF16))

    @pl.when(j == 0)
    def _select():
        qi = qi_ref[...]
        wi = wi_ref[0, 0].astype(F32)
        qit = [jnp.concatenate(split_t(qi[:, p * LANES:(p + 1) * LANES]), axis=1)
               for p in range(IDX_HEADS // 2)]
        q = q_ref[...]
        for p in range(n_pairs):
            qt_ref[p] = jnp.concatenate(split_t(q[:, p * LANES:(p + 1) * LANES]), axis=1)

        def score_tile(kt, carry):
            start = pl.multiple_of(kt * t, t)
            kblk = ki_ref[pl.ds(start, t), :]
            logits = [_dot(kblk, qit[p]) for p in range(IDX_HEADS // 2)]
            sc = jnp.zeros((t, t), F32)
            for h in range(IDX_HEADS):
                lg = logits[h // 2][:, (h % 2) * t:(h % 2 + 1) * t]
                sc = sc + wi[h:h + 1, :] * jnp.maximum(lg, 0.0)
            bits = pltpu.bitcast(sc, I32)
            key = jnp.where(bits >= 0, bits, bits ^ jnp.int32(0x7FFFFFFF))
            key = jnp.where(jnp.logical_or(kt < i, key_le_query), key, jnp.int32(INT_MIN))
            key_ref[kt] = key
            half_ref[kt] = lax.shift_right_arithmetic(key, jnp.int32(16)).astype(I16)
            return carry

        lax.fori_loop(0, i + 1, score_tile, 0)
        for b in range(1, g + 1):
            key_ref[i + b] = jnp.full((t, t), INT_MIN, I32)
        for b in range(1, COUNT_UNROLL):
            half_ref[i + b] = jnp.full((t, t), I16_MIN, I16)
        n_trips = (i + COUNT_UNROLL) // COUNT_UNROLL

        def count(pred):
            def cbody(trip, c):
                for kt in [COUNT_UNROLL * trip + u for u in range(COUNT_UNROLL)]:
                    hit = jnp.where(pred(half_ref[kt]), jnp.int16(1), jnp.int16(0))
                    for r in range(t // 16):
                        c = c + hit[r * 16:(r + 1) * 16]
                return c
            c = lax.fori_loop(0, n_trips, cbody, jnp.zeros((16, t), I16))
            return jnp.sum(c.astype(F32), axis=0, keepdims=True)

        def search(base):
            v = jnp.zeros((1, t), I32)
            for b in range(15, -1, -1):
                cand = v | jnp.int32(1 << b)
                cand16 = (cand + I16_MIN).astype(I16)
                cnt = base + count(lambda hh, c=cand16: hh >= c)
                v = jnp.where(cnt >= float(topk), cand, v)
            return v

        hi = search(0.0) + I16_MIN
        hi16 = hi.astype(I16)
        n_above = count(lambda hh: hh > hi16)

        def narrow(kt, carry):
            low = (key_ref[kt] & jnp.int32(0xFFFF)) + I16_MIN
            half_ref[kt] = jnp.where(half_ref[kt] == hi16, low.astype(I16), jnp.int16(I16_MIN))
            return carry

        lax.fori_loop(0, i + 1, narrow, 0)
        lo = search(n_above)
        lo16 = (lo + I16_MIN).astype(I16)
        n_gt = n_above + count(lambda hh: hh > lo16)
        thr_ref[...] = hi * 65536 + lo
        need_ref[...] = float(topk) - n_gt
        seen_ref[...] = jnp.zeros_like(seen_ref)
        m_ref[...] = jnp.full(m_ref.shape, NEG, F32)
        acc_ref[...] = jnp.zeros_like(acc_ref)

    thr = thr_ref[...]
    need = need_ref[...]
    earlier = (_iota((t, t), 1) < _iota((t, t), 0)).astype(BF16)
    seen = seen_ref[...]
    biases = []
    for b in range(g):
        blk = j * g + b
        key = key_ref[blk]
        eq = key == thr
        rank = seen + _dot(earlier, jnp.where(eq, 1.0, 0.0).astype(BF16))
        sel = (key > thr) | (eq & (rank < need))
        sel = sel & jnp.logical_or(blk < i, jnp.logical_and(blk == i, key_le_query))
        seen = seen + jnp.sum(jnp.where(eq, 1.0, 0.0), axis=0, keepdims=True)
        biases.append(jnp.where(sel, 0.0, NEG))
    seen_ref[...] = seen
    bias = jnp.concatenate(biases, axis=0)
    bias = jnp.concatenate([bias, bias], axis=1)

    kb = k_ref[...]
    ones = jnp.ones((ONES_ROWS, g * t), BF16)

    def scores(p):
        return _dot(kb[:, p * LANES:(p + 1) * LANES], qt_ref[p]) + bias

    def softmax(p, st):
        m_old = m_ref[p]
        m_new = jnp.maximum(m_old, jnp.max(st, axis=0, keepdims=True))
        m_ref[p] = m_new
        return jnp.exp2(m_old - m_new), jnp.exp2(st - m_new).astype(BF16)

    def accumulate(p, alpha, pt):
        vt = jnp.concatenate([vt_ref[p, b] for b in range(g)], axis=1)
        vt = jnp.concatenate([vt, ones], axis=0)
        acc_ref[p] = alpha * acc_ref[p] + _dot(vt, pt)

    st_next = scores(0)
    for p in range(n_pairs):
        st = st_next
        if p + 1 < n_pairs:
            st_next = scores(p + 1)
        accumulate(p, *softmax(p, st))

    @pl.when(j == lax.shift_right_logical(i, int(np.log2(g))))
    def _finish():
        first = _iota((LANES, t), 0) < HEAD_DIM
        for p in range(n_pairs):
            acc = acc_ref[p]
            a0, a1 = acc[:, :t], acc[:, t:]
            out_t = jnp.where(first, a0[:LANES] / a0[LANES:LANES + 1], a1[:LANES] / a1[LANES:LANES + 1])
            o_ref[:, p * LANES:(p + 1) * LANES] = out_t.T.astype(o_ref.dtype)


def _dsa_attention(h1, vt, *, n_heads, t=256, g=4):
    s = h1.shape[0]
    width = n_heads * HEAD_DIM
    nt = s // t
    g = min(g, nt)
    assert nt % g == 0
    topk = min(DSA_TOPK, s // 4)
    iw = IDX_HEADS * HEAD_DIM
    qi_blk = 2 * width // iw
    ki_blk = (2 * width + iw) // LANES
    wi_grp = width // LANES
    tri = [(i, j) for i in range(nt) for j in range(i // g + 1)]
    it = jnp.asarray([a for a, _ in tri], I32)
    jt = jnp.asarray([b for _, b in tri], I32)
    grid_spec = pltpu.PrefetchScalarGridSpec(
        num_scalar_prefetch=2,
        grid=(len(tri),),
        in_specs=[
            pl.BlockSpec((t, iw), lambda n, it, jt: (it[n], qi_blk)),
            pl.BlockSpec((1, 1, ONES_ROWS, t), lambda n, it, jt: (wi_grp, it[n], 0, 0)),
            pl.BlockSpec((s, LANES), lambda n, it, jt: (0, ki_blk)),
            pl.BlockSpec((t, width), lambda n, it, jt: (it[n], 0)),
            pl.BlockSpec((g * t, width), lambda n, it, jt: (jt[n], 1)),
            pl.BlockSpec((n_heads // 2, g, LANES, t), lambda n, it, jt: (0, jt[n], 0, 0)),
        ],
        out_specs=pl.BlockSpec((t, width), lambda n, it, jt: (it[n], 0)),
        scratch_shapes=[
            pltpu.VMEM((nt + g, t, t), I32),
            pltpu.VMEM((nt + COUNT_UNROLL - 1, t, t), I16),
            pltpu.VMEM((1, t), I32),
            pltpu.VMEM((1, t), F32),
            pltpu.VMEM((1, t), F32),
            pltpu.VMEM((n_heads // 2, 1, 2 * t), F32),
            pltpu.VMEM((n_heads // 2, LANES + ONES_ROWS, 2 * t), F32),
            pltpu.VMEM((n_heads // 2, LANES, 2 * t), BF16),
        ],
    )
    return pl.pallas_call(
        functools.partial(_dsa_kernel, t=t, g=g, n_heads=n_heads, topk=topk),
        grid_spec=grid_spec,
        out_shape=jax.ShapeDtypeStruct((s, width), BF16),
        compiler_params=_cparams(("arbitrary",)),
        name="dsa",
    )(it, jt, h1, vt, h1, h1, h1, vt)


def _layer_norm(y, g, b):
    mu = jnp.mean(y, axis=-1, keepdims=True)
    yc = y - mu
    var = jnp.mean(yc * yc, axis=-1, keepdims=True)
    return yc * lax.rsqrt(var + LN_EPS) * g + b


def _outproj_kernel(*refs, n_in, alpha):
    o_refs = refs[:n_in]
    w_refs = refs[n_in:2 * n_in]
    x_ref, g_ref, b_ref, out_ref = refs[2 * n_in:]
    y = alpha * x_ref[...]
    for o_ref, w_ref in zip(o_refs, w_refs):
        y = y + _dot(o_ref[...], w_ref[...])
    out_ref[...] = _layer_norm(y, g_ref[...], b_ref[...])


def _outproj_ln(os_, ws, x, g, b, alpha, *, tm=256):
    s, d = x.shape
    n_in = len(os_)
    in_specs = ([pl.BlockSpec((tm, o.shape[1]), lambda i: (i, 0)) for o in os_]
                + [pl.BlockSpec(w.shape, lambda i: (0, 0)) for w in ws]
                + [pl.BlockSpec((tm, d), lambda i: (i, 0)),
                   pl.BlockSpec((1, d), lambda i: (0, 0)),
                   pl.BlockSpec((1, d), lambda i: (0, 0))])
    return pl.pallas_call(
        functools.partial(_outproj_kernel, n_in=n_in, alpha=alpha),
        grid=(s // tm,),
        in_specs=in_specs,
        out_specs=pl.BlockSpec((tm, d), lambda i: (i, 0)),
        out_shape=jax.ShapeDtypeStruct((s, d), F32),
        compiler_params=_cparams(("arbitrary",)),
        name="out_proj_ln",
    )(*os_, *ws, x, g, b)


def _moe_kernel(x_ref, wr_ref, br_ref, w1g_ref, w1u_ref, b1g_ref, b1u_ref, w2_ref, b2_ref,
                g_ref, b_ref, out_ref,
                xb_ref, rank_ref, rank_t_ref, gate_t_ref, cnt_ref, acc_ref,
                *, tt, ch, alpha):
    e = pl.program_id(1)
    n_e = pl.num_programs(1)

    @pl.when(e == 0)
    def _route():
        x = x_ref[...]
        xb_ref[...] = x.astype(BF16)
        logits = lax.dot_general(wr_ref[...], x, (((1,), (1,)), ((), ())),
                                 precision=lax.Precision.HIGHEST,
                                 preferred_element_type=F32) + br_ref[...]
        row = _iota(logits.shape, 0).astype(F32)
        work = jnp.where(row < N_EXPERTS, logits, -jnp.inf)
        chosen = row < 0
        top = None
        for r in range(TOP_K):
            m = jnp.max(work, axis=0, keepdims=True)
            idx = jnp.min(jnp.where(work == m, row, float(LANES)), axis=0, keepdims=True)
            pick = row == idx
            chosen = chosen | pick
            work = jnp.where(pick, -jnp.inf, work)
            if r == 0:
                top = m
        ex = jnp.where(chosen, jnp.exp(logits - top), 0.0)
        gate = ex / jnp.sum(ex, axis=0, keepdims=True)
        sel16 = jnp.where(chosen, 1.0, 0.0).astype(BF16)
        before = (_iota((tt, tt), 0) < _iota((tt, tt), 1)).astype(BF16)
        pos = _dot(sel16, before)
        rank = jnp.where(chosen, pos, -1.0)
        rank_ref[...] = rank
        rank_t_ref[...] = rank.T
        gate_t_ref[...] = gate.T
        cnt = jnp.sum(jnp.where(chosen, 1.0, 0.0), axis=1, keepdims=True)
        cnt_ref[...] = jnp.broadcast_to(cnt, cnt_ref.shape)
        acc_ref[...] = jnp.zeros_like(acc_ref)

    lane = _iota((tt, LANES), 1)
    on_e = lane == e
    rank_col = jnp.sum(jnp.where(on_e, rank_t_ref[...], 0.0), axis=1, keepdims=True)
    gate_col = jnp.sum(jnp.where(on_e, gate_t_ref[...], 0.0), axis=1, keepdims=True)
    rank_row = rank_ref[pl.ds(e, 1), :]
    n_rows = jnp.max(cnt_ref[pl.ds(e, 1), :]).astype(I32)
    n_chunks = (n_rows + (ch - 1)) // ch
    pad_rows = jnp.zeros((2 * LANES - ch, x_ref.shape[1]), BF16)
    slot = _iota((1, 2 * LANES), 1)
    slot = jnp.where(slot < ch, slot.astype(F32), -1e9)
    b1g = b1g_ref[0]
    b1u = b1u_ref[0]
    b2 = b2_ref[0]

    def chunk(c, carry):
        w1g = w1g_ref[0]
        w1u = w1u_ref[0]
        w2 = w2_ref[0]
        r0 = (c * ch).astype(F32)
        gather = (rank_row == (r0 + _iota((ch, tt), 0).astype(F32))).astype(BF16)
        xg = _dot(gather, xb_ref[...]).astype(BF16)
        hg = jnp.minimum(_dot(xg, w1g) + b1g, SWIGLU_LIMIT)
        hu = jnp.clip(_dot(xg, w1u) + b1u, -SWIGLU_LIMIT, SWIGLU_LIMIT)
        act = (hu + 1.0) * (hg * jax.nn.sigmoid(SWIGLU_ALPHA * hg))
        y = _dot(act.astype(BF16), w2) + b2
        scatter = jnp.where(rank_col == (r0 + slot), gate_col, 0.0).astype(BF16)
        acc_ref[...] += _dot(scatter, jnp.concatenate([y.astype(BF16), pad_rows], axis=0))
        return carry

    lax.fori_loop(0, n_chunks, chunk, 0)

    @pl.when(e == n_e - 1)
    def _finish():
        out_ref[...] = _layer_norm(alpha * x_ref[...] + acc_ref[...], g_ref[...], b_ref[...])


def _moe_ln(x, wr_t, br, w1g, w1u, b1g, b1u, w2, b2, g, b, alpha, *, tt=1024, ch=160):
    t, d = x.shape
    tt = min(tt, t)
    assert ch % 16 == 0 and ch <= 2 * LANES
    f = w2.shape[1]
    return pl.pallas_call(
        functools.partial(_moe_kernel, tt=tt, ch=ch, alpha=alpha),
        grid=(t // tt, N_EXPERTS),
        in_specs=[
            pl.BlockSpec((tt, d), lambda i, e: (i, 0)),
            pl.BlockSpec((LANES, d), lambda i, e: (0, 0)),
            pl.BlockSpec((LANES, 1), lambda i, e: (0, 0)),
            pl.BlockSpec((1, d, f), lambda i, e: (e, 0, 0)),
            pl.BlockSpec((1, d, f), lambda i, e: (e, 0, 0)),
            pl.BlockSpec((1, 1, f), lambda i, e: (e, 0, 0)),
            pl.BlockSpec((1, 1, f), lambda i, e: (e, 0, 0)),
            pl.BlockSpec((1, f, d), lambda i, e: (e, 0, 0)),
            pl.BlockSpec((1, 1, d), lambda i, e: (e, 0, 0)),
            pl.BlockSpec((1, d), lambda i, e: (0, 0)),
            pl.BlockSpec((1, d), lambda i, e: (0, 0)),
        ],
        out_specs=pl.BlockSpec((tt, d), lambda i, e: (i, 0)),
        out_shape=jax.ShapeDtypeStruct((t, d), F32),
        scratch_shapes=[
            pltpu.VMEM((tt, d), BF16),
            pltpu.VMEM((LANES, tt), F32),
            pltpu.VMEM((tt, LANES), F32),
            pltpu.VMEM((tt, LANES), F32),
            pltpu.VMEM((LANES, LANES), F32),
            pltpu.VMEM((tt, d), F32),
        ],
        compiler_params=_cparams(("arbitrary", "arbitrary")),
        name="moe_ln",
    )(x, wr_t, br, w1g, w1u, b1g, b1u, w2, b2, g, b)


def _ple_kernel(x_ref, p_ref, wg_ref, wp_ref, o_ref):
    x = x_ref[...]
    gate = jax.nn.sigmoid(_dot(x.astype(BF16), wg_ref[...]))
    emb = _dot(p_ref[...].astype(BF16), wp_ref[...])
    o_ref[...] = x + gate * emb


def _ple(x, p, wg, wp, *, tm=512):
    s, d = x.shape
    dp = p.shape[1]
    return pl.pallas_call(
        _ple_kernel,
        grid=(s // tm,),
        in_specs=[
            pl.BlockSpec((tm, d), lambda i: (i, 0)),
            pl.BlockSpec((tm, dp), lambda i: (i, 0)),
            pl.BlockSpec((d, d), lambda i: (0, 0)),
            pl.BlockSpec((dp, d), lambda i: (0, 0)),
        ],
        out_specs=pl.BlockSpec((tm, d), lambda i: (i, 0)),
        out_shape=jax.ShapeDtypeStruct((s, d), F32),
        compiler_params=_cparams(("arbitrary",)),
        name="ple",
    )(x, p, wg, wp)


def _mixer_ab(x, w_in, cos_t, sin_t, *, sb_heads, moba_heads):
    s = x.shape[0]
    wa = sb_heads * HEAD_DIM
    wb = moba_heads * HEAD_DIM
    qa, ka, va, qb, kb, vb = jnp.split(w_in, np.cumsum([wa, wa, wa, wb, wb]).tolist(), axis=1)
    w = jnp.concatenate([qb, kb, qa, ka, va], axis=1).astype(BF16)
    att = HEAD_DIM ** -0.5
    scale = jnp.concatenate([jnp.full((wb,), att * LOG2E), jnp.ones((wb,)), jnp.full((wa,), att),
                             jnp.ones((2 * wa,))]).astype(F32)[None]
    tn = cos_t.shape[1]
    h0 = _proj(x, w, scale, cos_t, sin_t, ((0, 2 * wb // tn),), tn=tn)
    n_pairs = moba_heads // 2
    vt = _proj_t(x, vb.astype(BF16), jnp.ones((1, wb), F32), groups=n_pairs)
    c = lambda off: off // LANES
    o_a = _sb_attention(h0, q_col=c(2 * wb), k_col=c(2 * wb + wa), v_col=c(2 * wb + 2 * wa),
                        n_pairs=sb_heads // 2)
    nb = s // MOBA_BLOCK
    kaug, kmean = _moba_prep(h0, k_col=c(wb), n_pairs=n_pairs)
    kmean = jnp.pad(kmean.reshape(n_pairs, nb, LANES), ((0, 0), (0, LANES - nb), (0, 0)))
    kmean = kmean.reshape(n_pairs * LANES, LANES)
    o_b = _moba_attention(h0, kaug, vt, kmean, q_col=0, n_pairs=n_pairs)
    return o_a, o_b


def _mixer_c(x, w_in, cos_t, sin_t, *, n_heads):
    width = n_heads * HEAD_DIM
    iw = IDX_HEADS * HEAD_DIM
    d = w_in.shape[0]
    q, k, v, qi, ki, wi = jnp.split(w_in, np.cumsum([width, width, width, iw, HEAD_DIM]).tolist(), axis=1)
    tn = cos_t.shape[1]
    ki_pad = jnp.zeros((d, tn - 2 * HEAD_DIM), w_in.dtype)
    w = jnp.concatenate([q, k, qi, ki, ki, ki_pad], axis=1).astype(BF16)
    att = HEAD_DIM ** -0.5
    scale = jnp.concatenate([jnp.full((width,), att * LOG2E), jnp.ones((width,)), jnp.full((iw,), att),
                             jnp.ones((tn,))]).astype(F32)[None]
    h1 = _proj(x, w, scale, cos_t, sin_t, ((0, (2 * width + iw) // tn + 1),), tn=tn)
    wi_pad = jnp.zeros((d, LANES - IDX_HEADS), w_in.dtype)
    wv = jnp.concatenate([v, wi, wi_pad], axis=1).astype(BF16)
    vscale = jnp.concatenate([jnp.ones((width,)), jnp.full((LANES,), IDX_HEADS ** -0.5)]).astype(F32)[None]
    vt = _proj_t(x, wv, vscale, groups=3)
    return _dsa_attention(h1, vt, n_heads=n_heads)


def _split_kernel(w_ref, perm_ref, g_ref, u_ref):
    half = g_ref.shape[2]
    both = _dot(w_ref[0, 0].astype(BF16), perm_ref[...])
    g_ref[0] = both[:, :half].astype(BF16)
    u_ref[0] = both[:, half:].astype(BF16)


def _split_gate_up(w1_all, layer, *, tc=512):
    _, n_e, d, f2 = w1_all.shape
    half = tc // 2
    src = jnp.concatenate([2 * jnp.arange(half), 2 * jnp.arange(half) + 1])
    perm = (jnp.arange(tc)[:, None] == src[None, :]).astype(BF16)
    out = jax.ShapeDtypeStruct((n_e, d, f2 // 2), BF16)
    return pl.pallas_call(
        _split_kernel,
        grid=(n_e, f2 // tc),
        in_specs=[pl.BlockSpec((1, 1, d, tc), lambda e, c: (layer, e, 0, c)),
                  pl.BlockSpec((tc, tc), lambda e, c: (0, 0))],
        out_specs=[pl.BlockSpec((1, d, half), lambda e, c: (e, 0, c)),
                   pl.BlockSpec((1, d, half), lambda e, c: (e, 0, c))],
        out_shape=[out, out],
        compiler_params=_cparams(("arbitrary", "arbitrary")),
        name="split_gate_up",
    )(w1_all, perm)


def _cast_kernel(w_ref, o_ref):
    o_ref[0] = w_ref[0, 0].astype(o_ref.dtype)


def _cast_layer(w_all, layer):
    _, n_e, f, d = w_all.shape
    return pl.pallas_call(
        _cast_kernel,
        grid=(n_e,),
        in_specs=[pl.BlockSpec((1, 1, f, d), lambda e: (layer, e, 0, 0))],
        out_specs=pl.BlockSpec((1, f, d), lambda e: (e, 0, 0)),
        out_shape=jax.ShapeDtypeStruct((n_e, f, d), BF16),
        compiler_params=_cparams(("arbitrary",)),
        name="cast_w2",
    )(w_all)


def _moe_weights(w_r, b_r, w1_all, b1, w2_all, b2, layer):
    wr_t = jnp.pad(w_r.T, ((0, LANES - N_EXPERTS), (0, 0)))
    br = jnp.pad(b_r, (0, LANES - N_EXPERTS))[:, None]
    w1g, w1u = _split_gate_up(w1_all, layer)
    return (wr_t, br, w1g, w1u, b1[:, None, 0::2], b1[:, None, 1::2], _cast_layer(w2_all, layer),
            b2[:, None, :])


def kernel(x, p, ab_w_in, ab_w_out, c_w_in, c_w_out, ln_g, ln_b, router_w, router_b,
           moe_w1, moe_b1, moe_w2, moe_b2, ple_w_proj, ple_w_gate):
    b, s, d = x.shape
    assert b == 1
    depth = p.shape[0]
    alpha = float((2 * depth) ** 0.25)
    cos_t, sin_t = _rope_tables(s, 2 * LANES)
    sb_heads = moba_heads = ab_w_out.shape[1] // (2 * HEAD_DIM)
    dsa_heads = c_w_out.shape[1] // HEAD_DIM
    xs = x[0]
    for i in range(depth):
        j = i // 2
        g = ln_g[i][:, None, :]
        bb = ln_b[i][:, None, :]
        if i % 2 == 0:
            o_a, o_b = _mixer_ab(xs, ab_w_in[j], cos_t, sin_t, sb_heads=sb_heads, moba_heads=moba_heads)
            w_out = ab_w_out[j].astype(BF16)
            wa = sb_heads * HEAD_DIM
            xs = _outproj_ln([o_a, o_b], [w_out[:wa], w_out[wa:]], xs, g[0], bb[0], alpha)
        else:
            o_c = _mixer_c(xs, c_w_in[j], cos_t, sin_t, n_heads=dsa_heads)
            xs = _outproj_ln([o_c], [c_w_out[j].astype(BF16)], xs, g[0], bb[0], alpha)
        mw = _moe_weights(router_w[i], router_b[i], moe_w1, moe_b1[i], moe_w2, moe_b2[i], i)
        xs = _moe_ln(xs, *mw, g[1], bb[1], alpha)
        xs = _ple(xs, p[i, 0], ple_w_gate[i].astype(BF16), ple_w_proj[i].astype(BF16))
    return xs[None]
```

```python
import functools

import jax
import jax.numpy as jnp
import numpy as np
from jax import lax
from jax.experimental import pallas as pl
from jax.experimental.pallas import tpu as pltpu

F32 = jnp.float32
BF16 = jnp.bfloat16
I32 = jnp.int32
I16 = jnp.int16

HEAD_DIM = 64
LANES = 128
MOBA_BLOCK = 256
MOBA_TOPK = 3
DSA_TOPK = 256
IDX_HEADS = 8
N_EXPERTS = 32
TOP_K = 4
SWIGLU_LIMIT = 7.0
SWIGLU_ALPHA = 1.702
ROPE_THETA = 10000.0
LN_EPS = 1e-5
NEG = -1e30
EXP_FLUSH = -88.0
INT_MIN = -(2 ** 31)
I16_MIN = -(2 ** 15)
COUNT_UNROLL = 4
LOG2E = 1.4426950408889634
ONES_ROWS = 16
VMEM_LIMIT = 56 * 1024 * 1024


def _cparams(sem, flags=None):
    return pltpu.CompilerParams(dimension_semantics=sem, vmem_limit_bytes=VMEM_LIMIT, flags=flags)


def _dot(a, b):
    return jnp.dot(a, b, preferred_element_type=F32)


def _dot_nt(a, b):
    return lax.dot_general(a, b, (((1,), (1,)), ((), ())), preferred_element_type=F32)


def _dot_tn(a, b):
    return lax.dot_general(a, b, (((0,), (0,)), ((), ())), preferred_element_type=F32)


def _iota(shape, dim):
    return lax.broadcasted_iota(I32, shape, dim)


def _proj_kernel(x_ref, w_ref, scale_ref, cos_ref, sin_ref, o_ref, xb_ref, *, rope_ranges, tn):
    j = pl.program_id(1)

    @pl.when(j == 0)
    def _():
        xb_ref[...] = x_ref[...].astype(BF16)

    h = _dot(xb_ref[...], w_ref[...]) * scale_ref[...]
    is_rope = functools.reduce(jnp.logical_or, [(j >= a) & (j < b) for a, b in rope_ranges])

    @pl.when(is_rope)
    def _():
        for c in range(tn // LANES):
            sl = slice(c * LANES, (c + 1) * LANES)
            hs = h[:, sl]
            partner = pltpu.roll(hs, LANES // 2, 1)
            o_ref[:, sl] = (hs * cos_ref[:, sl] + partner * sin_ref[:, sl]).astype(o_ref.dtype)

    @pl.when(jnp.logical_not(is_rope))
    def _():
        o_ref[...] = h.astype(o_ref.dtype)


def _proj(x, w, scale, cos_t, sin_t, rope_ranges, *, tm=512, tn=256):
    s, d = x.shape
    n = w.shape[1]
    return pl.pallas_call(
        functools.partial(_proj_kernel, rope_ranges=rope_ranges, tn=tn),
        grid=(s // tm, n // tn),
        in_specs=[
            pl.BlockSpec((tm, d), lambda i, j: (i, 0)),
            pl.BlockSpec((d, tn), lambda i, j: (0, j)),
            pl.BlockSpec((1, tn), lambda i, j: (0, j)),
            pl.BlockSpec((tm, tn), lambda i, j: (i, 0)),
            pl.BlockSpec((tm, tn), lambda i, j: (i, 0)),
        ],
        out_specs=pl.BlockSpec((tm, tn), lambda i, j: (i, j)),
        out_shape=jax.ShapeDtypeStruct((s, n), BF16),
        scratch_shapes=[pltpu.VMEM((tm, d), BF16)],
        compiler_params=_cparams(("arbitrary", "arbitrary")),
        name="in_proj",
    )(x, w, scale, cos_t, sin_t)


def _proj_t_kernel(x_ref, w_ref, scale_ref, o_ref, *, groups):
    h = _dot(x_ref[...].astype(BF16), w_ref[...]) * scale_ref[...]
    for g in range(groups):
        o_ref[g, 0] = h[:, g * LANES:(g + 1) * LANES].T.astype(o_ref.dtype)


def _proj_t(x, w, scale, *, groups, tm=256):
    s, d = x.shape
    n = w.shape[1]
    tc = groups * LANES
    return pl.pallas_call(
        functools.partial(_proj_t_kernel, groups=groups),
        grid=(s // tm, n // tc),
        in_specs=[
            pl.BlockSpec((tm, d), lambda i, j: (i, 0)),
            pl.BlockSpec((d, tc), lambda i, j: (0, j)),
            pl.BlockSpec((1, tc), lambda i, j: (0, j)),
        ],
        out_specs=pl.BlockSpec((groups, 1, LANES, tm), lambda i, j: (j, i, 0, 0)),
        out_shape=jax.ShapeDtypeStruct((n // LANES, s // tm, LANES, tm), BF16),
        compiler_params=_cparams(("arbitrary", "arbitrary")),
        name="proj_t",
    )(x, w, scale)


def _rope_tables(s, width):
    half = HEAD_DIM // 2
    inv = ROPE_THETA ** (-jnp.arange(half, dtype=F32) / half)
    ang = jnp.arange(s, dtype=F32)[:, None] * inv[None, :]
    cos = jnp.cos(ang)
    sin = jnp.sin(ang)
    reps = width // LANES
    cos_t = jnp.tile(jnp.concatenate([cos, cos, cos, cos], axis=1), (1, reps))
    sin_t = jnp.tile(jnp.concatenate([-sin, -sin, sin, sin], axis=1), (1, reps))
    return cos_t, sin_t


def _rotary_layout(n_cols):
    half = HEAD_DIM // 2
    lane = np.arange(n_cols)
    base = lane // LANES * LANES
    within = lane % LANES
    head = within // half % 2
    part = within // HEAD_DIM
    return base + head * HEAD_DIM + part * half + within % half


def _head_a(lane):
    return (lane % HEAD_DIM) < (HEAD_DIM // 2)


def _sb_kernel(q_ref, k_ref, v_ref, o_ref, acc_ref, run_ref, *, t, pp):
    i = pl.program_id(1)
    lane = _iota((t, LANES), 1)
    lo = lane < HEAD_DIM
    qh = []
    for p in range(pp):
        q = q_ref[:, p * LANES:(p + 1) * LANES]
        zero = jnp.zeros_like(q)
        qh.extend([jnp.where(lo, q, zero), jnp.where(lo, zero, q)])
    row = _iota((t, t), 0)
    col = _iota((t, t), 1)
    incl = (row >= col).astype(BF16)
    ones = jnp.ones((t, LANES), BF16)
    strict = col < row

    acc_ref[...] = jnp.zeros_like(acc_ref)
    run_ref[...] = jnp.zeros_like(run_ref)

    def tile(kt, diagonal):
        start = pl.multiple_of(kt * t, t)
        heads = range(2 * pp)
        sls = [slice((h // 2) * LANES, (h // 2 + 1) * LANES) for h in heads]
        zs = [_dot_nt(qh[h], k_ref[pl.ds(start, t), sls[h]]) for h in heads]
        ls16s = []
        for h in heads:
            log_stay = -(jnp.maximum(zs[h], 0.0) + jnp.log1p(jnp.exp(-jnp.abs(zs[h]))))
            if diagonal:
                log_stay = jnp.where(strict, log_stay, 0.0)
            ls16s.append(log_stay.astype(BF16))
        cums = [_dot(ls16s[h], incl) for h in heads]
        tots = [_dot(ls16s[h], ones) for h in heads]
        a16s = []
        for h in heads:
            a = jnp.exp(zs[h] + cums[h] + run_ref[h])
            if diagonal:
                a = jnp.where(strict, a, 0.0)
            a16s.append(a.astype(BF16))
        pvs = [_dot(a16s[h], v_ref[pl.ds(start, t), sls[h]]) for h in heads]
        for h in heads:
            acc_ref[h] += pvs[h]
            run_ref[h] += tots[h]

    tile(i, True)

    def cond(c):
        kt, live = c
        return jnp.logical_and(kt >= 0, live)

    def body(c):
        kt, _ = c
        tile(kt, False)
        live = jnp.max(run_ref[...]) > EXP_FLUSH
        return kt - 1, live

    lax.while_loop(cond, body, (i - 1, jnp.bool_(True)))
    for p in range(pp):
        o_ref[:, p * LANES:(p + 1) * LANES] = jnp.where(lo, acc_ref[2 * p], acc_ref[2 * p + 1]).astype(o_ref.dtype)


def _sb_attention(h0, *, q_col, k_col, v_col, n_pairs, t=128, pp=4):
    s = h0.shape[0]
    assert n_pairs % pp == 0 and q_col % pp == 0 and k_col % pp == 0 and v_col % pp == 0
    w = pp * LANES
    resident = pl.Buffered(1)
    return pl.pallas_call(
        functools.partial(_sb_kernel, t=t, pp=pp),
        grid=(n_pairs // pp, s // t),
        in_specs=[
            pl.BlockSpec((t, w), lambda p, i: (i, q_col // pp + p)),
            pl.BlockSpec((s, w), lambda p, i: (0, k_col // pp + p), pipeline_mode=resident),
            pl.BlockSpec((s, w), lambda p, i: (0, v_col // pp + p), pipeline_mode=resident),
        ],
        out_specs=pl.BlockSpec((t, w), lambda p, i: (i, p)),
        out_shape=jax.ShapeDtypeStruct((s, n_pairs * LANES), BF16),
        scratch_shapes=[pltpu.VMEM((2 * pp, t, LANES), F32), pltpu.VMEM((2 * pp, t, LANES), F32)],
        compiler_params=_cparams(("arbitrary", "arbitrary")),
        name="stick_breaking",
    )(h0, h0, h0)


def _moba_prep_kernel(k_ref, kaug_ref, kmean_ref):
    b = pl.program_id(1)
    k = k_ref[...]
    kmean_ref[...] = jnp.mean(k.astype(F32), axis=0, keepdims=True)[None]
    onehot = (_iota(k.shape, 1) == b).astype(BF16)
    kaug_ref[...] = jnp.concatenate([k, onehot], axis=1)


def _moba_prep(h0, *, k_col, n_pairs):
    s = h0.shape[0]
    nb = s // MOBA_BLOCK
    return pl.pallas_call(
        _moba_prep_kernel,
        grid=(n_pairs, nb),
        in_specs=[pl.BlockSpec((MOBA_BLOCK, LANES), lambda p, b: (b, k_col + p))],
        out_specs=[
            pl.BlockSpec((MOBA_BLOCK, 2 * LANES), lambda p, b: (b, p)),
            pl.BlockSpec((1, 1, LANES), lambda p, b: (p * nb + b, 0, 0)),
        ],
        out_shape=[
            jax.ShapeDtypeStruct((s, n_pairs * 2 * LANES), BF16),
            jax.ShapeDtypeStruct((n_pairs * nb, 1, LANES), F32),
        ],
        compiler_params=_cparams(("arbitrary", "arbitrary")),
        name="moba_prep",
    )(h0)


def _moba_kernel(q_ref, kaug_ref, vt_ref, kmean_ref, o_ref, m_ref, acc_ref, st_ref, *, nb, g):
    t = MOBA_BLOCK
    i = pl.program_id(1)
    qf = q_ref[...].astype(F32)
    lo = _head_a(_iota((t, LANES), 1))
    qt = jnp.concatenate([jnp.where(lo, qf, 0.0).T, jnp.where(lo, 0.0, qf).T], axis=1).astype(BF16)
    kmean = kmean_ref[...]
    km_hi = kmean.astype(BF16)
    km_lo = (kmean - km_hi.astype(F32)).astype(BF16)
    gate = _dot(km_hi, qt) + _dot(km_lo, qt)
    blk = _iota((LANES, 2 * t), 0)
    blk_f = blk.astype(F32)
    gate = jnp.where(blk < i, gate, -jnp.inf)
    chosen = blk == i
    for _ in range(MOBA_TOPK):
        m = jnp.max(gate, axis=0, keepdims=True)
        first = jnp.min(jnp.where(gate == m, blk_f, float(LANES)), axis=0, keepdims=True)
        pick = blk_f == first
        chosen = chosen | (pick & (m > -jnp.inf))
        gate = jnp.where(pick, -jnp.inf, gate)
    bias = jnp.where(chosen | (blk >= nb), 0.0, NEG).astype(BF16)
    q_both = jnp.concatenate([qt, bias], axis=0)

    tk = g * t
    last = lax.shift_right_logical(i, int(np.log2(g)))
    ones = jnp.ones((ONES_ROWS, tk), BF16)

    def scores(u):
        start = pl.multiple_of(u * tk, tk)
        return _dot(kaug_ref[pl.ds(start, tk), :], q_both)

    m_ref[...] = jnp.full(m_ref.shape, NEG, F32)
    acc_ref[...] = jnp.zeros_like(acc_ref)
    key_pos = _iota((tk, 2 * t), 0) + (last * g - i) * t
    query_pos = _iota((tk, 2 * t), 1) & (t - 1)
    st_ref[...] = jnp.where(key_pos <= query_pos, scores(last), NEG)

    def body(n, carry):
        cur = jnp.where(n == 0, last, n - 1)
        nxt = jnp.minimum(n, jnp.maximum(last - 1, 0))
        st_next = scores(nxt)
        vt = jnp.concatenate([vt_ref[0, cur * g + b] for b in range(g)], axis=1)
        vt = jnp.concatenate([vt, ones], axis=0)
        st = st_ref[...]
        m_old = m_ref[...]
        m_new = jnp.maximum(m_old, jnp.max(st, axis=0, keepdims=True))
        alpha = jnp.exp2(m_old - m_new)
        pt = jnp.exp2(st - m_new).astype(BF16)
        acc_ref[...] = alpha * acc_ref[...] + _dot(vt, pt)
        m_ref[...] = m_new
        st_ref[...] = st_next
        return carry

    lax.fori_loop(0, last + 1, body, 0)
    acc = acc_ref[...]
    outs = [acc[:LANES, h * t:(h + 1) * t] / acc[LANES:LANES + 1, h * t:(h + 1) * t] for h in range(2)]
    first = _iota((LANES, t), 0) < HEAD_DIM
    o_ref[...] = jnp.where(first, outs[0], outs[1]).T.astype(o_ref.dtype)


def _moba_attention(h0, kaug, vt, kmean, *, q_col, n_pairs, g=4):
    s = h0.shape[0]
    t = MOBA_BLOCK
    nb = s // t
    g = min(g, nb)
    assert nb <= LANES and nb % g == 0
    return pl.pallas_call(
        functools.partial(_moba_kernel, nb=nb, g=g),
        grid=(n_pairs, nb),
        in_specs=[
            pl.BlockSpec((t, LANES), lambda p, i: (i, q_col + p)),
            pl.BlockSpec((s, 2 * LANES), lambda p, i: (0, p)),
            pl.BlockSpec((1, nb, LANES, t), lambda p, i: (p, 0, 0, 0)),
            pl.BlockSpec((LANES, LANES), lambda p, i: (p, 0)),
        ],
        out_specs=pl.BlockSpec((t, LANES), lambda p, i: (i, p)),
        out_shape=jax.ShapeDtypeStruct((s, n_pairs * LANES), BF16),
        scratch_shapes=[pltpu.VMEM((1, 2 * t), F32),
                        pltpu.VMEM((LANES + ONES_ROWS, 2 * t), F32),
                        pltpu.VMEM((g * t, 2 * t), F32)],
        compiler_params=_cparams(("arbitrary", "arbitrary")),
        name="moba",
    )(h0, kaug, vt, kmean)


def _dsa_kernel(it_ref, jt_ref, qi_ref, wi_ref, ki_ref, q_ref, k_ref, vt_ref, o_ref,
                key_ref, half_ref, thr_ref, need_ref, seen_ref, m_ref, acc_ref, qt_ref,
                *, t, g, n_heads, topk):
    step = pl.program_id(0)
    i = it_ref[step]
    j = jt_ref[step]
    lo = _head_a(_iota((t, LANES), 1))
    key_le_query = _iota((t, t), 0) <= _iota((t, t), 1)
    n_pairs = n_heads // 2

    def split_t(pair):
        pf = pair.astype(F32)
        return (jnp.where(lo, pf, 0.0).T.astype(BF16), jnp.where(lo, 0.0, pf).T.astype(BF16))

    @pl.when(j == 0)
    def _select():
        qi = qi_ref[...]
        wi = wi_ref[0, 0].astype(F32)
        qit = [jnp.concatenate(split_t(qi[:, p * LANES:(p + 1) * LANES]), axis=1)
               for p in range(IDX_HEADS // 2)]
        q = q_ref[...]
        for p in range(n_pairs):
            qt_ref[p] = jnp.concatenate(split_t(q[:, p * LANES:(p + 1) * LANES]), axis=1)

        def score_tile(kt, carry):
            start = pl.multiple_of(kt * t, t)
            kblk = ki_ref[pl.ds(start, t), :]
            logits = [_dot(kblk, qit[p]) for p in range(IDX_HEADS // 2)]
            sc = jnp.zeros((t, t), F32)
            for h in range(IDX_HEADS):
                lg = logits[h // 2][:, (h % 2) * t:(h % 2 + 1) * t]
                sc = sc + wi[h:h + 1, :] * jnp.maximum(lg, 0.0)
            bits = pltpu.bitcast(sc, I32)
            key = jnp.where(bits >= 0, bits, bits ^ jnp.int32(0x7FFFFFFF))
            key = jnp.where(jnp.logical_or(kt < i, key_le_query), key, jnp.int32(INT_MIN))
            key_ref[kt] = key
            half_ref[kt] = lax.shift_right_arithmetic(key, jnp.int32(16)).astype(I16)
            return carry

        lax.fori_loop(0, i + 1, score_tile, 0)
        for b in range(1, g + 1):
            key_ref[i + b] = jnp.full((t, t), INT_MIN, I32)
        for b in range(1, COUNT_UNROLL):
            half_ref[i + b] = jnp.full((t, t), I16_MIN, I16)
        n_trips = (i + COUNT_UNROLL) // COUNT_UNROLL

        def count(pred):
            def cbody(trip, c):
                for kt in [COUNT_UNROLL * trip + u for u in range(COUNT_UNROLL)]:
                    hit = jnp.where(pred(half_ref[kt]), jnp.int16(1), jnp.int16(0))
                    for r in range(t // 16):
                        c = c + hit[r * 16:(r + 1) * 16]
                return c
            c = lax.fori_loop(0, n_trips, cbody, jnp.zeros((16, t), I16))
            return jnp.sum(c.astype(F32), axis=0, keepdims=True)

        def search(base):
            v = jnp.zeros((1, t), I32)
            for b in range(15, -1, -1):
                cand = v | jnp.int32(1 << b)
                cand16 = (cand + I16_MIN).astype(I16)
                cnt = base + count(lambda hh, c=cand16: hh >= c)
                v = jnp.where(cnt >= float(topk), cand, v)
            return v

        hi = search(0.0) + I16_MIN
        hi16 = hi.astype(I16)
        n_above = count(lambda hh: hh > hi16)

        def narrow(kt, carry):
            low = (key_ref[kt] & jnp.int32(0xFFFF)) + I16_MIN
            half_ref[kt] = jnp.where(half_ref[kt] == hi16, low.astype(I16), jnp.int16(I16_MIN))
            return carry

        lax.fori_loop(0, i + 1, narrow, 0)
        lo = search(n_above)
        lo16 = (lo + I16_MIN).astype(I16)
        n_gt = n_above + count(lambda hh: hh > lo16)
        thr_ref[...] = hi * 65536 + lo
        need_ref[...] = float(topk) - n_gt
        seen_ref[...] = jnp.zeros_like(seen_ref)
        m_ref[...] = jnp.full(m_ref.shape, NEG, F32)
        acc_ref[...] = jnp.zeros_like(acc_ref)

    thr = thr_ref[...]
    need = need_ref[...]
    earlier = (_iota((t, t), 1) < _iota((t, t), 0)).astype(BF16)
    seen = seen_ref[...]
    biases = []
    for b in range(g):
        blk = j * g + b
        key = key_ref[blk]
        eq = key == thr
        rank = seen + _dot(earlier, jnp.where(eq, 1.0, 0.0).astype(BF16))
        sel = (key > thr) | (eq & (rank < need))
        sel = sel & jnp.logical_or(blk < i, jnp.logical_and(blk == i, key_le_query))
        seen = seen + jnp.sum(jnp.where(eq, 1.0, 0.0), axis=0, keepdims=True)
        biases.append(jnp.where(sel, 0.0, NEG))
    seen_ref[...] = seen
    bias = jnp.concatenate(biases, axis=0)
    bias = jnp.concatenate([bias, bias], axis=1)

    kb = k_ref[...]
    ones = jnp.ones((ONES_ROWS, g * t), BF16)

    def scores(p):
        return _dot(kb[:, p * LANES:(p + 1) * LANES], qt_ref[p]) + bias

    def softmax(p, st):
        m_old = m_ref[p]
        m_new = jnp.maximum(m_old, jnp.max(st, axis=0, keepdims=True))
        m_ref[p] = m_new
        return jnp.exp2(m_old - m_new), jnp.exp2(st - m_new).astype(BF16)

    def accumulate(p, alpha, pt):
        vt = jnp.concatenate([vt_ref[p, b] for b in range(g)], axis=1)
        vt = jnp.concatenate([vt, ones], axis=0)
        acc_ref[p] = alpha * acc_ref[p] + _dot(vt, pt)

    st_next = scores(0)
    for p in range(n_pairs):
        st = st_next
        if p + 1 < n_pairs:
            st_next = scores(p + 1)
        accumulate(p, *softmax(p, st))

    @pl.when(j == lax.shift_right_logical(i, int(np.log2(g))))
    def _finish():
        first = _iota((LANES, t), 0) < HEAD_DIM
        for p in range(n_pairs):
            acc = acc_ref[p]
            a0, a1 = acc[:, :t], acc[:, t:]
            out_t = jnp.where(first, a0[:LANES] / a0[LANES:LANES + 1], a1[:LANES] / a1[LANES:LANES + 1])
            o_ref[:, p * LANES:(p + 1) * LANES] = out_t.T.astype(o_ref.dtype)


def _dsa_attention(h1, vt, *, n_heads, t=256, g=4):
    s = h1.shape[0]
    width = n_heads * HEAD_DIM
    nt = s // t
    g = min(g, nt)
    assert nt % g == 0
    topk = min(DSA_TOPK, s // 4)
    iw = IDX_HEADS * HEAD_DIM
    qi_blk = 2 * width // iw
    ki_blk = (2 * width + iw) // LANES
    wi_grp = width // LANES
    tri = [(i, j) for i in range(nt) for j in range(i // g + 1)]
    it = jnp.asarray([a for a, _ in tri], I32)
    jt = jnp.asarray([b for _, b in tri], I32)
    grid_spec = pltpu.PrefetchScalarGridSpec(
        num_scalar_prefetch=2,
        grid=(len(tri),),
        in_specs=[
            pl.BlockSpec((t, iw), lambda n, it, jt: (it[n], qi_blk)),
            pl.BlockSpec((1, 1, ONES_ROWS, t), lambda n, it, jt: (wi_grp, it[n], 0, 0)),
            pl.BlockSpec((s, LANES), lambda n, it, jt: (0, ki_blk)),
            pl.BlockSpec((t, width), lambda n, it, jt: (it[n], 0)),
            pl.BlockSpec((g * t, width), lambda n, it, jt: (jt[n], 1)),
            pl.BlockSpec((n_heads // 2, g, LANES, t), lambda n, it, jt: (0, jt[n], 0, 0)),
        ],
        out_specs=pl.BlockSpec((t, width), lambda n, it, jt: (it[n], 0)),
        scratch_shapes=[
            pltpu.VMEM((nt + g, t, t), I32),
            pltpu.VMEM((nt + COUNT_UNROLL - 1, t, t), I16),
            pltpu.VMEM((1, t), I32),
            pltpu.VMEM((1, t), F32),
            pltpu.VMEM((1, t), F32),
            pltpu.VMEM((n_heads // 2, 1, 2 * t), F32),
            pltpu.VMEM((n_heads // 2, LANES + ONES_ROWS, 2 * t), F32),
            pltpu.VMEM((n_heads // 2, LANES, 2 * t), BF16),
        ],
    )
    return pl.pallas_call(
        functools.partial(_dsa_kernel, t=t, g=g, n_heads=n_heads, topk=topk),
        grid_spec=grid_spec,
        out_shape=jax.ShapeDtypeStruct((s, width), BF16),
        compiler_params=_cparams(("arbitrary",)),
        name="dsa",
    )(it, jt, h1, vt, h1, h1, h1, vt)


def _layer_norm(y, g, b):
    mu = jnp.mean(y, axis=-1, keepdims=True)
    yc = y - mu
    var = jnp.mean(yc * yc, axis=-1, keepdims=True)
    return yc * lax.rsqrt(var + LN_EPS) * g + b


def _outproj_kernel(*refs, n_in, alpha):
    o_refs = refs[:n_in]
    w_refs = refs[n_in:2 * n_in]
    x_ref, g_ref, b_ref, out_ref = refs[2 * n_in:]
    y = alpha * x_ref[...]
    for o_ref, w_ref in zip(o_refs, w_refs):
        y = y + _dot(o_ref[...], w_ref[...])
    out_ref[...] = _layer_norm(y, g_ref[...], b_ref[...])


def _outproj_ln(os_, ws, x, g, b, alpha, *, tm=256):
    s, d = x.shape
    n_in = len(os_)
    in_specs = ([pl.BlockSpec((tm, o.shape[1]), lambda i: (i, 0)) for o in os_]
                + [pl.BlockSpec(w.shape, lambda i: (0, 0)) for w in ws]
                + [pl.BlockSpec((tm, d), lambda i: (i, 0)),
                   pl.BlockSpec((1, d), lambda i: (0, 0)),
                   pl.BlockSpec((1, d), lambda i: (0, 0))])
    return pl.pallas_call(
        functools.partial(_outproj_kernel, n_in=n_in, alpha=alpha),
        grid=(s // tm,),
        in_specs=in_specs,
        out_specs=pl.BlockSpec((tm, d), lambda i: (i, 0)),
        out_shape=jax.ShapeDtypeStruct((s, d), F32),
        compiler_params=_cparams(("arbitrary",)),
        name="out_proj_ln",
    )(*os_, *ws, x, g, b)


def _moe_kernel(x_ref, wr_ref, br_ref, w1g_ref, w1u_ref, b1g_ref, b1u_ref, w2_ref, b2_ref,
                g_ref, b_ref, out_ref,
                xb_ref, rank_ref, rank_t_ref, gate_t_ref, cnt_ref, acc_ref,
                *, tt, ch, alpha):
    e = pl.program_id(1)
    n_e = pl.num_programs(1)

    @pl.when(e == 0)
    def _route():
        x = x_ref[...]
        xb_ref[...] = x.astype(BF16)
        logits = lax.dot_general(wr_ref[...], x, (((1,), (1,)), ((), ())),
                                 precision=lax.Precision.HIGHEST,
                                 preferred_element_type=F32) + br_ref[...]
        row = _iota(logits.shape, 0).astype(F32)
        work = jnp.where(row < N_EXPERTS, logits, -jnp.inf)
        chosen = row < 0
        top = None
        for r in range(TOP_K):
            m = jnp.max(work, axis=0, keepdims=True)
            idx = jnp.min(jnp.where(work == m, row, float(LANES)), axis=0, keepdims=True)
            pick = row == idx
            chosen = chosen | pick
            work = jnp.where(pick, -jnp.inf, work)
            if r == 0:
                top = m
        ex = jnp.where(chosen, jnp.exp(logits - top), 0.0)
        gate = ex / jnp.sum(ex, axis=0, keepdims=True)
        sel16 = jnp.where(chosen, 1.0, 0.0).astype(BF16)
        before = (_iota((tt, tt), 0) < _iota((tt, tt), 1)).astype(BF16)
        pos = _dot(sel16, before)
        rank = jnp.where(chosen, pos, -1.0)
        rank_ref[...] = rank
        rank_t_ref[...] = rank.T
        gate_t_ref[...] = gate.T
        cnt = jnp.sum(jnp.where(chosen, 1.0, 0.0), axis=1, keepdims=True)
        cnt_ref[...] = jnp.broadcast_to(cnt, cnt_ref.shape)
        acc_ref[...] = jnp.zeros_like(acc_ref)

    lane = _iota((tt, LANES), 1)
    on_e = lane == e
    rank_col = jnp.sum(jnp.where(on_e, rank_t_ref[...], 0.0), axis=1, keepdims=True)
    gate_col = jnp.sum(jnp.where(on_e, gate_t_ref[...], 0.0), axis=1, keepdims=True)
    rank_row = rank_ref[pl.ds(e, 1), :]
    n_rows = jnp.max(cnt_ref[pl.ds(e, 1), :]).astype(I32)
    n_chunks = (n_rows + (ch - 1)) // ch
    pad_rows = jnp.zeros((2 * LANES - ch, x_ref.shape[1]), BF16)
    slot = _iota((1, 2 * LANES), 1)
    slot = jnp.where(slot < ch, slot.astype(F32), -1e9)
    b1g = b1g_ref[0]
    b1u = b1u_ref[0]
    b2 = b2_ref[0]

    def chunk(c, carry):
        w1g = w1g_ref[0]
        w1u = w1u_ref[0]
        w2 = w2_ref[0]
        r0 = (c * ch).astype(F32)
        gather = (rank_row == (r0 + _iota((ch, tt), 0).astype(F32))).astype(BF16)
        xg = _dot(gather, xb_ref[...]).astype(BF16)
        hg = jnp.minimum(_dot(xg, w1g) + b1g, SWIGLU_LIMIT)
        hu = jnp.clip(_dot(xg, w1u) + b1u, -SWIGLU_LIMIT, SWIGLU_LIMIT)
        act = (hu + 1.0) * (hg * jax.nn.sigmoid(SWIGLU_ALPHA * hg))
        y = _dot(act.astype(BF16), w2) + b2
        scatter = jnp.where(rank_col == (r0 + slot), gate_col, 0.0).astype(BF16)
        acc_ref[...] += _dot(scatter, jnp.concatenate([y.astype(BF16), pad_rows], axis=0))
        return carry

    lax.fori_loop(0, n_chunks, chunk, 0)

    @pl.when(e == n_e - 1)
    def _finish():
        out_ref[...] = _layer_norm(alpha * x_ref[...] + acc_ref[...], g_ref[...], b_ref[...])


def _moe_ln(x, wr_t, br, w1g, w1u, b1g, b1u, w2, b2, g, b, alpha, *, tt=1024, ch=160):
    t, d = x.shape
    tt = min(tt, t)
    assert ch % 16 == 0 and ch <= 2 * LANES
    f = w2.shape[1]
    return pl.pallas_call(
        functools.partial(_moe_kernel, tt=tt, ch=ch, alpha=alpha),
        grid=(t // tt, N_EXPERTS),
        in_specs=[
            pl.BlockSpec((tt, d), lambda i, e: (i, 0)),
            pl.BlockSpec((LANES, d), lambda i, e: (0, 0)),
            pl.BlockSpec((LANES, 1), lambda i, e: (0, 0)),
            pl.BlockSpec((1, d, f), lambda i, e: (e, 0, 0)),
            pl.BlockSpec((1, d, f), lambda i, e: (e, 0, 0)),
            pl.BlockSpec((1, 1, f), lambda i, e: (e, 0, 0)),
            pl.BlockSpec((1, 1, f), lambda i, e: (e, 0, 0)),
            pl.BlockSpec((1, f, d), lambda i, e: (e, 0, 0)),
            pl.BlockSpec((1, 1, d), lambda i, e: (e, 0, 0)),
            pl.BlockSpec((1, d), lambda i, e: (0, 0)),
            pl.BlockSpec((1, d), lambda i, e: (0, 0)),
        ],
        out_specs=pl.BlockSpec((tt, d), lambda i, e: (i, 0)),
        out_shape=jax.ShapeDtypeStruct((t, d), F32),
        scratch_shapes=[
            pltpu.VMEM((tt, d), BF16),
            pltpu.VMEM((LANES, tt), F32),
            pltpu.VMEM((tt, LANES), F32),
            pltpu.VMEM((tt, LANES), F32),
            pltpu.VMEM((LANES, LANES), F32),
            pltpu.VMEM((tt, d), F32),
        ],
        compiler_params=_cparams(("arbitrary", "arbitrary")),
        name="moe_ln",
    )(x, wr_t, br, w1g, w1u, b1g, b1u, w2, b2, g, b)


def _ple_kernel(x_ref, p_ref, wg_ref, wp_ref, o_ref):
    x = x_ref[...]
    gate = jax.nn.sigmoid(_dot(x.astype(BF16), wg_ref[...]))
    emb = _dot(p_ref[...].astype(BF16), wp_ref[...])
    o_ref[...] = x + gate * emb


def _ple(x, p, wg, wp, *, tm=512):
    s, d = x.shape
    dp = p.shape[1]
    return pl.pallas_call(
        _ple_kernel,
        grid=(s // tm,),
        in_specs=[
            pl.BlockSpec((tm, d), lambda i: (i, 0)),
            pl.BlockSpec((tm, dp), lambda i: (i, 0)),
            pl.BlockSpec((d, d), lambda i: (0, 0)),
            pl.BlockSpec((dp, d), lambda i: (0, 0)),
        ],
        out_specs=pl.BlockSpec((tm, d), lambda i: (i, 0)),
        out_shape=jax.ShapeDtypeStruct((s, d), F32),
        compiler_params=_cparams(("arbitrary",)),
        name="ple",
    )(x, p, wg, wp)


def _mixer_ab(x, w_in, cos_t, sin_t, *, sb_heads, moba_heads):
    s = x.shape[0]
    wa = sb_heads * HEAD_DIM
    wb = moba_heads * HEAD_DIM
    qa, ka, va, qb, kb, vb = jnp.split(w_in, np.cumsum([wa, wa, wa, wb, wb]).tolist(), axis=1)
    rot = _rotary_layout(wb)
    w = jnp.concatenate([qb[:, rot], kb[:, rot], qa, ka, va], axis=1).astype(BF16)
    att = HEAD_DIM ** -0.5
    scale = jnp.concatenate([jnp.full((wb,), att * LOG2E), jnp.ones((wb,)), jnp.full((wa,), att),
                             jnp.ones((2 * wa,))]).astype(F32)[None]
    tn = cos_t.shape[1]
    h0 = _proj(x, w, scale, cos_t, sin_t, ((0, 2 * wb // tn),), tn=tn)
    n_pairs = moba_heads // 2
    vt = _proj_t(x, vb.astype(BF16), jnp.ones((1, wb), F32), groups=n_pairs)
    c = lambda off: off // LANES
    o_a = _sb_attention(h0, q_col=c(2 * wb), k_col=c(2 * wb + wa), v_col=c(2 * wb + 2 * wa),
                        n_pairs=sb_heads // 2)
    nb = s // MOBA_BLOCK
    kaug, kmean = _moba_prep(h0, k_col=c(wb), n_pairs=n_pairs)
    kmean = jnp.pad(kmean.reshape(n_pairs, nb, LANES), ((0, 0), (0, LANES - nb), (0, 0)))
    kmean = kmean.reshape(n_pairs * LANES, LANES)
    o_b = _moba_attention(h0, kaug, vt, kmean, q_col=0, n_pairs=n_pairs)
    return o_a, o_b


def _mixer_c(x, w_in, cos_t, sin_t, *, n_heads):
    width = n_heads * HEAD_DIM
    iw = IDX_HEADS * HEAD_DIM
    d = w_in.shape[0]
    q, k, v, qi, ki, wi = jnp.split(w_in, np.cumsum([width, width, width, iw, HEAD_DIM]).tolist(), axis=1)
    tn = cos_t.shape[1]
    half = HEAD_DIM // 2
    ki_pad = jnp.zeros((d, tn - 2 * HEAD_DIM), w_in.dtype)
    w = jnp.concatenate([q[:, _rotary_layout(width)], k[:, _rotary_layout(width)], qi[:, _rotary_layout(iw)],
                         ki[:, :half], ki[:, :half], ki[:, half:], ki[:, half:], ki_pad], axis=1).astype(BF16)
    att = HEAD_DIM ** -0.5
    scale = jnp.concatenate([jnp.full((width,), att * LOG2E), jnp.ones((width,)), jnp.full((iw,), att),
                             jnp.ones((tn,))]).astype(F32)[None]
    h1 = _proj(x, w, scale, cos_t, sin_t, ((0, (2 * width + iw) // tn + 1),), tn=tn)
    wi_pad = jnp.zeros((d, LANES - IDX_HEADS), w_in.dtype)
    wv = jnp.concatenate([v, wi, wi_pad], axis=1).astype(BF16)
    vscale = jnp.concatenate([jnp.ones((width,)), jnp.full((LANES,), IDX_HEADS ** -0.5)]).astype(F32)[None]
    vt = _proj_t(x, wv, vscale, groups=3)
    return _dsa_attention(h1, vt, n_heads=n_heads)


def _split_kernel(w_ref, perm_ref, g_ref, u_ref):
    half = g_ref.shape[2]
    both = _dot(w_ref[0, 0].astype(BF16), perm_ref[...])
    g_ref[0] = both[:, :half].astype(BF16)
    u_ref[0] = both[:, half:].astype(BF16)


def _split_gate_up(w1_all, layer, *, tc=512):
    _, n_e, d, f2 = w1_all.shape
    half = tc // 2
    src = jnp.concatenate([2 * jnp.arange(half), 2 * jnp.arange(half) + 1])
    perm = (jnp.arange(tc)[:, None] == src[None, :]).astype(BF16)
    out = jax.ShapeDtypeStruct((n_e, d, f2 // 2), BF16)
    return pl.pallas_call(
        _split_kernel,
        grid=(n_e, f2 // tc),
        in_specs=[pl.BlockSpec((1, 1, d, tc), lambda e, c: (layer, e, 0, c)),
                  pl.BlockSpec((tc, tc), lambda e, c: (0, 0))],
        out_specs=[pl.BlockSpec((1, d, half), lambda e, c: (e, 0, c)),
                   pl.BlockSpec((1, d, half), lambda e, c: (e, 0, c))],
        out_shape=[out, out],
        compiler_params=_cparams(("arbitrary", "arbitrary")),
        name="split_gate_up",
    )(w1_all, perm)


def _cast_kernel(w_ref, o_ref):
    o_ref[0] = w_ref[0, 0].astype(o_ref.dtype)


def _cast_layer(w_all, layer):
    _, n_e, f, d = w_all.shape
    return pl.pallas_call(
        _cast_kernel,
        grid=(n_e,),
        in_specs=[pl.BlockSpec((1, 1, f, d), lambda e: (layer, e, 0, 0))],
        out_specs=pl.BlockSpec((1, f, d), lambda e: (e, 0, 0)),
        out_shape=jax.ShapeDtypeStruct((n_e, f, d), BF16),
        compiler_params=_cparams(("arbitrary",)),
        name="cast_w2",
    )(w_all)


def _moe_weights(w_r, b_r, w1_all, b1, w2_all, b2, layer):
    wr_t = jnp.pad(w_r.T, ((0, LANES - N_EXPERTS), (0, 0)))
    br = jnp.pad(b_r, (0, LANES - N_EXPERTS))[:, None]
    w1g, w1u = _split_gate_up(w1_all, layer)
    return (wr_t, br, w1g, w1u, b1[:, None, 0::2], b1[:, None, 1::2], _cast_layer(w2_all, layer),
            b2[:, None, :])


def kernel(x, p, ab_w_in, ab_w_out, c_w_in, c_w_out, ln_g, ln_b, router_w, router_b,
           moe_w1, moe_b1, moe_w2, moe_b2, ple_w_proj, ple_w_gate):
    b, s, d = x.shape
    assert b == 1
    depth = p.shape[0]
    alpha = float((2 * depth) ** 0.25)
    cos_t, sin_t = _rope_tables(s, 4 * LANES)
    sb_heads = moba_heads = ab_w_out.shape[1] // (2 * HEAD_DIM)
    dsa_heads = c_w_out.shape[1] // HEAD_DIM
    xs = x[0]
    for i in range(depth):
        j = i // 2
        g = ln_g[i][:, None, :]
        bb = ln_b[i][:, None, :]
        if i % 2 == 0:
            o_a, o_b = _mixer_ab(xs, ab_w_in[j], cos_t, sin_t, sb_heads=sb_heads, moba_heads=moba_heads)
            w_out = ab_w_out[j].astype(BF16)
            wa = sb_heads * HEAD_DIM
            xs = _outproj_ln([o_a, o_b], [w_out[:wa], w_out[wa:]], xs, g[0], bb[0], alpha)
        else:
            o_c = _mixer_c(xs, c_w_in[j], cos_t, sin_t, n_heads=dsa_heads)
            xs = _outproj_ln([o_c], [c_w_out[j].astype(BF16)], xs, g[0], bb[0], alpha)
        mw = _moe_weights(router_w[i], router_b[i], moe_w1, moe_b1[i], moe_w2, moe_b2[i], i)
        xs = _moe_ln(xs, *mw, g[1], bb[1], alpha)
        xs = _ple(xs, p[i, 0], ple_w_gate[i].astype(BF16), ple_w_proj[i].astype(BF16))
    return xs[None]
```

```python
import functools

import jax
import jax.numpy as jnp
import numpy as np
from jax import lax
from jax.experimental import pallas as pl
from jax.experimental.pallas import tpu as pltpu

F32 = jnp.float32
BF16 = jnp.bfloat16
I32 = jnp.int32

HEAD_DIM = 64
LANES = 128
MOBA_BLOCK = 256
MOBA_TOPK = 3
DSA_TOPK = 256
IDX_HEADS = 8
N_EXPERTS = 32
TOP_K = 4
SWIGLU_LIMIT = 7.0
SWIGLU_ALPHA = 1.702
ROPE_THETA = 10000.0
LN_EPS = 1e-5
NEG = -1e30
EXP_FLUSH = -88.0
INT_MIN = -(2 ** 31)
COUNT_UNROLL = 4
LOG2E = 1.4426950408889634
ONES_ROWS = 16
VMEM_LIMIT = 56 * 1024 * 1024


def _cparams(sem, flags=None):
    return pltpu.CompilerParams(dimension_semantics=sem, vmem_limit_bytes=VMEM_LIMIT, flags=flags)


def _dot(a, b):
    return jnp.dot(a, b, preferred_element_type=F32)


def _dot_nt(a, b):
    return lax.dot_general(a, b, (((1,), (1,)), ((), ())), preferred_element_type=F32)


def _dot_tn(a, b):
    return lax.dot_general(a, b, (((0,), (0,)), ((), ())), preferred_element_type=F32)


def _iota(shape, dim):
    return lax.broadcasted_iota(I32, shape, dim)


def _proj_kernel(x_ref, w_ref, scale_ref, cos_ref, sin_ref, o_ref, xb_ref, *, rope_ranges, tn):
    j = pl.program_id(1)

    @pl.when(j == 0)
    def _():
        xb_ref[...] = x_ref[...].astype(BF16)

    h = _dot(xb_ref[...], w_ref[...]) * scale_ref[...]
    is_rope = functools.reduce(jnp.logical_or, [(j >= a) & (j < b) for a, b in rope_ranges])

    @pl.when(is_rope)
    def _():
        for c in range(tn // LANES):
            sl = slice(c * LANES, (c + 1) * LANES)
            hs = h[:, sl]
            partner = pltpu.roll(hs, LANES // 2, 1)
            o_ref[:, sl] = (hs * cos_ref[:, sl] + partner * sin_ref[:, sl]).astype(o_ref.dtype)

    @pl.when(jnp.logical_not(is_rope))
    def _():
        o_ref[...] = h.astype(o_ref.dtype)


def _proj(x, w, scale, cos_t, sin_t, rope_ranges, *, tm=512, tn=256):
    s, d = x.shape
    n = w.shape[1]
    return pl.pallas_call(
        functools.partial(_proj_kernel, rope_ranges=rope_ranges, tn=tn),
        grid=(s // tm, n // tn),
        in_specs=[
            pl.BlockSpec((tm, d), lambda i, j: (i, 0)),
            pl.BlockSpec((d, tn), lambda i, j: (0, j)),
            pl.BlockSpec((1, tn), lambda i, j: (0, j)),
            pl.BlockSpec((tm, tn), lambda i, j: (i, 0)),
            pl.BlockSpec((tm, tn), lambda i, j: (i, 0)),
        ],
        out_specs=pl.BlockSpec((tm, tn), lambda i, j: (i, j)),
        out_shape=jax.ShapeDtypeStruct((s, n), BF16),
        scratch_shapes=[pltpu.VMEM((tm, d), BF16)],
        compiler_params=_cparams(("arbitrary", "arbitrary")),
        name="in_proj",
    )(x, w, scale, cos_t, sin_t)


def _proj_t_kernel(x_ref, w_ref, scale_ref, o_ref, *, groups, tiles, tm):
    h = _dot(x_ref[...].astype(BF16), w_ref[...]) * scale_ref[...]
    for g in range(groups):
        for r in range(tiles):
            o_ref[g, r] = h[r * tm:(r + 1) * tm, g * LANES:(g + 1) * LANES].T.astype(o_ref.dtype)


def _proj_t(x, w, scale, *, groups, tm=256, tiles=4):
    s, d = x.shape
    n = w.shape[1]
    tc = groups * LANES
    tiles = min(tiles, s // tm)
    return pl.pallas_call(
        functools.partial(_proj_t_kernel, groups=groups, tiles=tiles, tm=tm),
        grid=(s // (tiles * tm), n // tc),
        in_specs=[
            pl.BlockSpec((tiles * tm, d), lambda i, j: (i, 0)),
            pl.BlockSpec((d, tc), lambda i, j: (0, j)),
            pl.BlockSpec((1, tc), lambda i, j: (0, j)),
        ],
        out_specs=pl.BlockSpec((groups, tiles, LANES, tm), lambda i, j: (j, i, 0, 0)),
        out_shape=jax.ShapeDtypeStruct((n // LANES, s // tm, LANES, tm), BF16),
        compiler_params=_cparams(("arbitrary", "arbitrary")),
        name="proj_t",
    )(x, w, scale)


def _rope_tables(s, width):
    half = HEAD_DIM // 2
    inv = ROPE_THETA ** (-jnp.arange(half, dtype=F32) / half)
    ang = jnp.arange(s, dtype=F32)[:, None] * inv[None, :]
    cos = jnp.cos(ang)
    sin = jnp.sin(ang)
    reps = width // LANES
    cos_t = jnp.tile(jnp.concatenate([cos, cos, cos, cos], axis=1), (1, reps))
    sin_t = jnp.tile(jnp.concatenate([-sin, -sin, sin, sin], axis=1), (1, reps))
    return cos_t, sin_t


def _rotary_layout(n_cols):
    half = HEAD_DIM // 2
    lane = np.arange(n_cols)
    base = lane // LANES * LANES
    within = lane % LANES
    head = within // half % 2
    part = within // HEAD_DIM
    return base + head * HEAD_DIM + part * half + within % half


def _head_a(lane):
    return (lane % HEAD_DIM) < (HEAD_DIM // 2)


def _sb_kernel(q_ref, k_ref, v_ref, o_ref, acc_ref, run_ref, *, t, pp):
    i = pl.program_id(1)
    lane = _iota((t, LANES), 1)
    lo = lane < HEAD_DIM
    qh = []
    for p in range(pp):
        q = q_ref[:, p * LANES:(p + 1) * LANES]
        zero = jnp.zeros_like(q)
        qh.extend([jnp.where(lo, q, zero), jnp.where(lo, zero, q)])
    row = _iota((t, t), 0)
    col = _iota((t, t), 1)
    incl = (row >= col).astype(BF16)
    ones = jnp.ones((t, LANES), BF16)
    strict = col < row

    acc_ref[...] = jnp.zeros_like(acc_ref)
    run_ref[...] = jnp.zeros_like(run_ref)

    def tile(kt, diagonal):
        start = pl.multiple_of(kt * t, t)
        heads = range(2 * pp)
        sls = [slice((h // 2) * LANES, (h // 2 + 1) * LANES) for h in heads]
        zs = [_dot_nt(qh[h], k_ref[pl.ds(start, t), sls[h]]) for h in heads]
        ls16s = []
        for h in heads:
            log_stay = -(jnp.maximum(zs[h], 0.0) + jnp.log1p(jnp.exp(-jnp.abs(zs[h]))))
            if diagonal:
                log_stay = jnp.where(strict, log_stay, 0.0)
            ls16s.append(log_stay.astype(BF16))
        cums = [_dot(ls16s[h], incl) for h in heads]
        tots = [_dot(ls16s[h], ones) for h in heads]
        a16s = []
        for h in heads:
            a = jnp.exp(zs[h] + cums[h] + run_ref[h])
            if diagonal:
                a = jnp.where(strict, a, 0.0)
            a16s.append(a.astype(BF16))
        pvs = [_dot(a16s[h], v_ref[pl.ds(start, t), sls[h]]) for h in heads]
        for h in heads:
            acc_ref[h] += pvs[h]
            run_ref[h] += tots[h]

    tile(i, True)

    def cond(c):
        kt, live = c
        return jnp.logical_and(kt >= 0, live)

    def body(c):
        kt, _ = c
        tile(kt, False)
        live = jnp.max(run_ref[...]) > EXP_FLUSH
        return kt - 1, live

    lax.while_loop(cond, body, (i - 1, jnp.bool_(True)))
    for p in range(pp):
        o_ref[:, p * LANES:(p + 1) * LANES] = jnp.where(lo, acc_ref[2 * p], acc_ref[2 * p + 1]).astype(o_ref.dtype)


def _sb_attention(h0, *, q_col, k_col, v_col, n_pairs, t=128, pp=4):
    s = h0.shape[0]
    assert n_pairs % pp == 0 and q_col % pp == 0 and k_col % pp == 0 and v_col % pp == 0
    w = pp * LANES
    resident = pl.Buffered(1)
    return pl.pallas_call(
        functools.partial(_sb_kernel, t=t, pp=pp),
        grid=(n_pairs // pp, s // t),
        in_specs=[
            pl.BlockSpec((t, w), lambda p, i: (i, q_col // pp + p)),
            pl.BlockSpec((s, w), lambda p, i: (0, k_col // pp + p), pipeline_mode=resident),
            pl.BlockSpec((s, w), lambda p, i: (0, v_col // pp + p), pipeline_mode=resident),
        ],
        out_specs=pl.BlockSpec((t, w), lambda p, i: (i, p)),
        out_shape=jax.ShapeDtypeStruct((s, n_pairs * LANES), BF16),
        scratch_shapes=[pltpu.VMEM((2 * pp, t, LANES), F32), pltpu.VMEM((2 * pp, t, LANES), F32)],
        compiler_params=_cparams(("arbitrary", "arbitrary")),
        name="stick_breaking",
    )(h0, h0, h0)


def _moba_prep_kernel(k_ref, kaug_ref, kmean_ref, *, per_step):
    c = pl.program_id(1)
    for u in range(per_step):
        rows = slice(u * MOBA_BLOCK, (u + 1) * MOBA_BLOCK)
        k = k_ref[rows, :]
        kmean_ref[0, u:u + 1, :] = jnp.mean(k.astype(F32), axis=0, keepdims=True)
        onehot = (_iota(k.shape, 1) == c * per_step + u).astype(BF16)
        kaug_ref[rows, :] = jnp.concatenate([k, onehot], axis=1)


def _moba_prep(h0, *, k_col, n_pairs, per_step=8):
    s = h0.shape[0]
    nb = s // MOBA_BLOCK
    per_step = min(per_step, nb)
    assert nb % per_step == 0
    rows = per_step * MOBA_BLOCK
    return pl.pallas_call(
        functools.partial(_moba_prep_kernel, per_step=per_step),
        grid=(n_pairs, nb // per_step),
        in_specs=[pl.BlockSpec((rows, LANES), lambda p, c: (c, k_col + p))],
        out_specs=[
            pl.BlockSpec((rows, 2 * LANES), lambda p, c: (c, p)),
            pl.BlockSpec((1, per_step, LANES), lambda p, c: (p * (nb // per_step) + c, 0, 0)),
        ],
        out_shape=[
            jax.ShapeDtypeStruct((s, n_pairs * 2 * LANES), BF16),
            jax.ShapeDtypeStruct((n_pairs * nb // per_step, per_step, LANES), F32),
        ],
        compiler_params=_cparams(("arbitrary", "arbitrary")),
        name="moba_prep",
    )(h0)


def _moba_kernel(q_ref, kaug_ref, vt_ref, kmean_ref, o_ref, m_ref, acc_ref, st_ref, *, nb, g):
    t = MOBA_BLOCK
    i = pl.program_id(1)
    qf = q_ref[...].astype(F32)
    lo = _head_a(_iota((t, LANES), 1))
    qt = jnp.concatenate([jnp.where(lo, qf, 0.0).T, jnp.where(lo, 0.0, qf).T], axis=1).astype(BF16)
    kmean = kmean_ref[...]
    km_hi = kmean.astype(BF16)
    km_lo = (kmean - km_hi.astype(F32)).astype(BF16)
    gate = _dot(km_hi, qt) + _dot(km_lo, qt)
    blk = _iota((LANES, 2 * t), 0)
    blk_f = blk.astype(F32)
    gate = jnp.where(blk < i, gate, -jnp.inf)
    chosen = blk == i
    for _ in range(MOBA_TOPK):
        m = jnp.max(gate, axis=0, keepdims=True)
        first = jnp.min(jnp.where(gate == m, blk_f, float(LANES)), axis=0, keepdims=True)
        pick = blk_f == first
        chosen = chosen | (pick & (m > -jnp.inf))
        gate = jnp.where(pick, -jnp.inf, gate)
    bias = jnp.where(chosen | (blk >= nb), 0.0, NEG).astype(BF16)
    q_both = jnp.concatenate([qt, bias], axis=0)

    tk = g * t
    last = lax.shift_right_logical(i, int(np.log2(g)))
    ones = jnp.ones((ONES_ROWS, tk), BF16)

    def scores(u):
        start = pl.multiple_of(u * tk, tk)
        return _dot(kaug_ref[pl.ds(start, tk), :], q_both)

    m_ref[...] = jnp.full(m_ref.shape, NEG, F32)
    acc_ref[...] = jnp.zeros_like(acc_ref)
    key_pos = _iota((tk, 2 * t), 0) + (last * g - i) * t
    query_pos = _iota((tk, 2 * t), 1) & (t - 1)
    st_ref[...] = jnp.where(key_pos <= query_pos, scores(last), NEG)

    def body(n, carry):
        cur = jnp.where(n == 0, last, n - 1)
        nxt = jnp.minimum(n, jnp.maximum(last - 1, 0))
        st_next = scores(nxt)
        vt = jnp.concatenate([vt_ref[0, cur * g + b] for b in range(g)], axis=1)
        vt = jnp.concatenate([vt, ones], axis=0)
        st = st_ref[...]
        m_old = m_ref[...]
        m_new = jnp.maximum(m_old, jnp.max(st, axis=0, keepdims=True))
        alpha = jnp.exp2(m_old - m_new)
        pt = jnp.exp2(st - m_new).astype(BF16)
        acc_ref[...] = alpha * acc_ref[...] + _dot(vt, pt)
        m_ref[...] = m_new
        st_ref[...] = st_next
        return carry

    lax.fori_loop(0, last + 1, body, 0)
    acc = acc_ref[...]
    outs = [acc[:LANES, h * t:(h + 1) * t] / acc[LANES:LANES + 1, h * t:(h + 1) * t] for h in range(2)]
    first = _iota((LANES, t), 0) < HEAD_DIM
    o_ref[...] = jnp.where(first, outs[0], outs[1]).T.astype(o_ref.dtype)


def _moba_attention(h0, kaug, vt, kmean, *, q_col, n_pairs, g=4):
    s = h0.shape[0]
    t = MOBA_BLOCK
    nb = s // t
    g = min(g, nb)
    assert nb <= LANES and nb % g == 0
    return pl.pallas_call(
        functools.partial(_moba_kernel, nb=nb, g=g),
        grid=(n_pairs, nb),
        in_specs=[
            pl.BlockSpec((t, LANES), lambda p, i: (i, q_col + p)),
            pl.BlockSpec((s, 2 * LANES), lambda p, i: (0, p)),
            pl.BlockSpec((1, nb, LANES, t), lambda p, i: (p, 0, 0, 0)),
            pl.BlockSpec((LANES, LANES), lambda p, i: (p, 0)),
        ],
        out_specs=pl.BlockSpec((t, LANES), lambda p, i: (i, p)),
        out_shape=jax.ShapeDtypeStruct((s, n_pairs * LANES), BF16),
        scratch_shapes=[pltpu.VMEM((1, 2 * t), F32),
                        pltpu.VMEM((LANES + ONES_ROWS, 2 * t), F32),
                        pltpu.VMEM((g * t, 2 * t), F32)],
        compiler_params=_cparams(("arbitrary", "arbitrary")),
        name="moba",
    )(h0, kaug, vt, kmean)


def _dsa_kernel(it_ref, jt_ref, qi_ref, wi_ref, ki_ref, q_ref, k_ref, vt_ref, o_ref,
                key_ref, thr_ref, need_ref, seen_ref, m_ref, acc_ref, qt_ref,
                *, t, g, n_heads, topk):
    step = pl.program_id(0)
    i = it_ref[step]
    j = jt_ref[step]
    lo = _head_a(_iota((t, LANES), 1))
    key_le_query = _iota((t, t), 0) <= _iota((t, t), 1)
    n_pairs = n_heads // 2

    def split_t(pair):
        pf = pair.astype(F32)
        return (jnp.where(lo, pf, 0.0).T.astype(BF16), jnp.where(lo, 0.0, pf).T.astype(BF16))

    @pl.when(j == 0)
    def _select():
        qi = qi_ref[...]
        wi = wi_ref[0, 0].astype(F32)
        qit = [jnp.concatenate(split_t(qi[:, p * LANES:(p + 1) * LANES]), axis=1)
               for p in range(IDX_HEADS // 2)]
        q = q_ref[...]
        for p in range(n_pairs):
            qt_ref[p] = jnp.concatenate(split_t(q[:, p * LANES:(p + 1) * LANES]), axis=1)

        def products(kt):
            start = pl.multiple_of(kt * t, t)
            kblk = ki_ref[pl.ds(start, t), :]
            return [_dot(kblk, qit[p]) for p in range(IDX_HEADS // 2)]

        def keys_of(kt, logits):
            sc = jnp.zeros((t, t), F32)
            for h in range(IDX_HEADS):
                lg = logits[h // 2][:, (h % 2) * t:(h % 2 + 1) * t]
                sc = sc + wi[h:h + 1, :] * jnp.maximum(lg, 0.0)
            causal = jnp.logical_or(kt < i, jnp.logical_and(kt == i, key_le_query))
            key_ref[kt] = jnp.where(causal, sc, -jnp.inf)

        n_trips = (i + COUNT_UNROLL) // COUNT_UNROLL

        def score_trip(trip, carry):
            first = COUNT_UNROLL * trip
            logits = products(first)
            for u in range(COUNT_UNROLL):
                cur = logits
                if u + 1 < COUNT_UNROLL:
                    logits = products(first + u + 1)
                keys_of(first + u, cur)
            return carry

        lax.fori_loop(0, n_trips, score_trip, 0)
        for b in range(1, g + 1):
            key_ref[i + b] = jnp.full((t, t), -jnp.inf, F32)

        def count(pred):
            def cbody(trip, c):
                for kt in [COUNT_UNROLL * trip + u for u in range(COUNT_UNROLL)]:
                    hit = jnp.where(pred(key_ref[kt]), 1.0, 0.0)
                    c = c + hit.reshape(t // 8, 8, t).sum(axis=0)
                return c
            c = lax.fori_loop(0, n_trips, cbody, jnp.zeros((8, t), F32))
            return jnp.sum(c, axis=0, keepdims=True)

        def as_score(code):
            signed = code ^ jnp.int32(INT_MIN)
            bits = jnp.where(signed >= 0, signed, signed ^ jnp.int32(0x7FFFFFFF))
            return pltpu.bitcast(bits, F32)

        code = jnp.zeros((1, t), I32)
        for b in range(31, -1, -1):
            bit = jnp.int32(INT_MIN) if b == 31 else jnp.int32(1 << b)
            cand = as_score(code | bit)
            cnt = count(lambda ss, c=cand: ss >= c)
            code = jnp.where(cnt >= float(topk), code | bit, code)
        thr = jnp.where(code == 0, -jnp.inf, as_score(code))
        n_gt = count(lambda ss: ss > thr)
        thr_ref[...] = thr
        need_ref[...] = float(topk) - n_gt
        seen_ref[...] = jnp.zeros_like(seen_ref)
        m_ref[...] = jnp.full(m_ref.shape, NEG, F32)
        acc_ref[...] = jnp.zeros_like(acc_ref)

    thr = thr_ref[...]
    need = need_ref[...]
    earlier = (_iota((t, t), 1) < _iota((t, t), 0)).astype(BF16)
    seen = seen_ref[...]
    biases = []
    for b in range(g):
        blk = j * g + b
        key = key_ref[blk]
        eq = key == thr
        rank = seen + _dot(earlier, jnp.where(eq, 1.0, 0.0).astype(BF16))
        sel = (key > thr) | (eq & (rank < need))
        sel = sel & jnp.logical_or(blk < i, jnp.logical_and(blk == i, key_le_query))
        seen = seen + jnp.sum(jnp.where(eq, 1.0, 0.0), axis=0, keepdims=True)
        biases.append(jnp.where(sel, 0.0, NEG))
    seen_ref[...] = seen
    bias = jnp.concatenate(biases, axis=0)
    bias = jnp.concatenate([bias, bias], axis=1)

    kb = k_ref[...]
    ones = jnp.ones((ONES_ROWS, g * t), BF16)

    def scores(p):
        return _dot(kb[:, p * LANES:(p + 1) * LANES], qt_ref[p]) + bias

    def softmax(p, st):
        m_old = m_ref[p]
        m_new = jnp.maximum(m_old, jnp.max(st, axis=0, keepdims=True))
        m_ref[p] = m_new
        return jnp.exp2(m_old - m_new), jnp.exp2(st - m_new).astype(BF16)

    def accumulate(p, alpha, pt):
        vt = jnp.concatenate([vt_ref[p, b] for b in range(g)], axis=1)
        vt = jnp.concatenate([vt, ones], axis=0)
        acc_ref[p] = alpha * acc_ref[p] + _dot(vt, pt)

    st_next = scores(0)
    for p in range(n_pairs):
        st = st_next
        if p + 1 < n_pairs:
            st_next = scores(p + 1)
        accumulate(p, *softmax(p, st))

    @pl.when(j == lax.shift_right_logical(i, int(np.log2(g))))
    def _finish():
        first = _iota((LANES, t), 0) < HEAD_DIM
        for p in range(n_pairs):
            acc = acc_ref[p]
            a0, a1 = acc[:, :t], acc[:, t:]
            out_t = jnp.where(first, a0[:LANES] / a0[LANES:LANES + 1], a1[:LANES] / a1[LANES:LANES + 1])
            o_ref[:, p * LANES:(p + 1) * LANES] = out_t.T.astype(o_ref.dtype)


def _dsa_attention(h1, vt, *, n_heads, t=256, g=4):
    s = h1.shape[0]
    width = n_heads * HEAD_DIM
    nt = s // t
    g = min(g, nt)
    assert nt % g == 0
    topk = min(DSA_TOPK, s // 4)
    iw = IDX_HEADS * HEAD_DIM
    qi_blk = 2 * width // iw
    ki_blk = (2 * width + iw) // LANES
    wi_grp = width // LANES
    tri = [(i, j) for i in range(nt) for j in range(i // g + 1)]
    it = jnp.asarray([a for a, _ in tri], I32)
    jt = jnp.asarray([b for _, b in tri], I32)
    grid_spec = pltpu.PrefetchScalarGridSpec(
        num_scalar_prefetch=2,
        grid=(len(tri),),
        in_specs=[
            pl.BlockSpec((t, iw), lambda n, it, jt: (it[n], qi_blk)),
            pl.BlockSpec((1, 1, ONES_ROWS, t), lambda n, it, jt: (wi_grp, it[n], 0, 0)),
            pl.BlockSpec((s, LANES), lambda n, it, jt: (0, ki_blk)),
            pl.BlockSpec((t, width), lambda n, it, jt: (it[n], 0)),
            pl.BlockSpec((g * t, width), lambda n, it, jt: (jt[n], 1)),
            pl.BlockSpec((n_heads // 2, g, LANES, t), lambda n, it, jt: (0, jt[n], 0, 0)),
        ],
        out_specs=pl.BlockSpec((t, width), lambda n, it, jt: (it[n], 0)),
        scratch_shapes=[
            pltpu.VMEM((nt + g, t, t), F32),
            pltpu.VMEM((1, t), F32),
            pltpu.VMEM((1, t), F32),
            pltpu.VMEM((1, t), F32),
            pltpu.VMEM((n_heads // 2, 1, 2 * t), F32),
            pltpu.VMEM((n_heads // 2, LANES + ONES_ROWS, 2 * t), F32),
            pltpu.VMEM((n_heads // 2, LANES, 2 * t), BF16),
        ],
    )
    return pl.pallas_call(
        functools.partial(_dsa_kernel, t=t, g=g, n_heads=n_heads, topk=topk),
        grid_spec=grid_spec,
        out_shape=jax.ShapeDtypeStruct((s, width), BF16),
        compiler_params=_cparams(("arbitrary",)),
        name="dsa",
    )(it, jt, h1, vt, h1, h1, h1, vt)


def _layer_norm(y, g, b):
    mu = jnp.mean(y, axis=-1, keepdims=True)
    yc = y - mu
    var = jnp.mean(yc * yc, axis=-1, keepdims=True)
    return yc * lax.rsqrt(var + LN_EPS) * g + b


def _outproj_kernel(*refs, n_in, alpha):
    o_refs = refs[:n_in]
    w_refs = refs[n_in:2 * n_in]
    x_ref, g_ref, b_ref, out_ref = refs[2 * n_in:]
    y = alpha * x_ref[...]
    for o_ref, w_ref in zip(o_refs, w_refs):
        y = y + _dot(o_ref[...], w_ref[...])
    out_ref[...] = _layer_norm(y, g_ref[...], b_ref[...])


def _outproj_ln(os_, ws, x, g, b, alpha, *, tm=512):
    s, d = x.shape
    n_in = len(os_)
    in_specs = ([pl.BlockSpec((tm, o.shape[1]), lambda i: (i, 0)) for o in os_]
                + [pl.BlockSpec(w.shape, lambda i: (0, 0)) for w in ws]
                + [pl.BlockSpec((tm, d), lambda i: (i, 0)),
                   pl.BlockSpec((1, d), lambda i: (0, 0)),
                   pl.BlockSpec((1, d), lambda i: (0, 0))])
    return pl.pallas_call(
        functools.partial(_outproj_kernel, n_in=n_in, alpha=alpha),
        grid=(s // tm,),
        in_specs=in_specs,
        out_specs=pl.BlockSpec((tm, d), lambda i: (i, 0)),
        out_shape=jax.ShapeDtypeStruct((s, d), F32),
        compiler_params=_cparams(("arbitrary",)),
        name="out_proj_ln",
    )(*os_, *ws, x, g, b)


def _moe_kernel(x_ref, wr_ref, br_ref, w1g_ref, w1u_ref, b1g_ref, b1u_ref, w2_ref, b2_ref,
                g_ref, b_ref, out_ref,
                xb_ref, rank_ref, rank_t_ref, gate_t_ref, cnt_ref, acc_ref,
                *, tt, ch, alpha):
    e = pl.program_id(1)
    n_e = pl.num_programs(1)

    @pl.when(e == 0)
    def _route():
        x = x_ref[...]
        xb_ref[...] = x.astype(BF16)
        logits = lax.dot_general(wr_ref[...], x, (((1,), (1,)), ((), ())),
                                 precision=lax.Precision.HIGHEST,
                                 preferred_element_type=F32) + br_ref[...]
        row = _iota(logits.shape, 0).astype(F32)
        work = jnp.where(row < N_EXPERTS, logits, -jnp.inf)
        chosen = row < 0
        top = None
        for r in range(TOP_K):
            m = jnp.max(work, axis=0, keepdims=True)
            idx = jnp.min(jnp.where(work == m, row, float(LANES)), axis=0, keepdims=True)
            pick = row == idx
            chosen = chosen | pick
            work = jnp.where(pick, -jnp.inf, work)
            if r == 0:
                top = m
        ex = jnp.where(chosen, jnp.exp(logits - top), 0.0)
        gate = ex / jnp.sum(ex, axis=0, keepdims=True)
        sel16 = jnp.where(chosen, 1.0, 0.0).astype(BF16)
        before = (_iota((tt, tt), 0) < _iota((tt, tt), 1)).astype(BF16)
        pos = _dot(sel16, before)
        rank = jnp.where(chosen, pos, -1.0)
        rank_ref[...] = rank
        rank_t_ref[...] = rank.T
        gate_t_ref[...] = gate.T
        cnt = jnp.sum(jnp.where(chosen, 1.0, 0.0), axis=1, keepdims=True)
        cnt_ref[...] = jnp.broadcast_to(cnt, cnt_ref.shape)
        acc_ref[...] = jnp.zeros_like(acc_ref)

    lane = _iota((tt, LANES), 1)
    on_e = lane == e
    rank_col = jnp.sum(jnp.where(on_e, rank_t_ref[...], 0.0), axis=1, keepdims=True)
    gate_col = jnp.sum(jnp.where(on_e, gate_t_ref[...], 0.0), axis=1, keepdims=True)
    rank_row = rank_ref[pl.ds(e, 1), :]
    n_rows = jnp.max(cnt_ref[pl.ds(e, 1), :]).astype(I32)
    n_chunks = (n_rows + (ch - 1)) // ch
    pad_rows = jnp.zeros((2 * LANES - ch, x_ref.shape[1]), BF16)
    slot = _iota((1, 2 * LANES), 1)
    slot = jnp.where(slot < ch, slot.astype(F32), -1e9)
    b1g = b1g_ref[0]
    b1u = b1u_ref[0]
    b2 = b2_ref[0]

    def chunk(c, carry):
        w1g = w1g_ref[0]
        w1u = w1u_ref[0]
        w2 = w2_ref[0]
        r0 = (c * ch).astype(F32)
        gather = (rank_row == (r0 + _iota((ch, tt), 0).astype(F32))).astype(BF16)
        xg = _dot(gather, xb_ref[...]).astype(BF16)
        hg = jnp.minimum(_dot(xg, w1g) + b1g, SWIGLU_LIMIT)
        hu = jnp.clip(_dot(xg, w1u) + b1u, -SWIGLU_LIMIT, SWIGLU_LIMIT)
        act = (hu + 1.0) * (hg * jax.nn.sigmoid(SWIGLU_ALPHA * hg))
        y = _dot(act.astype(BF16), w2) + b2
        scatter = jnp.where(rank_col == (r0 + slot), gate_col, 0.0).astype(BF16)
        acc_ref[...] += _dot(scatter, jnp.concatenate([y.astype(BF16), pad_rows], axis=0))
        return carry

    lax.fori_loop(0, n_chunks, chunk, 0)

    @pl.when(e == n_e - 1)
    def _finish():
        out_ref[...] = _layer_norm(alpha * x_ref[...] + acc_ref[...], g_ref[...], b_ref[...])


def _moe_ln(x, wr_t, br, w1g, w1u, b1g, b1u, w2, b2, g, b, alpha, *, tt=1024, ch=160):
    t, d = x.shape
    tt = min(tt, t)
    assert ch % 16 == 0 and ch <= 2 * LANES
    f = w2.shape[1]
    return pl.pallas_call(
        functools.partial(_moe_kernel, tt=tt, ch=ch, alpha=alpha),
        grid=(t // tt, N_EXPERTS),
        in_specs=[
            pl.BlockSpec((tt, d), lambda i, e: (i, 0)),
            pl.BlockSpec((LANES, d), lambda i, e: (0, 0)),
            pl.BlockSpec((LANES, 1), lambda i, e: (0, 0)),
            pl.BlockSpec((1, d, f), lambda i, e: (e, 0, 0)),
            pl.BlockSpec((1, d, f), lambda i, e: (e, 0, 0)),
            pl.BlockSpec((1, 1, f), lambda i, e: (e, 0, 0)),
            pl.BlockSpec((1, 1, f), lambda i, e: (e, 0, 0)),
            pl.BlockSpec((1, f, d), lambda i, e: (e, 0, 0)),
            pl.BlockSpec((1, 1, d), lambda i, e: (e, 0, 0)),
            pl.BlockSpec((1, d), lambda i, e: (0, 0)),
            pl.BlockSpec((1, d), lambda i, e: (0, 0)),
        ],
        out_specs=pl.BlockSpec((tt, d), lambda i, e: (i, 0)),
        out_shape=jax.ShapeDtypeStruct((t, d), F32),
        scratch_shapes=[
            pltpu.VMEM((tt, d), BF16),
            pltpu.VMEM((LANES, tt), F32),
            pltpu.VMEM((tt, LANES), F32),
            pltpu.VMEM((tt, LANES), F32),
            pltpu.VMEM((LANES, LANES), F32),
            pltpu.VMEM((tt, d), F32),
        ],
        compiler_params=_cparams(("arbitrary", "arbitrary")),
        name="moe_ln",
    )(x, wr_t, br, w1g, w1u, b1g, b1u, w2, b2, g, b)


def _ple_kernel(x_ref, p_ref, wg_ref, wp_ref, o_ref):
    x = x_ref[...]
    gate = jax.nn.sigmoid(_dot(x.astype(BF16), wg_ref[...]))
    emb = _dot(p_ref[...].astype(BF16), wp_ref[...])
    o_ref[...] = x + gate * emb


def _ple(x, p, wg, wp, *, tm=512):
    s, d = x.shape
    dp = p.shape[1]
    return pl.pallas_call(
        _ple_kernel,
        grid=(s // tm,),
        in_specs=[
            pl.BlockSpec((tm, d), lambda i: (i, 0)),
            pl.BlockSpec((tm, dp), lambda i: (i, 0)),
            pl.BlockSpec((d, d), lambda i: (0, 0)),
            pl.BlockSpec((dp, d), lambda i: (0, 0)),
        ],
        out_specs=pl.BlockSpec((tm, d), lambda i: (i, 0)),
        out_shape=jax.ShapeDtypeStruct((s, d), F32),
        compiler_params=_cparams(("arbitrary",)),
        name="ple",
    )(x, p, wg, wp)


def _mixer_ab(x, w_in, cos_t, sin_t, *, sb_heads, moba_heads):
    s = x.shape[0]
    wa = sb_heads * HEAD_DIM
    wb = moba_heads * HEAD_DIM
    qa, ka, va, qb, kb, vb = jnp.split(w_in, np.cumsum([wa, wa, wa, wb, wb]).tolist(), axis=1)
    rot = _rotary_layout(wb)
    w = jnp.concatenate([qb[:, rot], kb[:, rot], qa, ka, va], axis=1).astype(BF16)
    att = HEAD_DIM ** -0.5
    scale = jnp.concatenate([jnp.full((wb,), att * LOG2E), jnp.ones((wb,)), jnp.full((wa,), att),
                             jnp.ones((2 * wa,))]).astype(F32)[None]
    tn = cos_t.shape[1]
    h0 = _proj(x, w, scale, cos_t, sin_t, ((0, 2 * wb // tn),), tn=tn)
    n_pairs = moba_heads // 2
    vt = _proj_t(x, vb.astype(BF16), jnp.ones((1, wb), F32), groups=n_pairs)
    c = lambda off: off // LANES
    o_a = _sb_attention(h0, q_col=c(2 * wb), k_col=c(2 * wb + wa), v_col=c(2 * wb + 2 * wa),
                        n_pairs=sb_heads // 2)
    nb = s // MOBA_BLOCK
    kaug, kmean = _moba_prep(h0, k_col=c(wb), n_pairs=n_pairs)
    kmean = jnp.pad(kmean.reshape(n_pairs, nb, LANES), ((0, 0), (0, LANES - nb), (0, 0)))
    kmean = kmean.reshape(n_pairs * LANES, LANES)
    o_b = _moba_attention(h0, kaug, vt, kmean, q_col=0, n_pairs=n_pairs)
    return o_a, o_b


def _mixer_c(x, w_in, cos_t, sin_t, *, n_heads):
    width = n_heads * HEAD_DIM
    iw = IDX_HEADS * HEAD_DIM
    d = w_in.shape[0]
    q, k, v, qi, ki, wi = jnp.split(w_in, np.cumsum([width, width, width, iw, HEAD_DIM]).tolist(), axis=1)
    tn = cos_t.shape[1]
    half = HEAD_DIM // 2
    ki_pad = jnp.zeros((d, tn - 2 * HEAD_DIM), w_in.dtype)
    w = jnp.concatenate([q[:, _rotary_layout(width)], k[:, _rotary_layout(width)], qi[:, _rotary_layout(iw)],
                         ki[:, :half], ki[:, :half], ki[:, half:], ki[:, half:], ki_pad], axis=1).astype(BF16)
    att = HEAD_DIM ** -0.5
    scale = jnp.concatenate([jnp.full((width,), att * LOG2E), jnp.ones((width,)), jnp.full((iw,), att),
                             jnp.ones((tn,))]).astype(F32)[None]
    h1 = _proj(x, w, scale, cos_t, sin_t, ((0, (2 * width + iw) // tn + 1),), tn=tn)
    wi_pad = jnp.zeros((d, LANES - IDX_HEADS), w_in.dtype)
    wv = jnp.concatenate([v, wi, wi_pad], axis=1).astype(BF16)
    vscale = jnp.concatenate([jnp.ones((width,)), jnp.full((LANES,), IDX_HEADS ** -0.5)]).astype(F32)[None]
    vt = _proj_t(x, wv, vscale, groups=3)
    return _dsa_attention(h1, vt, n_heads=n_heads)


def _split_kernel(w_ref, perm_ref, g_ref, u_ref):
    half = g_ref.shape[2]
    both = _dot(w_ref[0, 0].astype(BF16), perm_ref[...])
    g_ref[0] = both[:, :half].astype(BF16)
    u_ref[0] = both[:, half:].astype(BF16)


def _split_gate_up(w1_all, layer, *, tc=512):
    _, n_e, d, f2 = w1_all.shape
    half = tc // 2
    src = jnp.concatenate([2 * jnp.arange(half), 2 * jnp.arange(half) + 1])
    perm = (jnp.arange(tc)[:, None] == src[None, :]).astype(BF16)
    out = jax.ShapeDtypeStruct((n_e, d, f2 // 2), BF16)
    return pl.pallas_call(
        _split_kernel,
        grid=(n_e, f2 // tc),
        in_specs=[pl.BlockSpec((1, 1, d, tc), lambda e, c: (layer, e, 0, c)),
                  pl.BlockSpec((tc, tc), lambda e, c: (0, 0))],
        out_specs=[pl.BlockSpec((1, d, half), lambda e, c: (e, 0, c)),
                   pl.BlockSpec((1, d, half), lambda e, c: (e, 0, c))],
        out_shape=[out, out],
        compiler_params=_cparams(("arbitrary", "arbitrary")),
        name="split_gate_up",
    )(w1_all, perm)


def _cast_kernel(w_ref, o_ref):
    o_ref[0] = w_ref[0, 0].astype(o_ref.dtype)


def _cast_layer(w_all, layer):
    _, n_e, f, d = w_all.shape
    return pl.pallas_call(
        _cast_kernel,
        grid=(n_e,),
        in_specs=[pl.BlockSpec((1, 1, f, d), lambda e: (layer, e, 0, 0))],
        out_specs=pl.BlockSpec((1, f, d), lambda e: (e, 0, 0)),
        out_shape=jax.ShapeDtypeStruct((n_e, f, d), BF16),
        compiler_params=_cparams(("arbitrary",)),
        name="cast_w2",
    )(w_all)


def _moe_weights(w_r, b_r, w1_all, b1, w2_all, b2, layer):
    wr_t = jnp.pad(w_r.T, ((0, LANES - N_EXPERTS), (0, 0)))
    br = jnp.pad(b_r, (0, LANES - N_EXPERTS))[:, None]
    w1g, w1u = _split_gate_up(w1_all, layer)
    return (wr_t, br, w1g, w1u, b1[:, None, 0::2], b1[:, None, 1::2], _cast_layer(w2_all, layer),
            b2[:, None, :])


def kernel(x, p, ab_w_in, ab_w_out, c_w_in, c_w_out, ln_g, ln_b, router_w, router_b,
           moe_w1, moe_b1, moe_w2, moe_b2, ple_w_proj, ple_w_gate):
    b, s, d = x.shape
    assert b == 1
    depth = p.shape[0]
    alpha = float((2 * depth) ** 0.25)
    cos_t, sin_t = _rope_tables(s, 4 * LANES)
    sb_heads = moba_heads = ab_w_out.shape[1] // (2 * HEAD_DIM)
    dsa_heads = c_w_out.shape[1] // HEAD_DIM
    xs = x[0]
    for i in range(depth):
        j = i // 2
        g = ln_g[i][:, None, :]
        bb = ln_b[i][:, None, :]
        if i % 2 == 0:
            o_a, o_b = _mixer_ab(xs, ab_w_in[j], cos_t, sin_t, sb_heads=sb_heads, moba_heads=moba_heads)
            w_out = ab_w_out[j].astype(BF16)
            wa = sb_heads * HEAD_DIM
            xs = _outproj_ln([o_a, o_b], [w_out[:wa], w_out[wa:]], xs, g[0], bb[0], alpha)
        else:
            o_c = _mixer_c(xs, c_w_in[j], cos_t, sin_t, n_heads=dsa_heads)
            xs = _outproj_ln([o_c], [c_w_out[j].astype(BF16)], xs, g[0], bb[0], alpha)
        mw = _moe_weights(router_w[i], router_b[i], moe_w1, moe_b1[i], moe_w2, moe_b2[i], i)
        xs = _moe_ln(xs, *mw, g[1], bb[1], alpha)
        xs = _ple(xs, p[i, 0], ple_w_gate[i].astype(BF16), ple_w_proj[i].astype(BF16))
    return xs[None]
```

```python
import functools

import jax
import jax.numpy as jnp
import numpy as np
from jax import lax
from jax.experimental import pallas as pl
from jax.experimental.pallas import tpu as pltpu

F32 = jnp.float32
BF16 = jnp.bfloat16
I32 = jnp.int32

HEAD_DIM = 64
LANES = 128
MOBA_BLOCK = 256
MOBA_TOPK = 3
DSA_TOPK = 256
IDX_HEADS = 8
N_EXPERTS = 32
TOP_K = 4
SWIGLU_LIMIT = 7.0
SWIGLU_ALPHA = 1.702
ROPE_THETA = 10000.0
LN_EPS = 1e-5
NEG = -1e30
EXP_FLUSH = -88.0
INT_MIN = -(2 ** 31)
COUNT_UNROLL = 4
LOG2E = 1.4426950408889634
ONES_ROWS = 16
VMEM_LIMIT = 56 * 1024 * 1024


def _cparams(sem, flags=None):
    return pltpu.CompilerParams(dimension_semantics=sem, vmem_limit_bytes=VMEM_LIMIT, flags=flags)


def _dot(a, b):
    return jnp.dot(a, b, preferred_element_type=F32)


def _dot_nt(a, b):
    return lax.dot_general(a, b, (((1,), (1,)), ((), ())), preferred_element_type=F32)


def _dot_tn(a, b):
    return lax.dot_general(a, b, (((0,), (0,)), ((), ())), preferred_element_type=F32)


def _iota(shape, dim):
    return lax.broadcasted_iota(I32, shape, dim)


def _proj_kernel(x_ref, w_ref, scale_ref, cos_ref, sin_ref, o_ref, xb_ref, *, rope_ranges, tn):
    j = pl.program_id(1)

    @pl.when(j == 0)
    def _():
        xb_ref[...] = x_ref[...].astype(BF16)

    h = _dot(xb_ref[...], w_ref[...]) * scale_ref[...]
    is_rope = functools.reduce(jnp.logical_or, [(j >= a) & (j < b) for a, b in rope_ranges])

    @pl.when(is_rope)
    def _():
        for c in range(tn // LANES):
            sl = slice(c * LANES, (c + 1) * LANES)
            hs = h[:, sl]
            partner = pltpu.roll(hs, LANES // 2, 1)
            o_ref[:, sl] = (hs * cos_ref[:, sl] + partner * sin_ref[:, sl]).astype(o_ref.dtype)

    @pl.when(jnp.logical_not(is_rope))
    def _():
        o_ref[...] = h.astype(o_ref.dtype)


def _proj(x, w, scale, cos_t, sin_t, rope_ranges, *, tm=512, tn=256):
    s, d = x.shape
    n = w.shape[1]
    return pl.pallas_call(
        functools.partial(_proj_kernel, rope_ranges=rope_ranges, tn=tn),
        grid=(s // tm, n // tn),
        in_specs=[
            pl.BlockSpec((tm, d), lambda i, j: (i, 0)),
            pl.BlockSpec((d, tn), lambda i, j: (0, j)),
            pl.BlockSpec((1, tn), lambda i, j: (0, j)),
            pl.BlockSpec((tm, tn), lambda i, j: (i, 0)),
            pl.BlockSpec((tm, tn), lambda i, j: (i, 0)),
        ],
        out_specs=pl.BlockSpec((tm, tn), lambda i, j: (i, j)),
        out_shape=jax.ShapeDtypeStruct((s, n), BF16),
        scratch_shapes=[pltpu.VMEM((tm, d), BF16)],
        compiler_params=_cparams(("arbitrary", "arbitrary")),
        name="in_proj",
    )(x, w, scale, cos_t, sin_t)


def _proj_t_kernel(x_ref, w_ref, scale_ref, o_ref, *, groups, tiles, tm):
    h = _dot(x_ref[...].astype(BF16), w_ref[...]) * scale_ref[...]
    for g in range(groups):
        for r in range(tiles):
            o_ref[g, r] = h[r * tm:(r + 1) * tm, g * LANES:(g + 1) * LANES].T.astype(o_ref.dtype)


def _proj_t(x, w, scale, *, groups, tm=256, tiles=4):
    s, d = x.shape
    n = w.shape[1]
    tc = groups * LANES
    tiles = min(tiles, s // tm)
    return pl.pallas_call(
        functools.partial(_proj_t_kernel, groups=groups, tiles=tiles, tm=tm),
        grid=(s // (tiles * tm), n // tc),
        in_specs=[
            pl.BlockSpec((tiles * tm, d), lambda i, j: (i, 0)),
            pl.BlockSpec((d, tc), lambda i, j: (0, j)),
            pl.BlockSpec((1, tc), lambda i, j: (0, j)),
        ],
        out_specs=pl.BlockSpec((groups, tiles, LANES, tm), lambda i, j: (j, i, 0, 0)),
        out_shape=jax.ShapeDtypeStruct((n // LANES, s // tm, LANES, tm), BF16),
        compiler_params=_cparams(("arbitrary", "arbitrary")),
        name="proj_t",
    )(x, w, scale)


def _rope_tables(s, width):
    half = HEAD_DIM // 2
    inv = ROPE_THETA ** (-jnp.arange(half, dtype=F32) / half)
    ang = jnp.arange(s, dtype=F32)[:, None] * inv[None, :]
    cos = jnp.cos(ang)
    sin = jnp.sin(ang)
    reps = width // LANES
    cos_t = jnp.tile(jnp.concatenate([cos, cos, cos, cos], axis=1), (1, reps))
    sin_t = jnp.tile(jnp.concatenate([-sin, -sin, sin, sin], axis=1), (1, reps))
    return cos_t, sin_t


def _rotary_layout(n_cols):
    half = HEAD_DIM // 2
    lane = np.arange(n_cols)
    base = lane // LANES * LANES
    within = lane % LANES
    head = within // half % 2
    part = within // HEAD_DIM
    return base + head * HEAD_DIM + part * half + within % half


def _head_a(lane):
    return (lane % HEAD_DIM) < (HEAD_DIM // 2)


def _sb_kernel(q_ref, k_ref, v_ref, o_ref, acc_ref, run_ref, *, t, pp):
    i = pl.program_id(1)
    lane = _iota((t, LANES), 1)
    lo = lane < HEAD_DIM
    qh = []
    for p in range(pp):
        q = q_ref[:, p * LANES:(p + 1) * LANES]
        zero = jnp.zeros_like(q)
        qh.extend([jnp.where(lo, q, zero), jnp.where(lo, zero, q)])
    row = _iota((t, t), 0)
    col = _iota((t, t), 1)
    incl = (row >= col).astype(BF16)
    ones = jnp.ones((t, LANES), BF16)
    strict = col < row

    acc_ref[...] = jnp.zeros_like(acc_ref)
    run_ref[...] = jnp.zeros_like(run_ref)

    def tile(kt, diagonal):
        start = pl.multiple_of(kt * t, t)
        heads = range(2 * pp)
        sls = [slice((h // 2) * LANES, (h // 2 + 1) * LANES) for h in heads]
        zs = [_dot_nt(qh[h], k_ref[pl.ds(start, t), sls[h]]) for h in heads]
        ls16s = []
        for h in heads:
            log_stay = -(jnp.maximum(zs[h], 0.0) + jnp.log1p(jnp.exp(-jnp.abs(zs[h]))))
            if diagonal:
                log_stay = jnp.where(strict, log_stay, 0.0)
            ls16s.append(log_stay.astype(BF16))
        cums = [_dot(ls16s[h], incl) for h in heads]
        tots = [_dot(ls16s[h], ones) for h in heads]
        a16s = []
        for h in heads:
            a = jnp.exp(zs[h] + cums[h] + run_ref[h])
            if diagonal:
                a = jnp.where(strict, a, 0.0)
            a16s.append(a.astype(BF16))
        pvs = [_dot(a16s[h], v_ref[pl.ds(start, t), sls[h]]) for h in heads]
        for h in heads:
            acc_ref[h] += pvs[h]
            run_ref[h] += tots[h]

    tile(i, True)

    def cond(c):
        kt, live = c
        return jnp.logical_and(kt >= 0, live)

    def body(c):
        kt, _ = c
        tile(kt, False)
        live = jnp.max(run_ref[...]) > EXP_FLUSH
        return kt - 1, live

    lax.while_loop(cond, body, (i - 1, jnp.bool_(True)))
    for p in range(pp):
        o_ref[:, p * LANES:(p + 1) * LANES] = jnp.where(lo, acc_ref[2 * p], acc_ref[2 * p + 1]).astype(o_ref.dtype)


def _sb_attention(h0, *, q_col, k_col, v_col, n_pairs, t=128, pp=4):
    s = h0.shape[0]
    assert n_pairs % pp == 0 and q_col % pp == 0 and k_col % pp == 0 and v_col % pp == 0
    w = pp * LANES
    resident = pl.Buffered(1)
    return pl.pallas_call(
        functools.partial(_sb_kernel, t=t, pp=pp),
        grid=(n_pairs // pp, s // t),
        in_specs=[
            pl.BlockSpec((t, w), lambda p, i: (i, q_col // pp + p)),
            pl.BlockSpec((s, w), lambda p, i: (0, k_col // pp + p), pipeline_mode=resident),
            pl.BlockSpec((s, w), lambda p, i: (0, v_col // pp + p), pipeline_mode=resident),
        ],
        out_specs=pl.BlockSpec((t, w), lambda p, i: (i, p)),
        out_shape=jax.ShapeDtypeStruct((s, n_pairs * LANES), BF16),
        scratch_shapes=[pltpu.VMEM((2 * pp, t, LANES), F32), pltpu.VMEM((2 * pp, t, LANES), F32)],
        compiler_params=_cparams(("arbitrary", "arbitrary")),
        name="stick_breaking",
    )(h0, h0, h0)


def _moba_prep_kernel(k_ref, kaug_ref, kmean_ref, *, per_step):
    c = pl.program_id(1)
    for u in range(per_step):
        rows = slice(u * MOBA_BLOCK, (u + 1) * MOBA_BLOCK)
        k = k_ref[rows, :]
        kmean_ref[0, u:u + 1, :] = jnp.mean(k.astype(F32), axis=0, keepdims=True)
        onehot = (_iota(k.shape, 1) == c * per_step + u).astype(BF16)
        kaug_ref[rows, :] = jnp.concatenate([k, onehot], axis=1)


def _moba_prep(h0, *, k_col, n_pairs, per_step=8):
    s = h0.shape[0]
    nb = s // MOBA_BLOCK
    per_step = min(per_step, nb)
    assert nb % per_step == 0
    rows = per_step * MOBA_BLOCK
    return pl.pallas_call(
        functools.partial(_moba_prep_kernel, per_step=per_step),
        grid=(n_pairs, nb // per_step),
        in_specs=[pl.BlockSpec((rows, LANES), lambda p, c: (c, k_col + p))],
        out_specs=[
            pl.BlockSpec((rows, 2 * LANES), lambda p, c: (c, p)),
            pl.BlockSpec((1, per_step, LANES), lambda p, c: (p * (nb // per_step) + c, 0, 0)),
        ],
        out_shape=[
            jax.ShapeDtypeStruct((s, n_pairs * 2 * LANES), BF16),
            jax.ShapeDtypeStruct((n_pairs * nb // per_step, per_step, LANES), F32),
        ],
        compiler_params=_cparams(("arbitrary", "arbitrary")),
        name="moba_prep",
    )(h0)


def _moba_kernel(q_ref, kaug_ref, vt_ref, kmean_ref, o_ref, m_ref, acc_ref, st_ref, *, nb, g):
    t = MOBA_BLOCK
    i = pl.program_id(1)
    qf = q_ref[...].astype(F32)
    lo = _head_a(_iota((t, LANES), 1))
    qt = jnp.concatenate([jnp.where(lo, qf, 0.0).T, jnp.where(lo, 0.0, qf).T], axis=1).astype(BF16)
    kmean = kmean_ref[...]
    km_hi = kmean.astype(BF16)
    km_lo = (kmean - km_hi.astype(F32)).astype(BF16)
    gate = _dot(km_hi, qt) + _dot(km_lo, qt)
    blk = _iota((LANES, 2 * t), 0)
    blk_f = blk.astype(F32)
    gate = jnp.where(blk < i, gate, -jnp.inf)
    chosen = blk == i
    for _ in range(MOBA_TOPK):
        m = jnp.max(gate, axis=0, keepdims=True)
        first = jnp.min(jnp.where(gate == m, blk_f, float(LANES)), axis=0, keepdims=True)
        pick = blk_f == first
        chosen = chosen | (pick & (m > -jnp.inf))
        gate = jnp.where(pick, -jnp.inf, gate)
    bias = jnp.where(chosen | (blk >= nb), 0.0, NEG).astype(BF16)
    q_both = jnp.concatenate([qt, bias], axis=0)

    tk = g * t
    last = lax.shift_right_logical(i, int(np.log2(g)))
    ones = jnp.ones((ONES_ROWS, tk), BF16)

    def scores(u):
        start = pl.multiple_of(u * tk, tk)
        return _dot(kaug_ref[pl.ds(start, tk), :], q_both)

    m_ref[...] = jnp.full(m_ref.shape, NEG, F32)
    acc_ref[...] = jnp.zeros_like(acc_ref)
    key_pos = _iota((tk, 2 * t), 0) + (last * g - i) * t
    query_pos = _iota((tk, 2 * t), 1) & (t - 1)
    st_ref[...] = jnp.where(key_pos <= query_pos, scores(last), NEG)

    def body(n, carry):
        cur = jnp.where(n == 0, last, n - 1)
        nxt = jnp.minimum(n, jnp.maximum(last - 1, 0))
        st_next = scores(nxt)
        vt = jnp.concatenate([vt_ref[0, cur * g + b] for b in range(g)], axis=1)
        vt = jnp.concatenate([vt, ones], axis=0)
        st = st_ref[...]
        m_old = m_ref[...]
        m_new = jnp.maximum(m_old, jnp.max(st, axis=0, keepdims=True))
        alpha = jnp.exp2(m_old - m_new)
        pt = jnp.exp2(st - m_new).astype(BF16)
        acc_ref[...] = alpha * acc_ref[...] + _dot(vt, pt)
        m_ref[...] = m_new
        st_ref[...] = st_next
        return carry

    lax.fori_loop(0, last + 1, body, 0)
    acc = acc_ref[...]
    outs = [acc[:LANES, h * t:(h + 1) * t] / acc[LANES:LANES + 1, h * t:(h + 1) * t] for h in range(2)]
    first = _iota((LANES, t), 0) < HEAD_DIM
    o_ref[...] = jnp.where(first, outs[0], outs[1]).T.astype(o_ref.dtype)


def _moba_attention(h0, kaug, vt, kmean, *, q_col, n_pairs, g=4):
    s = h0.shape[0]
    t = MOBA_BLOCK
    nb = s // t
    g = min(g, nb)
    assert nb <= LANES and nb % g == 0
    return pl.pallas_call(
        functools.partial(_moba_kernel, nb=nb, g=g),
        grid=(n_pairs, nb),
        in_specs=[
            pl.BlockSpec((t, LANES), lambda p, i: (i, q_col + p)),
            pl.BlockSpec((s, 2 * LANES), lambda p, i: (0, p)),
            pl.BlockSpec((1, nb, LANES, t), lambda p, i: (p, 0, 0, 0)),
            pl.BlockSpec((LANES, LANES), lambda p, i: (p, 0)),
        ],
        out_specs=pl.BlockSpec((t, LANES), lambda p, i: (i, p)),
        out_shape=jax.ShapeDtypeStruct((s, n_pairs * LANES), BF16),
        scratch_shapes=[pltpu.VMEM((1, 2 * t), F32),
                        pltpu.VMEM((LANES + ONES_ROWS, 2 * t), F32),
                        pltpu.VMEM((g * t, 2 * t), F32)],
        compiler_params=_cparams(("arbitrary", "arbitrary")),
        name="moba",
    )(h0, kaug, vt, kmean)


def _dsa_kernel(it_ref, jt_ref, qi_ref, wi_ref, ki_ref, q_ref, k_ref, vt_ref, o_ref,
                key_ref, thr_ref, need_ref, seen_ref, m_ref, acc_ref, qt_ref,
                *, t, g, n_heads, topk):
    step = pl.program_id(0)
    i = it_ref[step]
    j = jt_ref[step]
    lo = _head_a(_iota((t, LANES), 1))
    key_le_query = _iota((t, t), 0) <= _iota((t, t), 1)
    n_pairs = n_heads // 2

    def split_t(pair):
        pf = pair.astype(F32)
        return (jnp.where(lo, pf, 0.0).T.astype(BF16), jnp.where(lo, 0.0, pf).T.astype(BF16))

    @pl.when(j == 0)
    def _select():
        qi = qi_ref[...]
        wi = wi_ref[0, 0].astype(F32)
        qit = [jnp.concatenate(split_t(qi[:, p * LANES:(p + 1) * LANES]), axis=1)
               for p in range(IDX_HEADS // 2)]
        q = q_ref[...]
        for p in range(n_pairs):
            qt_ref[p] = jnp.concatenate(split_t(q[:, p * LANES:(p + 1) * LANES]), axis=1)

        def products(kt):
            start = pl.multiple_of(kt * t, t)
            kblk = ki_ref[pl.ds(start, t), :]
            return [_dot(kblk, qit[p]) for p in range(IDX_HEADS // 2)]

        def keys_of(kt, logits):
            sc = jnp.zeros((t, t), F32)
            for h in range(IDX_HEADS):
                lg = logits[h // 2][:, (h % 2) * t:(h % 2 + 1) * t]
                sc = sc + wi[h:h + 1, :] * jnp.maximum(lg, 0.0)
            causal = jnp.logical_or(kt < i, jnp.logical_and(kt == i, key_le_query))
            key_ref[kt] = jnp.where(causal, sc, -jnp.inf)

        n_trips = (i + COUNT_UNROLL) // COUNT_UNROLL

        def score_trip(trip, carry):
            first = COUNT_UNROLL * trip
            logits = products(first)
            for u in range(COUNT_UNROLL):
                cur = logits
                if u + 1 < COUNT_UNROLL:
                    logits = products(first + u + 1)
                keys_of(first + u, cur)
            return carry

        lax.fori_loop(0, n_trips, score_trip, 0)
        for b in range(1, g + 1):
            key_ref[i + b] = jnp.full((t, t), -jnp.inf, F32)

        def count(pred):
            def cbody(trip, c):
                for kt in [COUNT_UNROLL * trip + u for u in range(COUNT_UNROLL)]:
                    hit = jnp.where(pred(key_ref[kt]), 1.0, 0.0)
                    c = c + hit.reshape(t // 8, 8, t).sum(axis=0)
                return c
            c = lax.fori_loop(0, n_trips, cbody, jnp.zeros((8, t), F32))
            return jnp.sum(c, axis=0, keepdims=True)

        def as_score(code):
            signed = code ^ jnp.int32(INT_MIN)
            bits = jnp.where(signed >= 0, signed, signed ^ jnp.int32(0x7FFFFFFF))
            return pltpu.bitcast(bits, F32)

        code = jnp.zeros((1, t), I32)
        for b in range(31, -1, -1):
            bit = jnp.int32(INT_MIN) if b == 31 else jnp.int32(1 << b)
            cand = as_score(code | bit)
            cnt = count(lambda ss, c=cand: ss >= c)
            code = jnp.where(cnt >= float(topk), code | bit, code)
        thr = jnp.where(code == 0, -jnp.inf, as_score(code))
        n_gt = count(lambda ss: ss > thr)
        thr_ref[...] = thr
        need_ref[...] = float(topk) - n_gt
        seen_ref[...] = jnp.zeros_like(seen_ref)
        m_ref[...] = jnp.full(m_ref.shape, NEG, F32)
        acc_ref[...] = jnp.zeros_like(acc_ref)

    thr = thr_ref[...]
    need = need_ref[...]
    earlier = (_iota((t, t), 1) < _iota((t, t), 0)).astype(BF16)
    seen = seen_ref[...]
    biases = []
    for b in range(g):
        blk = j * g + b
        key = key_ref[blk]
        eq = key == thr
        rank = seen + _dot(earlier, jnp.where(eq, 1.0, 0.0).astype(BF16))
        sel = (key > thr) | (eq & (rank < need))
        sel = sel & jnp.logical_or(blk < i, jnp.logical_and(blk == i, key_le_query))
        seen = seen + jnp.sum(jnp.where(eq, 1.0, 0.0), axis=0, keepdims=True)
        biases.append(jnp.where(sel, 0.0, NEG))
    seen_ref[...] = seen
    bias = jnp.concatenate(biases, axis=0)
    bias = jnp.concatenate([bias, bias], axis=1)

    kb = k_ref[...]
    ones = jnp.ones((ONES_ROWS, g * t), BF16)

    def scores(p):
        return _dot(kb[:, p * LANES:(p + 1) * LANES], qt_ref[p]) + bias

    def softmax(p, st):
        m_old = m_ref[p]
        m_new = jnp.maximum(m_old, jnp.max(st, axis=0, keepdims=True))
        m_ref[p] = m_new
        return jnp.exp2(m_old - m_new), jnp.exp2(st - m_new).astype(BF16)

    def accumulate(p, alpha, pt):
        vt = jnp.concatenate([vt_ref[p, b] for b in range(g)], axis=1)
        vt = jnp.concatenate([vt, ones], axis=0)
        acc_ref[p] = alpha * acc_ref[p] + _dot(vt, pt)

    st_next = scores(0)
    for p in range(n_pairs):
        st = st_next
        if p + 1 < n_pairs:
            st_next = scores(p + 1)
        accumulate(p, *softmax(p, st))

    @pl.when(j == lax.shift_right_logical(i, int(np.log2(g))))
    def _finish():
        first = _iota((LANES, t), 0) < HEAD_DIM
        for p in range(n_pairs):
            acc = acc_ref[p]
            a0, a1 = acc[:, :t], acc[:, t:]
            out_t = jnp.where(first, a0[:LANES] / a0[LANES:LANES + 1], a1[:LANES] / a1[LANES:LANES + 1])
            o_ref[:, p * LANES:(p + 1) * LANES] = out_t.T.astype(o_ref.dtype)


def _dsa_attention(h1, vt, *, n_heads, t=256, g=4):
    s = h1.shape[0]
    width = n_heads * HEAD_DIM
    nt = s // t
    g = min(g, nt)
    assert nt % g == 0
    topk = min(DSA_TOPK, s // 4)
    iw = IDX_HEADS * HEAD_DIM
    qi_blk = 2 * width // iw
    ki_blk = (2 * width + iw) // LANES
    wi_grp = width // LANES
    tri = [(i, j) for i in range(nt) for j in range(i // g + 1)]
    it = jnp.asarray([a for a, _ in tri], I32)
    jt = jnp.asarray([b for _, b in tri], I32)
    grid_spec = pltpu.PrefetchScalarGridSpec(
        num_scalar_prefetch=2,
        grid=(len(tri),),
        in_specs=[
            pl.BlockSpec((t, iw), lambda n, it, jt: (it[n], qi_blk)),
            pl.BlockSpec((1, 1, ONES_ROWS, t), lambda n, it, jt: (wi_grp, it[n], 0, 0)),
            pl.BlockSpec((s, LANES), lambda n, it, jt: (0, ki_blk)),
            pl.BlockSpec((t, width), lambda n, it, jt: (it[n], 0)),
            pl.BlockSpec((g * t, width), lambda n, it, jt: (jt[n], 1)),
            pl.BlockSpec((n_heads // 2, g, LANES, t), lambda n, it, jt: (0, jt[n], 0, 0)),
        ],
        out_specs=pl.BlockSpec((t, width), lambda n, it, jt: (it[n], 0)),
        scratch_shapes=[
            pltpu.VMEM((nt + g, t, t), F32),
            pltpu.VMEM((1, t), F32),
            pltpu.VMEM((1, t), F32),
            pltpu.VMEM((1, t), F32),
            pltpu.VMEM((n_heads // 2, 1, 2 * t), F32),
            pltpu.VMEM((n_heads // 2, LANES + ONES_ROWS, 2 * t), F32),
            pltpu.VMEM((n_heads // 2, LANES, 2 * t), BF16),
        ],
    )
    return pl.pallas_call(
        functools.partial(_dsa_kernel, t=t, g=g, n_heads=n_heads, topk=topk),
        grid_spec=grid_spec,
        out_shape=jax.ShapeDtypeStruct((s, width), BF16),
        compiler_params=_cparams(("arbitrary",)),
        name="dsa",
    )(it, jt, h1, vt, h1, h1, h1, vt)


def _layer_norm(y, g, b):
    mu = jnp.mean(y, axis=-1, keepdims=True)
    yc = y - mu
    var = jnp.mean(yc * yc, axis=-1, keepdims=True)
    return yc * lax.rsqrt(var + LN_EPS) * g + b


def _outproj_kernel(*refs, n_in, alpha):
    o_refs = refs[:n_in]
    w_refs = refs[n_in:2 * n_in]
    x_ref, g_ref, b_ref, out_ref = refs[2 * n_in:]
    y = alpha * x_ref[...]
    for o_ref, w_ref in zip(o_refs, w_refs):
        y = y + _dot(o_ref[...], w_ref[...])
    out_ref[...] = _layer_norm(y, g_ref[...], b_ref[...])


def _outproj_ln(os_, ws, x, g, b, alpha, *, tm=512):
    s, d = x.shape
    n_in = len(os_)
    in_specs = ([pl.BlockSpec((tm, o.shape[1]), lambda i: (i, 0)) for o in os_]
                + [pl.BlockSpec(w.shape, lambda i: (0, 0)) for w in ws]
                + [pl.BlockSpec((tm, d), lambda i: (i, 0)),
                   pl.BlockSpec((1, d), lambda i: (0, 0)),
                   pl.BlockSpec((1, d), lambda i: (0, 0))])
    return pl.pallas_call(
        functools.partial(_outproj_kernel, n_in=n_in, alpha=alpha),
        grid=(s // tm,),
        in_specs=in_specs,
        out_specs=pl.BlockSpec((tm, d), lambda i: (i, 0)),
        out_shape=jax.ShapeDtypeStruct((s, d), F32),
        compiler_params=_cparams(("arbitrary",)),
        name="out_proj_ln",
    )(*os_, *ws, x, g, b)


def _moe_kernel(x_ref, wr_ref, br_ref, w1g_ref, w1u_ref, b1g_ref, b1u_ref, w2_ref, b2_ref,
                g_ref, b_ref, out_ref,
                xb_ref, rank_ref, rank_t_ref, gate_t_ref, cnt_ref, acc_ref,
                *, tt, ch, alpha):
    e = pl.program_id(1)
    n_e = pl.num_programs(1)

    @pl.when(e == 0)
    def _route():
        x = x_ref[...]
        x_hi = x.astype(BF16)
        xb_ref[...] = x_hi
        x_lo = (x - x_hi.astype(F32)).astype(BF16)
        wr = wr_ref[...]
        w_hi = wr.astype(BF16)
        w_lo = (wr - w_hi.astype(F32)).astype(BF16)
        logits = (_dot_nt(w_hi, x_hi) + _dot_nt(w_hi, x_lo) + _dot_nt(w_lo, x_hi)) + br_ref[...]
        row = _iota(logits.shape, 0).astype(F32)
        work = jnp.where(row < N_EXPERTS, logits, -jnp.inf)
        chosen = row < 0
        top = None
        for r in range(TOP_K):
            m = jnp.max(work, axis=0, keepdims=True)
            idx = jnp.min(jnp.where(work == m, row, float(LANES)), axis=0, keepdims=True)
            pick = row == idx
            chosen = chosen | pick
            work = jnp.where(pick, -jnp.inf, work)
            if r == 0:
                top = m
        ex = jnp.where(chosen, jnp.exp(logits - top), 0.0)
        gate = ex / jnp.sum(ex, axis=0, keepdims=True)
        sel16 = jnp.where(chosen, 1.0, 0.0).astype(BF16)
        before = (_iota((tt, tt), 0) < _iota((tt, tt), 1)).astype(BF16)
        pos = _dot(sel16, before)
        rank = jnp.where(chosen, pos, -1.0)
        rank_ref[...] = rank
        rank_t_ref[...] = rank.T
        gate_t_ref[...] = gate.T
        cnt = jnp.sum(jnp.where(chosen, 1.0, 0.0), axis=1, keepdims=True)
        cnt_ref[...] = jnp.broadcast_to(cnt, cnt_ref.shape)
        acc_ref[...] = jnp.zeros_like(acc_ref)

    lane = _iota((tt, LANES), 1)
    on_e = lane == e
    rank_col = jnp.sum(jnp.where(on_e, rank_t_ref[...], 0.0), axis=1, keepdims=True)
    gate_col = jnp.sum(jnp.where(on_e, gate_t_ref[...], 0.0), axis=1, keepdims=True)
    rank_row = rank_ref[pl.ds(e, 1), :]
    n_rows = jnp.max(cnt_ref[pl.ds(e, 1), :]).astype(I32)
    n_chunks = (n_rows + (ch - 1)) // ch
    pad_rows = jnp.zeros((2 * LANES - ch, x_ref.shape[1]), BF16)
    slot = _iota((1, 2 * LANES), 1)
    slot = jnp.where(slot < ch, slot.astype(F32), -1e9)
    b1g = b1g_ref[0]
    b1u = b1u_ref[0]
    b2 = b2_ref[0]

    def chunk(c, carry):
        w1g = w1g_ref[0]
        w1u = w1u_ref[0]
        w2 = w2_ref[0]
        r0 = (c * ch).astype(F32)
        gather = (rank_row == (r0 + _iota((ch, tt), 0).astype(F32))).astype(BF16)
        xg = _dot(gather, xb_ref[...]).astype(BF16)
        hg = jnp.minimum(_dot(xg, w1g) + b1g, SWIGLU_LIMIT)
        hu = jnp.clip(_dot(xg, w1u) + b1u, -SWIGLU_LIMIT, SWIGLU_LIMIT)
        act = (hu + 1.0) * (hg * jax.nn.sigmoid(SWIGLU_ALPHA * hg))
        y = _dot(act.astype(BF16), w2) + b2
        scatter = jnp.where(rank_col == (r0 + slot), gate_col, 0.0).astype(BF16)
        acc_ref[...] += _dot(scatter, jnp.concatenate([y.astype(BF16), pad_rows], axis=0))
        return carry

    lax.fori_loop(0, n_chunks, chunk, 0)

    @pl.when(e == n_e - 1)
    def _finish():
        out_ref[...] = _layer_norm(alpha * x_ref[...] + acc_ref[...], g_ref[...], b_ref[...])


def _moe_ln(x, wr_t, br, w1g, w1u, b1g, b1u, w2, b2, g, b, alpha, *, tt=1024, ch=160):
    t, d = x.shape
    tt = min(tt, t)
    assert ch % 16 == 0 and ch <= 2 * LANES
    f = w2.shape[1]
    return pl.pallas_call(
        functools.partial(_moe_kernel, tt=tt, ch=ch, alpha=alpha),
        grid=(t // tt, N_EXPERTS),
        in_specs=[
            pl.BlockSpec((tt, d), lambda i, e: (i, 0)),
            pl.BlockSpec((LANES, d), lambda i, e: (0, 0)),
            pl.BlockSpec((LANES, 1), lambda i, e: (0, 0)),
            pl.BlockSpec((1, d, f), lambda i, e: (e, 0, 0)),
            pl.BlockSpec((1, d, f), lambda i, e: (e, 0, 0)),
            pl.BlockSpec((1, 1, f), lambda i, e: (e, 0, 0)),
            pl.BlockSpec((1, 1, f), lambda i, e: (e, 0, 0)),
            pl.BlockSpec((1, f, d), lambda i, e: (e, 0, 0)),
            pl.BlockSpec((1, 1, d), lambda i, e: (e, 0, 0)),
            pl.BlockSpec((1, d), lambda i, e: (0, 0)),
            pl.BlockSpec((1, d), lambda i, e: (0, 0)),
        ],
        out_specs=pl.BlockSpec((tt, d), lambda i, e: (i, 0)),
        out_shape=jax.ShapeDtypeStruct((t, d), F32),
        scratch_shapes=[
            pltpu.VMEM((tt, d), BF16),
            pltpu.VMEM((LANES, tt), F32),
            pltpu.VMEM((tt, LANES), F32),
            pltpu.VMEM((tt, LANES), F32),
            pltpu.VMEM((LANES, LANES), F32),
            pltpu.VMEM((tt, d), F32),
        ],
        compiler_params=_cparams(("arbitrary", "arbitrary")),
        name="moe_ln",
    )(x, wr_t, br, w1g, w1u, b1g, b1u, w2, b2, g, b)


def _ple_kernel(x_ref, p_ref, wg_ref, wp_ref, o_ref):
    x = x_ref[...]
    gate = jax.nn.sigmoid(_dot(x.astype(BF16), wg_ref[...]))
    emb = _dot(p_ref[...].astype(BF16), wp_ref[...])
    o_ref[...] = x + gate * emb


def _ple(x, p, wg, wp, *, tm=512):
    s, d = x.shape
    dp = p.shape[1]
    return pl.pallas_call(
        _ple_kernel,
        grid=(s // tm,),
        in_specs=[
            pl.BlockSpec((tm, d), lambda i: (i, 0)),
            pl.BlockSpec((tm, dp), lambda i: (i, 0)),
            pl.BlockSpec((d, d), lambda i: (0, 0)),
            pl.BlockSpec((dp, d), lambda i: (0, 0)),
        ],
        out_specs=pl.BlockSpec((tm, d), lambda i: (i, 0)),
        out_shape=jax.ShapeDtypeStruct((s, d), F32),
        compiler_params=_cparams(("arbitrary",)),
        name="ple",
    )(x, p, wg, wp)


def _mixer_ab(x, w_in, cos_t, sin_t, *, sb_heads, moba_heads):
    s = x.shape[0]
    wa = sb_heads * HEAD_DIM
    wb = moba_heads * HEAD_DIM
    qa, ka, va, qb, kb, vb = jnp.split(w_in, np.cumsum([wa, wa, wa, wb, wb]).tolist(), axis=1)
    rot = _rotary_layout(wb)
    w = jnp.concatenate([qb[:, rot], kb[:, rot], qa, ka, va], axis=1).astype(BF16)
    att = HEAD_DIM ** -0.5
    scale = jnp.concatenate([jnp.full((wb,), att * LOG2E), jnp.ones((wb,)), jnp.full((wa,), att),
                             jnp.ones((2 * wa,))]).astype(F32)[None]
    tn = cos_t.shape[1]
    h0 = _proj(x, w, scale, cos_t, sin_t, ((0, 2 * wb // tn),), tn=tn)
    n_pairs = moba_heads // 2
    vt = _proj_t(x, vb.astype(BF16), jnp.ones((1, wb), F32), groups=n_pairs)
    c = lambda off: off // LANES
    o_a = _sb_attention(h0, q_col=c(2 * wb), k_col=c(2 * wb + wa), v_col=c(2 * wb + 2 * wa),
                        n_pairs=sb_heads // 2)
    nb = s // MOBA_BLOCK
    kaug, kmean = _moba_prep(h0, k_col=c(wb), n_pairs=n_pairs)
    kmean = jnp.pad(kmean.reshape(n_pairs, nb, LANES), ((0, 0), (0, LANES - nb), (0, 0)))
    kmean = kmean.reshape(n_pairs * LANES, LANES)
    o_b = _moba_attention(h0, kaug, vt, kmean, q_col=0, n_pairs=n_pairs)
    return o_a, o_b


def _mixer_c(x, w_in, cos_t, sin_t, *, n_heads):
    width = n_heads * HEAD_DIM
    iw = IDX_HEADS * HEAD_DIM
    d = w_in.shape[0]
    q, k, v, qi, ki, wi = jnp.split(w_in, np.cumsum([width, width, width, iw, HEAD_DIM]).tolist(), axis=1)
    tn = cos_t.shape[1]
    half = HEAD_DIM // 2
    ki_pad = jnp.zeros((d, tn - 2 * HEAD_DIM), w_in.dtype)
    w = jnp.concatenate([q[:, _rotary_layout(width)], k[:, _rotary_layout(width)], qi[:, _rotary_layout(iw)],
                         ki[:, :half], ki[:, :half], ki[:, half:], ki[:, half:], ki_pad], axis=1).astype(BF16)
    att = HEAD_DIM ** -0.5
    scale = jnp.concatenate([jnp.full((width,), att * LOG2E), jnp.ones((width,)), jnp.full((iw,), att),
                             jnp.ones((tn,))]).astype(F32)[None]
    h1 = _proj(x, w, scale, cos_t, sin_t, ((0, (2 * width + iw) // tn + 1),), tn=tn)
    wi_pad = jnp.zeros((d, LANES - IDX_HEADS), w_in.dtype)
    wv = jnp.concatenate([v, wi, wi_pad], axis=1).astype(BF16)
    vscale = jnp.concatenate([jnp.ones((width,)), jnp.full((LANES,), IDX_HEADS ** -0.5)]).astype(F32)[None]
    vt = _proj_t(x, wv, vscale, groups=3)
    return _dsa_attention(h1, vt, n_heads=n_heads)


def _split_kernel(w_ref, perm_ref, g_ref, u_ref):
    half = g_ref.shape[2]
    both = _dot(w_ref[0, 0].astype(BF16), perm_ref[...])
    g_ref[0] = both[:, :half].astype(BF16)
    u_ref[0] = both[:, half:].astype(BF16)


def _split_gate_up(w1_all, layer, *, tc=512):
    _, n_e, d, f2 = w1_all.shape
    half = tc // 2
    src = jnp.concatenate([2 * jnp.arange(half), 2 * jnp.arange(half) + 1])
    perm = (jnp.arange(tc)[:, None] == src[None, :]).astype(BF16)
    out = jax.ShapeDtypeStruct((n_e, d, f2 // 2), BF16)
    return pl.pallas_call(
        _split_kernel,
        grid=(n_e, f2 // tc),
        in_specs=[pl.BlockSpec((1, 1, d, tc), lambda e, c: (layer, e, 0, c)),
                  pl.BlockSpec((tc, tc), lambda e, c: (0, 0))],
        out_specs=[pl.BlockSpec((1, d, half), lambda e, c: (e, 0, c)),
                   pl.BlockSpec((1, d, half), lambda e, c: (e, 0, c))],
        out_shape=[out, out],
        compiler_params=_cparams(("arbitrary", "arbitrary")),
        name="split_gate_up",
    )(w1_all, perm)


def _cast_kernel(w_ref, o_ref):
    o_ref[0] = w_ref[0, 0].astype(o_ref.dtype)


def _cast_layer(w_all, layer):
    _, n_e, f, d = w_all.shape
    return pl.pallas_call(
        _cast_kernel,
        grid=(n_e,),
        in_specs=[pl.BlockSpec((1, 1, f, d), lambda e: (layer, e, 0, 0))],
        out_specs=pl.BlockSpec((1, f, d), lambda e: (e, 0, 0)),
        out_shape=jax.ShapeDtypeStruct((n_e, f, d), BF16),
        compiler_params=_cparams(("arbitrary",)),
        name="cast_w2",
    )(w_all)


def _moe_weights(w_r, b_r, w1_all, b1, w2_all, b2, layer):
    wr_t = jnp.pad(w_r.T, ((0, LANES - N_EXPERTS), (0, 0)))
    br = jnp.pad(b_r, (0, LANES - N_EXPERTS))[:, None]
    w1g, w1u = _split_gate_up(w1_all, layer)
    return (wr_t, br, w1g, w1u, b1[:, None, 0::2], b1[:, None, 1::2], _cast_layer(w2_all, layer),
            b2[:, None, :])


def kernel(x, p, ab_w_in, ab_w_out, c_w_in, c_w_out, ln_g, ln_b, router_w, router_b,
           moe_w1, moe_b1, moe_w2, moe_b2, ple_w_proj, ple_w_gate):
    b, s, d = x.shape
    assert b == 1
    depth = p.shape[0]
    alpha = float((2 * depth) ** 0.25)
    cos_t, sin_t = _rope_tables(s, 4 * LANES)
    sb_heads = moba_heads = ab_w_out.shape[1] // (2 * HEAD_DIM)
    dsa_heads = c_w_out.shape[1] // HEAD_DIM
    xs = x[0]
    for i in range(depth):
        j = i // 2
        g = ln_g[i][:, None, :]
        bb = ln_b[i][:, None, :]
        if i % 2 == 0:
            o_a, o_b = _mixer_ab(xs, ab_w_in[j], cos_t, sin_t, sb_heads=sb_heads, moba_heads=moba_heads)
            w_out = ab_w_out[j].astype(BF16)
            wa = sb_heads * HEAD_DIM
            xs = _outproj_ln([o_a, o_b], [w_out[:wa], w_out[wa:]], xs, g[0], bb[0], alpha)
        else:
            o_c = _mixer_c(xs, c_w_in[j], cos_t, sin_t, n_heads=dsa_heads)
            xs = _outproj_ln([o_c], [c_w_out[j].astype(BF16)], xs, g[0], bb[0], alpha)
        mw = _moe_weights(router_w[i], router_b[i], moe_w1, moe_b1[i], moe_w2, moe_b2[i], i)
        xs = _moe_ln(xs, *mw, g[1], bb[1], alpha)
        xs = _ple(xs, p[i, 0], ple_w_gate[i].astype(BF16), ple_w_proj[i].astype(BF16))
    return xs[None]
```

```python
import functools

import jax
import jax.numpy as jnp
import numpy as np
from jax import lax
from jax.experimental import pallas as pl
from jax.experimental.pallas import tpu as pltpu

F32 = jnp.float32
BF16 = jnp.bfloat16
I32 = jnp.int32

HEAD_DIM = 64
LANES = 128
MOBA_BLOCK = 256
MOBA_TOPK = 3
DSA_TOPK = 256
IDX_HEADS = 8
N_EXPERTS = 32
TOP_K = 4
SWIGLU_LIMIT = 7.0
SWIGLU_ALPHA = 1.702
ROPE_THETA = 10000.0
LN_EPS = 1e-5
NEG = -1e30
EXP_FLUSH = -88.0
INT_MIN = -(2 ** 31)
COUNT_UNROLL = 4
LOG2E = 1.4426950408889634
ONES_ROWS = 16
VMEM_LIMIT = 56 * 1024 * 1024


def _cparams(sem, flags=None):
    return pltpu.CompilerParams(dimension_semantics=sem, vmem_limit_bytes=VMEM_LIMIT, flags=flags)


def _dot(a, b):
    return jnp.dot(a, b, preferred_element_type=F32)


def _dot_nt(a, b):
    return lax.dot_general(a, b, (((1,), (1,)), ((), ())), preferred_element_type=F32)


def _dot_tn(a, b):
    return lax.dot_general(a, b, (((0,), (0,)), ((), ())), preferred_element_type=F32)


def _iota(shape, dim):
    return lax.broadcasted_iota(I32, shape, dim)


def _proj_kernel(x_ref, w_ref, scale_ref, cos_ref, sin_ref, o_ref, xb_ref, *, rope_ranges, tn):
    j = pl.program_id(1)

    @pl.when(j == 0)
    def _():
        xb_ref[...] = x_ref[...].astype(BF16)

    h = _dot(xb_ref[...], w_ref[...]) * scale_ref[...]
    is_rope = functools.reduce(jnp.logical_or, [(j >= a) & (j < b) for a, b in rope_ranges])

    @pl.when(is_rope)
    def _():
        for c in range(tn // LANES):
            sl = slice(c * LANES, (c + 1) * LANES)
            hs = h[:, sl]
            partner = pltpu.roll(hs, LANES // 2, 1)
            o_ref[:, sl] = (hs * cos_ref[:, sl] + partner * sin_ref[:, sl]).astype(o_ref.dtype)

    @pl.when(jnp.logical_not(is_rope))
    def _():
        o_ref[...] = h.astype(o_ref.dtype)


def _proj(x, w, scale, cos_t, sin_t, rope_ranges, *, tm=512, tn=256):
    s, d = x.shape
    n = w.shape[1]
    return pl.pallas_call(
        functools.partial(_proj_kernel, rope_ranges=rope_ranges, tn=tn),
        grid=(s // tm, n // tn),
        in_specs=[
            pl.BlockSpec((tm, d), lambda i, j: (i, 0)),
            pl.BlockSpec((d, tn), lambda i, j: (0, j)),
            pl.BlockSpec((1, tn), lambda i, j: (0, j)),
            pl.BlockSpec((tm, tn), lambda i, j: (i, 0)),
            pl.BlockSpec((tm, tn), lambda i, j: (i, 0)),
        ],
        out_specs=pl.BlockSpec((tm, tn), lambda i, j: (i, j)),
        out_shape=jax.ShapeDtypeStruct((s, n), BF16),
        scratch_shapes=[pltpu.VMEM((tm, d), BF16)],
        compiler_params=_cparams(("arbitrary", "arbitrary")),
        name="in_proj",
    )(x, w, scale, cos_t, sin_t)


def _proj_t_kernel(x_ref, w_ref, scale_ref, o_ref, *, groups, tiles, tm):
    h = _dot(x_ref[...].astype(BF16), w_ref[...]) * scale_ref[...]
    for g in range(groups):
        for r in range(tiles):
            o_ref[g, r] = h[r * tm:(r + 1) * tm, g * LANES:(g + 1) * LANES].T.astype(o_ref.dtype)


def _proj_t(x, w, scale, *, groups, tm=256, tiles=4):
    s, d = x.shape
    n = w.shape[1]
    tc = groups * LANES
    tiles = min(tiles, s // tm)
    return pl.pallas_call(
        functools.partial(_proj_t_kernel, groups=groups, tiles=tiles, tm=tm),
        grid=(s // (tiles * tm), n // tc),
        in_specs=[
            pl.BlockSpec((tiles * tm, d), lambda i, j: (i, 0)),
            pl.BlockSpec((d, tc), lambda i, j: (0, j)),
            pl.BlockSpec((1, tc), lambda i, j: (0, j)),
        ],
        out_specs=pl.BlockSpec((groups, tiles, LANES, tm), lambda i, j: (j, i, 0, 0)),
        out_shape=jax.ShapeDtypeStruct((n // LANES, s // tm, LANES, tm), BF16),
        compiler_params=_cparams(("arbitrary", "arbitrary")),
        name="proj_t",
    )(x, w, scale)


def _rope_tables(s, width):
    half = HEAD_DIM // 2
    inv = ROPE_THETA ** (-jnp.arange(half, dtype=F32) / half)
    ang = jnp.arange(s, dtype=F32)[:, None] * inv[None, :]
    cos = jnp.cos(ang)
    sin = jnp.sin(ang)
    reps = width // LANES
    cos_t = jnp.tile(jnp.concatenate([cos, cos, cos, cos], axis=1), (1, reps))
    sin_t = jnp.tile(jnp.concatenate([-sin, -sin, sin, sin], axis=1), (1, reps))
    return cos_t, sin_t


def _rotary_layout(n_cols):
    half = HEAD_DIM // 2
    lane = np.arange(n_cols)
    base = lane // LANES * LANES
    within = lane % LANES
    head = within // half % 2
    part = within // HEAD_DIM
    return base + head * HEAD_DIM + part * half + within % half


def _head_a(lane):
    return (lane % HEAD_DIM) < (HEAD_DIM // 2)


def _sb_kernel(q_ref, k_ref, v_ref, o_ref, acc_ref, run_ref, *, t, pp):
    i = pl.program_id(1)
    lane = _iota((t, LANES), 1)
    lo = lane < HEAD_DIM
    qh = []
    for p in range(pp):
        q = q_ref[:, p * LANES:(p + 1) * LANES]
        zero = jnp.zeros_like(q)
        qh.extend([jnp.where(lo, q, zero), jnp.where(lo, zero, q)])
    row = _iota((t, t), 0)
    col = _iota((t, t), 1)
    incl = (row >= col).astype(BF16)
    ones = jnp.ones((t, LANES), BF16)
    strict = col < row

    acc_ref[...] = jnp.zeros_like(acc_ref)
    run_ref[...] = jnp.zeros_like(run_ref)

    def tile(kt, diagonal):
        start = pl.multiple_of(kt * t, t)
        heads = range(2 * pp)
        sls = [slice((h // 2) * LANES, (h // 2 + 1) * LANES) for h in heads]
        zs = [_dot_nt(qh[h], k_ref[pl.ds(start, t), sls[h]]) for h in heads]
        ls16s = []
        for h in heads:
            log_stay = -(jnp.maximum(zs[h], 0.0) + jnp.log1p(jnp.exp(-jnp.abs(zs[h]))))
            if diagonal:
                log_stay = jnp.where(strict, log_stay, 0.0)
            ls16s.append(log_stay.astype(BF16))
        cums = [_dot(ls16s[h], incl) for h in heads]
        tots = [_dot(ls16s[h], ones) for h in heads]
        a16s = []
        for h in heads:
            a = jnp.exp(zs[h] + cums[h] + run_ref[h])
            if diagonal:
                a = jnp.where(strict, a, 0.0)
            a16s.append(a.astype(BF16))
        pvs = [_dot(a16s[h], v_ref[pl.ds(start, t), sls[h]]) for h in heads]
        for h in heads:
            acc_ref[h] += pvs[h]
            run_ref[h] += tots[h]

    tile(i, True)

    def cond(c):
        kt, live = c
        return jnp.logical_and(kt >= 0, live)

    def body(c):
        kt, _ = c
        tile(kt, False)
        live = jnp.max(run_ref[...]) > EXP_FLUSH
        return kt - 1, live

    lax.while_loop(cond, body, (i - 1, jnp.bool_(True)))
    for p in range(pp):
        o_ref[:, p * LANES:(p + 1) * LANES] = jnp.where(lo, acc_ref[2 * p], acc_ref[2 * p + 1]).astype(o_ref.dtype)


def _sb_attention(h0, *, q_col, k_col, v_col, n_pairs, t=128, pp=4):
    s = h0.shape[0]
    assert n_pairs % pp == 0 and q_col % pp == 0 and k_col % pp == 0 and v_col % pp == 0
    w = pp * LANES
    resident = pl.Buffered(1)
    return pl.pallas_call(
        functools.partial(_sb_kernel, t=t, pp=pp),
        grid=(n_pairs // pp, s // t),
        in_specs=[
            pl.BlockSpec((t, w), lambda p, i: (i, q_col // pp + p)),
            pl.BlockSpec((s, w), lambda p, i: (0, k_col // pp + p), pipeline_mode=resident),
            pl.BlockSpec((s, w), lambda p, i: (0, v_col // pp + p), pipeline_mode=resident),
        ],
        out_specs=pl.BlockSpec((t, w), lambda p, i: (i, p)),
        out_shape=jax.ShapeDtypeStruct((s, n_pairs * LANES), BF16),
        scratch_shapes=[pltpu.VMEM((2 * pp, t, LANES), F32), pltpu.VMEM((2 * pp, t, LANES), F32)],
        compiler_params=_cparams(("arbitrary", "arbitrary")),
        name="stick_breaking",
    )(h0, h0, h0)


def _moba_prep_kernel(k_ref, kaug_ref, kmean_ref, *, per_step):
    c = pl.program_id(1)
    for u in range(per_step):
        rows = slice(u * MOBA_BLOCK, (u + 1) * MOBA_BLOCK)
        k = k_ref[rows, :]
        kmean_ref[0, u:u + 1, :] = jnp.mean(k.astype(F32), axis=0, keepdims=True)
        onehot = (_iota(k.shape, 1) == c * per_step + u).astype(BF16)
        kaug_ref[rows, :] = jnp.concatenate([k, onehot], axis=1)


def _moba_prep(h0, *, k_col, n_pairs, per_step=8):
    s = h0.shape[0]
    nb = s // MOBA_BLOCK
    per_step = min(per_step, nb)
    assert nb % per_step == 0
    rows = per_step * MOBA_BLOCK
    return pl.pallas_call(
        functools.partial(_moba_prep_kernel, per_step=per_step),
        grid=(n_pairs, nb // per_step),
        in_specs=[pl.BlockSpec((rows, LANES), lambda p, c: (c, k_col + p))],
        out_specs=[
            pl.BlockSpec((rows, 2 * LANES), lambda p, c: (c, p)),
            pl.BlockSpec((1, per_step, LANES), lambda p, c: (p * (nb // per_step) + c, 0, 0)),
        ],
        out_shape=[
            jax.ShapeDtypeStruct((s, n_pairs * 2 * LANES), BF16),
            jax.ShapeDtypeStruct((n_pairs * nb // per_step, per_step, LANES), F32),
        ],
        compiler_params=_cparams(("arbitrary", "arbitrary")),
        name="moba_prep",
    )(h0)


def _moba_kernel(q_ref, kaug_ref, vt_ref, kmean_ref, o_ref, m_ref, acc_ref, st_ref, *, nb, g):
    t = MOBA_BLOCK
    i = pl.program_id(1)
    qf = q_ref[...].astype(F32)
    lo = _head_a(_iota((t, LANES), 1))
    qt = jnp.concatenate([jnp.where(lo, qf, 0.0).T, jnp.where(lo, 0.0, qf).T], axis=1).astype(BF16)
    kmean = kmean_ref[...]
    km_hi = kmean.astype(BF16)
    km_lo = (kmean - km_hi.astype(F32)).astype(BF16)
    gate = _dot(km_hi, qt) + _dot(km_lo, qt)
    blk = _iota((LANES, 2 * t), 0)
    blk_f = blk.astype(F32)
    gate = jnp.where(blk < i, gate, -jnp.inf)
    chosen = blk == i
    for _ in range(MOBA_TOPK):
        m = jnp.max(gate, axis=0, keepdims=True)
        first = jnp.min(jnp.where(gate == m, blk_f, float(LANES)), axis=0, keepdims=True)
        pick = blk_f == first
        chosen = chosen | (pick & (m > -jnp.inf))
        gate = jnp.where(pick, -jnp.inf, gate)
    bias = jnp.where(chosen | (blk >= nb), 0.0, NEG).astype(BF16)
    q_both = jnp.concatenate([qt, bias], axis=0)

    tk = g * t
    last = lax.shift_right_logical(i, int(np.log2(g)))
    ones = jnp.ones((ONES_ROWS, tk), BF16)

    def scores(u):
        start = pl.multiple_of(u * tk, tk)
        return _dot(kaug_ref[pl.ds(start, tk), :], q_both)

    m_ref[...] = jnp.full(m_ref.shape, NEG, F32)
    acc_ref[...] = jnp.zeros_like(acc_ref)
    key_pos = _iota((tk, 2 * t), 0) + (last * g - i) * t
    query_pos = _iota((tk, 2 * t), 1) & (t - 1)
    st_ref[...] = jnp.where(key_pos <= query_pos, scores(last), NEG)

    def body(n, carry):
        cur = jnp.where(n == 0, last, n - 1)
        nxt = jnp.minimum(n, jnp.maximum(last - 1, 0))
        st_next = scores(nxt)
        vt = jnp.concatenate([vt_ref[0, cur * g + b] for b in range(g)], axis=1)
        vt = jnp.concatenate([vt, ones], axis=0)
        st = st_ref[...]
        m_old = m_ref[...]
        m_new = jnp.maximum(m_old, jnp.max(st, axis=0, keepdims=True))
        alpha = jnp.exp2(m_old - m_new)
        pt = jnp.exp2(st - m_new).astype(BF16)
        acc_ref[...] = alpha * acc_ref[...] + _dot(vt, pt)
        m_ref[...] = m_new
        st_ref[...] = st_next
        return carry

    lax.fori_loop(0, last + 1, body, 0)
    acc = acc_ref[...]
    outs = [acc[:LANES, h * t:(h + 1) * t] / acc[LANES:LANES + 1, h * t:(h + 1) * t] for h in range(2)]
    first = _iota((LANES, t), 0) < HEAD_DIM
    o_ref[...] = jnp.where(first, outs[0], outs[1]).T.astype(o_ref.dtype)


def _moba_attention(h0, kaug, vt, kmean, *, q_col, n_pairs, g=4):
    s = h0.shape[0]
    t = MOBA_BLOCK
    nb = s // t
    g = min(g, nb)
    assert nb <= LANES and nb % g == 0
    return pl.pallas_call(
        functools.partial(_moba_kernel, nb=nb, g=g),
        grid=(n_pairs, nb),
        in_specs=[
            pl.BlockSpec((t, LANES), lambda p, i: (i, q_col + p)),
            pl.BlockSpec((s, 2 * LANES), lambda p, i: (0, p)),
            pl.BlockSpec((1, nb, LANES, t), lambda p, i: (p, 0, 0, 0)),
            pl.BlockSpec((LANES, LANES), lambda p, i: (p, 0)),
        ],
        out_specs=pl.BlockSpec((t, LANES), lambda p, i: (i, p)),
        out_shape=jax.ShapeDtypeStruct((s, n_pairs * LANES), BF16),
        scratch_shapes=[pltpu.VMEM((1, 2 * t), F32),
                        pltpu.VMEM((LANES + ONES_ROWS, 2 * t), F32),
                        pltpu.VMEM((g * t, 2 * t), F32)],
        compiler_params=_cparams(("arbitrary", "arbitrary")),
        name="moba",
    )(h0, kaug, vt, kmean)


def _dsa_kernel(it_ref, jt_ref, qi_ref, wi_ref, ki_ref, q_ref, k_ref, vt_ref, o_ref,
                key_ref, thr_ref, need_ref, seen_ref, m_ref, acc_ref, qt_ref,
                *, t, g, n_heads, topk):
    step = pl.program_id(0)
    i = it_ref[step]
    j = jt_ref[step]
    lo = _head_a(_iota((t, LANES), 1))
    key_le_query = _iota((t, t), 0) <= _iota((t, t), 1)
    n_pairs = n_heads // 2

    def split_t(pair):
        pf = pair.astype(F32)
        return (jnp.where(lo, pf, 0.0).T.astype(BF16), jnp.where(lo, 0.0, pf).T.astype(BF16))

    @pl.when(j == 0)
    def _select():
        qi = qi_ref[...]
        wi = wi_ref[0, 0].astype(F32)
        qit = [jnp.concatenate(split_t(qi[:, p * LANES:(p + 1) * LANES]), axis=1)
               for p in range(IDX_HEADS // 2)]
        q = q_ref[...]
        for p in range(n_pairs):
            qt_ref[p] = jnp.concatenate(split_t(q[:, p * LANES:(p + 1) * LANES]), axis=1)

        def products(kt):
            start = pl.multiple_of(kt * t, t)
            kblk = ki_ref[pl.ds(start, t), :]
            return [_dot(kblk, qit[p]) for p in range(IDX_HEADS // 2)]

        def keys_of(kt, logits):
            sc = jnp.zeros((t, t), F32)
            for h in range(IDX_HEADS):
                lg = logits[h // 2][:, (h % 2) * t:(h % 2 + 1) * t]
                sc = sc + wi[h:h + 1, :] * jnp.maximum(lg, 0.0)
            causal = jnp.logical_or(kt < i, jnp.logical_and(kt == i, key_le_query))
            key_ref[kt] = jnp.where(causal, sc, -jnp.inf)

        n_trips = (i + COUNT_UNROLL) // COUNT_UNROLL

        def score_trip(trip, carry):
            first = COUNT_UNROLL * trip
            logits = products(first)
            for u in range(COUNT_UNROLL):
                cur = logits
                if u + 1 < COUNT_UNROLL:
                    logits = products(first + u + 1)
                keys_of(first + u, cur)
            return carry

        lax.fori_loop(0, n_trips, score_trip, 0)
        for b in range(1, g + 1):
            key_ref[i + b] = jnp.full((t, t), -jnp.inf, F32)

        def count(pred):
            def cbody(trip, c):
                for kt in [COUNT_UNROLL * trip + u for u in range(COUNT_UNROLL)]:
                    hit = jnp.where(pred(key_ref[kt]), 1.0, 0.0)
                    c = c + hit.reshape(t // 8, 8, t).sum(axis=0)
                return c
            c = lax.fori_loop(0, n_trips, cbody, jnp.zeros((8, t), F32))
            return jnp.sum(c, axis=0, keepdims=True)

        def as_score(code):
            signed = code ^ jnp.int32(INT_MIN)
            bits = jnp.where(signed >= 0, signed, signed ^ jnp.int32(0x7FFFFFFF))
            return pltpu.bitcast(bits, F32)

        code = jnp.zeros((1, t), I32)
        for b in range(31, -1, -1):
            bit = jnp.int32(INT_MIN) if b == 31 else jnp.int32(1 << b)
            cand = as_score(code | bit)
            cnt = count(lambda ss, c=cand: ss >= c)
            code = jnp.where(cnt >= float(topk), code | bit, code)
        thr = jnp.where(code == 0, -jnp.inf, as_score(code))
        n_gt = count(lambda ss: ss > thr)
        thr_ref[...] = thr
        need_ref[...] = float(topk) - n_gt
        seen_ref[...] = jnp.zeros_like(seen_ref)
        m_ref[...] = jnp.full(m_ref.shape, NEG, F32)
        acc_ref[...] = jnp.zeros_like(acc_ref)

    thr = thr_ref[...]
    need = need_ref[...]
    earlier = (_iota((t, t), 1) < _iota((t, t), 0)).astype(BF16)
    seen = seen_ref[...]
    biases = []
    for b in range(g):
        blk = j * g + b
        key = key_ref[blk]
        eq = key == thr
        rank = seen + _dot(earlier, jnp.where(eq, 1.0, 0.0).astype(BF16))
        sel = (key > thr) | (eq & (rank < need))
        sel = sel & jnp.logical_or(blk < i, jnp.logical_and(blk == i, key_le_query))
        seen = seen + jnp.sum(jnp.where(eq, 1.0, 0.0), axis=0, keepdims=True)
        biases.append(jnp.where(sel, 0.0, NEG))
    seen_ref[...] = seen
    bias = jnp.concatenate(biases, axis=0)
    bias = jnp.concatenate([bias, bias], axis=1)

    kb = k_ref[...]
    ones = jnp.ones((ONES_ROWS, g * t), BF16)

    def scores(p):
        return _dot(kb[:, p * LANES:(p + 1) * LANES], qt_ref[p]) + bias

    def softmax(p, st):
        m_old = m_ref[p]
        m_new = jnp.maximum(m_old, jnp.max(st, axis=0, keepdims=True))
        m_ref[p] = m_new
        return jnp.exp2(m_old - m_new), jnp.exp2(st - m_new).astype(BF16)

    def accumulate(p, alpha, pt):
        vt = jnp.concatenate([vt_ref[p, b] for b in range(g)], axis=1)
        vt = jnp.concatenate([vt, ones], axis=0)
        acc_ref[p] = alpha * acc_ref[p] + _dot(vt, pt)

    st_next = scores(0)
    for p in range(n_pairs):
        st = st_next
        if p + 1 < n_pairs:
            st_next = scores(p + 1)
        accumulate(p, *softmax(p, st))

    @pl.when(j == lax.shift_right_logical(i, int(np.log2(g))))
    def _finish():
        first = _iota((LANES, t), 0) < HEAD_DIM
        for p in range(n_pairs):
            acc = acc_ref[p]
            a0, a1 = acc[:, :t], acc[:, t:]
            out_t = jnp.where(first, a0[:LANES] / a0[LANES:LANES + 1], a1[:LANES] / a1[LANES:LANES + 1])
            o_ref[:, p * LANES:(p + 1) * LANES] = out_t.T.astype(o_ref.dtype)


def _dsa_attention(h1, vt, *, n_heads, t=256, g=4):
    s = h1.shape[0]
    width = n_heads * HEAD_DIM
    nt = s // t
    g = min(g, nt)
    assert nt % g == 0
    topk = min(DSA_TOPK, s // 4)
    iw = IDX_HEADS * HEAD_DIM
    qi_blk = 2 * width // iw
    ki_blk = (2 * width + iw) // LANES
    wi_grp = width // LANES
    tri = [(i, j) for i in range(nt) for j in range(i // g + 1)]
    it = jnp.asarray([a for a, _ in tri], I32)
    jt = jnp.asarray([b for _, b in tri], I32)
    grid_spec = pltpu.PrefetchScalarGridSpec(
        num_scalar_prefetch=2,
        grid=(len(tri),),
        in_specs=[
            pl.BlockSpec((t, iw), lambda n, it, jt: (it[n], qi_blk)),
            pl.BlockSpec((1, 1, ONES_ROWS, t), lambda n, it, jt: (wi_grp, it[n], 0, 0)),
            pl.BlockSpec((s, LANES), lambda n, it, jt: (0, ki_blk)),
            pl.BlockSpec((t, width), lambda n, it, jt: (it[n], 0)),
            pl.BlockSpec((g * t, width), lambda n, it, jt: (jt[n], 1)),
            pl.BlockSpec((n_heads // 2, g, LANES, t), lambda n, it, jt: (0, jt[n], 0, 0)),
        ],
        out_specs=pl.BlockSpec((t, width), lambda n, it, jt: (it[n], 0)),
        scratch_shapes=[
            pltpu.VMEM((nt + g, t, t), F32),
            pltpu.VMEM((1, t), F32),
            pltpu.VMEM((1, t), F32),
            pltpu.VMEM((1, t), F32),
            pltpu.VMEM((n_heads // 2, 1, 2 * t), F32),
            pltpu.VMEM((n_heads // 2, LANES + ONES_ROWS, 2 * t), F32),
            pltpu.VMEM((n_heads // 2, LANES, 2 * t), BF16),
        ],
    )
    return pl.pallas_call(
        functools.partial(_dsa_kernel, t=t, g=g, n_heads=n_heads, topk=topk),
        grid_spec=grid_spec,
        out_shape=jax.ShapeDtypeStruct((s, width), BF16),
        compiler_params=_cparams(("arbitrary",)),
        name="dsa",
    )(it, jt, h1, vt, h1, h1, h1, vt)


def _layer_norm(y, g, b):
    mu = jnp.mean(y, axis=-1, keepdims=True)
    yc = y - mu
    var = jnp.mean(yc * yc, axis=-1, keepdims=True)
    return yc * lax.rsqrt(var + LN_EPS) * g + b


def _outproj_kernel(*refs, n_in, alpha):
    o_refs = refs[:n_in]
    w_refs = refs[n_in:2 * n_in]
    x_ref, g_ref, b_ref, out_ref = refs[2 * n_in:]
    y = alpha * x_ref[...]
    for o_ref, w_ref in zip(o_refs, w_refs):
        y = y + _dot(o_ref[...], w_ref[...])
    out_ref[...] = _layer_norm(y, g_ref[...], b_ref[...])


def _outproj_ln(os_, ws, x, g, b, alpha, *, tm=512):
    s, d = x.shape
    n_in = len(os_)
    in_specs = ([pl.BlockSpec((tm, o.shape[1]), lambda i: (i, 0)) for o in os_]
                + [pl.BlockSpec(w.shape, lambda i: (0, 0)) for w in ws]
                + [pl.BlockSpec((tm, d), lambda i: (i, 0)),
                   pl.BlockSpec((1, d), lambda i: (0, 0)),
                   pl.BlockSpec((1, d), lambda i: (0, 0))])
    return pl.pallas_call(
        functools.partial(_outproj_kernel, n_in=n_in, alpha=alpha),
        grid=(s // tm,),
        in_specs=in_specs,
        out_specs=pl.BlockSpec((tm, d), lambda i: (i, 0)),
        out_shape=jax.ShapeDtypeStruct((s, d), F32),
        compiler_params=_cparams(("arbitrary",)),
        name="out_proj_ln",
    )(*os_, *ws, x, g, b)


def _moe_kernel(x_ref, wr_ref, br_ref, w1g_ref, w1u_ref, b1g_ref, b1u_ref, w2_ref, b2_ref,
                g_ref, b_ref, p_ref, wg_ref, wp_ref, out_ref,
                xb_ref, rank_ref, rank_t_ref, gate_t_ref, cnt_ref, acc_ref,
                *, tt, ch, alpha):
    e = pl.program_id(1)
    n_e = pl.num_programs(1)

    @pl.when(e == 0)
    def _route():
        x = x_ref[...]
        xb_ref[...] = x.astype(BF16)
        logits = lax.dot_general(wr_ref[...], x, (((1,), (1,)), ((), ())),
                                 precision=lax.Precision.HIGHEST,
                                 preferred_element_type=F32) + br_ref[...]
        row = _iota(logits.shape, 0).astype(F32)
        work = jnp.where(row < N_EXPERTS, logits, -jnp.inf)
        chosen = row < 0
        top = None
        for r in range(TOP_K):
            m = jnp.max(work, axis=0, keepdims=True)
            idx = jnp.min(jnp.where(work == m, row, float(LANES)), axis=0, keepdims=True)
            pick = row == idx
            chosen = chosen | pick
            work = jnp.where(pick, -jnp.inf, work)
            if r == 0:
                top = m
        ex = jnp.where(chosen, jnp.exp(logits - top), 0.0)
        gate = ex / jnp.sum(ex, axis=0, keepdims=True)
        sel16 = jnp.where(chosen, 1.0, 0.0).astype(BF16)
        before = (_iota((tt, tt), 0) < _iota((tt, tt), 1)).astype(BF16)
        pos = _dot(sel16, before)
        rank = jnp.where(chosen, pos, -1.0)
        rank_ref[...] = rank
        rank_t_ref[...] = rank.T
        gate_t_ref[...] = gate.T
        cnt = jnp.sum(jnp.where(chosen, 1.0, 0.0), axis=1, keepdims=True)
        cnt_ref[...] = jnp.broadcast_to(cnt, cnt_ref.shape)
        acc_ref[...] = jnp.zeros_like(acc_ref)

    lane = _iota((tt, LANES), 1)
    on_e = lane == e
    rank_col = jnp.sum(jnp.where(on_e, rank_t_ref[...], 0.0), axis=1, keepdims=True)
    gate_col = jnp.sum(jnp.where(on_e, gate_t_ref[...], 0.0), axis=1, keepdims=True)
    rank_row = rank_ref[pl.ds(e, 1), :]
    n_rows = jnp.max(cnt_ref[pl.ds(e, 1), :]).astype(I32)
    n_chunks = (n_rows + (ch - 1)) // ch
    pad_rows = jnp.zeros((2 * LANES - ch, x_ref.shape[1]), BF16)
    slot = _iota((1, 2 * LANES), 1)
    slot = jnp.where(slot < ch, slot.astype(F32), -1e9)
    b1g = b1g_ref[0]
    b1u = b1u_ref[0]
    b2 = b2_ref[0]

    def chunk(c, carry):
        w1g = w1g_ref[0]
        w1u = w1u_ref[0]
        w2 = w2_ref[0]
        r0 = (c * ch).astype(F32)
        gather = (rank_row == (r0 + _iota((ch, tt), 0).astype(F32))).astype(BF16)
        xg = _dot(gather, xb_ref[...]).astype(BF16)
        hg = jnp.minimum(_dot(xg, w1g) + b1g, SWIGLU_LIMIT)
        hu = jnp.clip(_dot(xg, w1u) + b1u, -SWIGLU_LIMIT, SWIGLU_LIMIT)
        act = (hu + 1.0) * (hg * jax.nn.sigmoid(SWIGLU_ALPHA * hg))
        y = _dot(act.astype(BF16), w2) + b2
        scatter = jnp.where(rank_col == (r0 + slot), gate_col, 0.0).astype(BF16)
        acc_ref[...] += _dot(scatter, jnp.concatenate([y.astype(BF16), pad_rows], axis=0))
        return carry

    lax.fori_loop(0, n_chunks, chunk, 0)

    @pl.when(e == n_e - 1)
    def _finish():
        y = _layer_norm(alpha * x_ref[...] + acc_ref[...], g_ref[...], b_ref[...])
        gate = jax.nn.sigmoid(_dot(y.astype(BF16), wg_ref[...]))
        out_ref[...] = y + gate * _dot(p_ref[...].astype(BF16), wp_ref[...])


def _moe_ln(x, wr_t, br, w1g, w1u, b1g, b1u, w2, b2, g, b, p, wg, wp, alpha, *, tt=1024, ch=160):
    t, d = x.shape
    tt = min(tt, t)
    assert ch % 16 == 0 and ch <= 2 * LANES
    f = w2.shape[1]
    return pl.pallas_call(
        functools.partial(_moe_kernel, tt=tt, ch=ch, alpha=alpha),
        grid=(t // tt, N_EXPERTS),
        in_specs=[
            pl.BlockSpec((tt, d), lambda i, e: (i, 0)),
            pl.BlockSpec((LANES, d), lambda i, e: (0, 0)),
            pl.BlockSpec((LANES, 1), lambda i, e: (0, 0)),
            pl.BlockSpec((1, d, f), lambda i, e: (e, 0, 0)),
            pl.BlockSpec((1, d, f), lambda i, e: (e, 0, 0)),
            pl.BlockSpec((1, 1, f), lambda i, e: (e, 0, 0)),
            pl.BlockSpec((1, 1, f), lambda i, e: (e, 0, 0)),
            pl.BlockSpec((1, f, d), lambda i, e: (e, 0, 0)),
            pl.BlockSpec((1, 1, d), lambda i, e: (e, 0, 0)),
            pl.BlockSpec((1, d), lambda i, e: (0, 0)),
            pl.BlockSpec((1, d), lambda i, e: (0, 0)),
            pl.BlockSpec((tt, p.shape[1]), lambda i, e: (i, 0)),
            pl.BlockSpec(wg.shape, lambda i, e: (0, 0)),
            pl.BlockSpec(wp.shape, lambda i, e: (0, 0)),
        ],
        out_specs=pl.BlockSpec((tt, d), lambda i, e: (i, 0)),
        out_shape=jax.ShapeDtypeStruct((t, d), F32),
        scratch_shapes=[
            pltpu.VMEM((tt, d), BF16),
            pltpu.VMEM((LANES, tt), F32),
            pltpu.VMEM((tt, LANES), F32),
            pltpu.VMEM((tt, LANES), F32),
            pltpu.VMEM((LANES, LANES), F32),
            pltpu.VMEM((tt, d), F32),
        ],
        compiler_params=_cparams(("arbitrary", "arbitrary")),
        name="moe_ln",
    )(x, wr_t, br, w1g, w1u, b1g, b1u, w2, b2, g, b, p, wg, wp)


def _ple_kernel(x_ref, p_ref, wg_ref, wp_ref, o_ref):
    x = x_ref[...]
    gate = jax.nn.sigmoid(_dot(x.astype(BF16), wg_ref[...]))
    emb = _dot(p_ref[...].astype(BF16), wp_ref[...])
    o_ref[...] = x + gate * emb


def _ple(x, p, wg, wp, *, tm=512):
    s, d = x.shape
    dp = p.shape[1]
    return pl.pallas_call(
        _ple_kernel,
        grid=(s // tm,),
        in_specs=[
            pl.BlockSpec((tm, d), lambda i: (i, 0)),
            pl.BlockSpec((tm, dp), lambda i: (i, 0)),
            pl.BlockSpec((d, d), lambda i: (0, 0)),
            pl.BlockSpec((dp, d), lambda i: (0, 0)),
        ],
        out_specs=pl.BlockSpec((tm, d), lambda i: (i, 0)),
        out_shape=jax.ShapeDtypeStruct((s, d), F32),
        compiler_params=_cparams(("arbitrary",)),
        name="ple",
    )(x, p, wg, wp)


def _mixer_ab(x, w_in, cos_t, sin_t, *, sb_heads, moba_heads):
    s = x.shape[0]
    wa = sb_heads * HEAD_DIM
    wb = moba_heads * HEAD_DIM
    qa, ka, va, qb, kb, vb = jnp.split(w_in, np.cumsum([wa, wa, wa, wb, wb]).tolist(), axis=1)
    rot = _rotary_layout(wb)
    w = jnp.concatenate([qb[:, rot], kb[:, rot], qa, ka, va], axis=1).astype(BF16)
    att = HEAD_DIM ** -0.5
    scale = jnp.concatenate([jnp.full((wb,), att * LOG2E), jnp.ones((wb,)), jnp.full((wa,), att),
                             jnp.ones((2 * wa,))]).astype(F32)[None]
    tn = cos_t.shape[1]
    h0 = _proj(x, w, scale, cos_t, sin_t, ((0, 2 * wb // tn),), tn=tn)
    n_pairs = moba_heads // 2
    vt = _proj_t(x, vb.astype(BF16), jnp.ones((1, wb), F32), groups=n_pairs)
    c = lambda off: off // LANES
    o_a = _sb_attention(h0, q_col=c(2 * wb), k_col=c(2 * wb + wa), v_col=c(2 * wb + 2 * wa),
                        n_pairs=sb_heads // 2)
    nb = s // MOBA_BLOCK
    kaug, kmean = _moba_prep(h0, k_col=c(wb), n_pairs=n_pairs)
    kmean = jnp.pad(kmean.reshape(n_pairs, nb, LANES), ((0, 0), (0, LANES - nb), (0, 0)))
    kmean = kmean.reshape(n_pairs * LANES, LANES)
    o_b = _moba_attention(h0, kaug, vt, kmean, q_col=0, n_pairs=n_pairs)
    return o_a, o_b


def _mixer_c(x, w_in, cos_t, sin_t, *, n_heads):
    width = n_heads * HEAD_DIM
    iw = IDX_HEADS * HEAD_DIM
    d = w_in.shape[0]
    q, k, v, qi, ki, wi = jnp.split(w_in, np.cumsum([width, width, width, iw, HEAD_DIM]).tolist(), axis=1)
    tn = cos_t.shape[1]
    half = HEAD_DIM // 2
    ki_pad = jnp.zeros((d, tn - 2 * HEAD_DIM), w_in.dtype)
    w = jnp.concatenate([q[:, _rotary_layout(width)], k[:, _rotary_layout(width)], qi[:, _rotary_layout(iw)],
                         ki[:, :half], ki[:, :half], ki[:, half:], ki[:, half:], ki_pad], axis=1).astype(BF16)
    att = HEAD_DIM ** -0.5
    scale = jnp.concatenate([jnp.full((width,), att * LOG2E), jnp.ones((width,)), jnp.full((iw,), att),
                             jnp.ones((tn,))]).astype(F32)[None]
    h1 = _proj(x, w, scale, cos_t, sin_t, ((0, (2 * width + iw) // tn + 1),), tn=tn)
    wi_pad = jnp.zeros((d, LANES - IDX_HEADS), w_in.dtype)
    wv = jnp.concatenate([v, wi, wi_pad], axis=1).astype(BF16)
    vscale = jnp.concatenate([jnp.ones((width,)), jnp.full((LANES,), IDX_HEADS ** -0.5)]).astype(F32)[None]
    vt = _proj_t(x, wv, vscale, groups=3)
    return _dsa_attention(h1, vt, n_heads=n_heads)


def _split_kernel(w_ref, perm_ref, g_ref, u_ref):
    half = g_ref.shape[2]
    both = _dot(w_ref[0, 0].astype(BF16), perm_ref[...])
    g_ref[0] = both[:, :half].astype(BF16)
    u_ref[0] = both[:, half:].astype(BF16)


def _split_gate_up(w1_all, layer, *, tc=512):
    _, n_e, d, f2 = w1_all.shape
    half = tc // 2
    src = jnp.concatenate([2 * jnp.arange(half), 2 * jnp.arange(half) + 1])
    perm = (jnp.arange(tc)[:, None] == src[None, :]).astype(BF16)
    out = jax.ShapeDtypeStruct((n_e, d, f2 // 2), BF16)
    return pl.pallas_call(
        _split_kernel,
        grid=(n_e, f2 // tc),
        in_specs=[pl.BlockSpec((1, 1, d, tc), lambda e, c: (layer, e, 0, c)),
                  pl.BlockSpec((tc, tc), lambda e, c: (0, 0))],
        out_specs=[pl.BlockSpec((1, d, half), lambda e, c: (e, 0, c)),
                   pl.BlockSpec((1, d, half), lambda e, c: (e, 0, c))],
        out_shape=[out, out],
        compiler_params=_cparams(("arbitrary", "arbitrary")),
        name="split_gate_up",
    )(w1_all, perm)


def _cast_kernel(w_ref, o_ref):
    o_ref[0] = w_ref[0, 0].astype(o_ref.dtype)


def _cast_layer(w_all, layer):
    _, n_e, f, d = w_all.shape
    return pl.pallas_call(
        _cast_kernel,
        grid=(n_e,),
        in_specs=[pl.BlockSpec((1, 1, f, d), lambda e: (layer, e, 0, 0))],
        out_specs=pl.BlockSpec((1, f, d), lambda e: (e, 0, 0)),
        out_shape=jax.ShapeDtypeStruct((n_e, f, d), BF16),
        compiler_params=_cparams(("arbitrary",)),
        name="cast_w2",
    )(w_all)


def _moe_weights(w_r, b_r, w1_all, b1, w2_all, b2, layer):
    wr_t = jnp.pad(w_r.T, ((0, LANES - N_EXPERTS), (0, 0)))
    br = jnp.pad(b_r, (0, LANES - N_EXPERTS))[:, None]
    w1g, w1u = _split_gate_up(w1_all, layer)
    return (wr_t, br, w1g, w1u, b1[:, None, 0::2], b1[:, None, 1::2], _cast_layer(w2_all, layer),
            b2[:, None, :])


def kernel(x, p, ab_w_in, ab_w_out, c_w_in, c_w_out, ln_g, ln_b, router_w, router_b,
           moe_w1, moe_b1, moe_w2, moe_b2, ple_w_proj, ple_w_gate):
    b, s, d = x.shape
    assert b == 1
    depth = p.shape[0]
    alpha = float((2 * depth) ** 0.25)
    cos_t, sin_t = _rope_tables(s, 4 * LANES)
    sb_heads = moba_heads = ab_w_out.shape[1] // (2 * HEAD_DIM)
    dsa_heads = c_w_out.shape[1] // HEAD_DIM
    xs = x[0]
    for i in range(depth):
        j = i // 2
        g = ln_g[i][:, None, :]
        bb = ln_b[i][:, None, :]
        if i % 2 == 0:
            o_a, o_b = _mixer_ab(xs, ab_w_in[j], cos_t, sin_t, sb_heads=sb_heads, moba_heads=moba_heads)
            w_out = ab_w_out[j].astype(BF16)
            wa = sb_heads * HEAD_DIM
            xs = _outproj_ln([o_a, o_b], [w_out[:wa], w_out[wa:]], xs, g[0], bb[0], alpha)
        else:
            o_c = _mixer_c(xs, c_w_in[j], cos_t, sin_t, n_heads=dsa_heads)
            xs = _outproj_ln([o_c], [c_w_out[j].astype(BF16)], xs, g[0], bb[0], alpha)
        mw = _moe_weights(router_w[i], router_b[i], moe_w1, moe_b1[i], moe_w2, moe_b2[i], i)
        xs = _moe_ln(xs, *mw, g[1], bb[1], p[i, 0], ple_w_gate[i].astype(BF16), ple_w_proj[i].astype(BF16),
                     alpha)
    return xs[None]
```
